```python
import jax, jax.numpy as jnp
from jax import lax
import numpy as np

D_MODEL = 4096
BATCH = 4
SEQ = 4096
DEPTH = 1

CHUNK = 64
Q_BLOCK = 128
PLE_DIM = 256
RET_HEADS = 8
RET_HEAD_DIM = D_MODEL // (2 * RET_HEADS)
RET_WIDTH = RET_HEADS * RET_HEAD_DIM
MLA_HEADS = 16
MLA_NOPE = 128
MLA_ROPE = 64
MLA_V = (D_MODEL - RET_WIDTH) // MLA_HEADS
Q_LORA = D_MODEL // 4
KV_LORA = D_MODEL // 8
D_FF = ((8 * D_MODEL // 3 + 255) // 256) * 256
CONV_WIDTH = 3
ROPE_BASE = 10000.0
EPS = 1e-6
IN_WIDTH = 4 * RET_WIDTH + Q_LORA + KV_LORA + MLA_ROPE
IN_SPLITS = (RET_WIDTH, 2 * RET_WIDTH, 3 * RET_WIDTH, 4 * RET_WIDTH,
             4 * RET_WIDTH + Q_LORA, 4 * RET_WIDTH + Q_LORA + KV_LORA)

kernel_name = "hybrid_retention_mla_convffn_ple"


def rms_norm(x, g):
    xf = x.astype(jnp.float32)
    y = xf * lax.rsqrt(jnp.mean(xf * xf, axis=-1, keepdims=True) + EPS)
    return (y * g.astype(jnp.float32)).astype(x.dtype)


def head_norm(o):
    of = o.astype(jnp.float32)
    mu = jnp.mean(of, axis=-1, keepdims=True)
    var = jnp.mean(jnp.square(of - mu), axis=-1, keepdims=True)
    return ((of - mu) * lax.rsqrt(var + EPS)).astype(o.dtype)


def rope_tables(seq, dim):
    inv = 1.0 / (ROPE_BASE ** (jnp.arange(0, dim, 2, dtype=jnp.float32) / dim))
    ang = jnp.arange(seq, dtype=jnp.float32)[:, None] * inv[None, :]
    return jnp.cos(ang), jnp.sin(ang)


def apply_rope(x, cos, sin):
    x1, x2 = jnp.split(x, 2, axis=-1)
    c = cos[None, :, None, :].astype(x.dtype)
    s = sin[None, :, None, :].astype(x.dtype)
    return jnp.concatenate([x1 * c - x2 * s, x2 * c + x1 * s], axis=-1)


def retention(q, k, v):
    B, S, H, dk = q.shape
    dv = v.shape[-1]
    nC = S // CHUNK
    dt = q.dtype
    log_g = jnp.log1p(-jnp.exp2(-5.0 - jnp.arange(H, dtype=jnp.float32)))
    idx = jnp.arange(CHUNK, dtype=jnp.float32)
    inner = jnp.exp(log_g[:, None, None] * jnp.abs(idx[:, None] - idx[None, :])).astype(dt)
    q_dec = jnp.exp(log_g[:, None] * (idx + 1.0))[..., None].astype(dt)
    k_dec = jnp.exp(log_g[:, None] * (CHUNK - 1.0 - idx))[..., None].astype(dt)
    s_dec = jnp.exp(log_g * CHUNK)[:, None, None].astype(dt)
    k = k * (dk ** -0.5)

    def to_chunks(t):
        return t.reshape(B, nC, CHUNK, H, t.shape[-1]).transpose(1, 0, 3, 2, 4)

    def step(state, qkv):
        qc, kc, vc = qkv
        scores = jnp.einsum('bhnd,bhmd->bhnm', qc, kc) * inner
        out = (jnp.einsum('bhnm,bhmv->bhnv', scores, vc)
               + jnp.einsum('bhnd,bhdv->bhnv', qc * q_dec, state))
        state = state * s_dec + jnp.einsum('bhmd,bhmv->bhdv', kc * k_dec, vc)
        return state, out

    s0 = jnp.zeros((B, H, dk, dv), dt)
    _, out = lax.scan(step, s0, (to_chunks(q), to_chunks(k), to_chunks(v)))
    return out.transpose(1, 0, 3, 2, 4).reshape(B, S, H, dv)


def mla_attention(q_nope, q_rope, k_nope, k_rope, v):
    B, S, H, _ = q_nope.shape
    nQ = S // Q_BLOCK
    scale = (MLA_NOPE + MLA_ROPE) ** -0.5
    key_chunk = jnp.arange(S) // CHUNK

    def blocks(t):
        return t.reshape(B, nQ, Q_BLOCK, *t.shape[2:]).swapaxes(0, 1)

    def one_block(args):
        qn, qr, b = args
        s = (jnp.einsum('bqhd,bkhd->bhqk', qn, k_nope)
             + jnp.einsum('bqhr,bkr->bhqk', qr, k_rope)).astype(jnp.float32) * scale
        q_chunk = (b * Q_BLOCK + jnp.arange(Q_BLOCK)) // CHUNK
        mask = key_chunk[None, :] <= q_chunk[:, None]
        s = jnp.where(mask[None, None], s, -jnp.inf)
        w = jax.nn.softmax(s, axis=-1).astype(v.dtype)
        return jnp.einsum('bhqk,bkhv->bqhv', w, v)

    out = lax.map(one_block, (blocks(q_nope), blocks(q_rope), jnp.arange(nQ)))
    return out.swapaxes(0, 1).reshape(B, S, H, v.shape[-1])


def causal_dwconv(h, w, b):
    S = h.shape[1]
    hp = jnp.pad(h, ((0, 0), (CONV_WIDTH - 1, 0), (0, 0)))
    out = b
    for j in range(CONV_WIDTH):
        out = out + hp[:, j:j + S, :] * w[j]
    return out


def setup_inputs(seed: int = 0) -> dict:
    key = jax.random.key(seed)
    ks = jax.random.split(key, 20)

    def nrm(k, shape, fan_in):
        return jax.random.normal(k, shape, jnp.float32) * (fan_in ** -0.5)

    def gain(k, shape):
        return 1.0 + 0.01 * jax.random.normal(k, shape, jnp.float32)

    L = DEPTH
    return {
        "x": jax.random.normal(ks[0], (BATCH, SEQ, D_MODEL), jnp.float32),
        "p": jax.random.normal(ks[1], (DEPTH, BATCH, SEQ, PLE_DIM), jnp.float32),
        "w_in": nrm(ks[2], (L, D_MODEL, IN_WIDTH), D_MODEL),
        "g_attn": gain(ks[3], (L, D_MODEL)),
        "g_q_lora": gain(ks[4], (L, Q_LORA)),
        "g_kv_lora": gain(ks[5], (L, KV_LORA)),
        "w_uq": nrm(ks[6], (L, Q_LORA, MLA_HEADS * (MLA_NOPE + MLA_ROPE)), Q_LORA),
        "w_ukv": nrm(ks[7], (L, KV_LORA, MLA_HEADS * (MLA_NOPE + MLA_V)), KV_LORA),
        "w_o": nrm(ks[8], (L, RET_WIDTH + MLA_HEADS * MLA_V, D_MODEL), D_MODEL),
        "g_ffn": gain(ks[9], (L, D_MODEL)),
        "w_ffn_gate": nrm(ks[10], (L, D_MODEL, D_FF), D_MODEL),
        "w_ffn_up": nrm(ks[11], (L, D_MODEL, D_FF), D_MODEL),
        "conv_w": nrm(ks[12], (L, CONV_WIDTH, D_FF), CONV_WIDTH),
        "conv_b": 0.01 * jax.random.normal(ks[13], (L, D_FF), jnp.float32),
        "w_ffn_down": nrm(ks[14], (L, D_FF, D_MODEL), D_FF),
        "g_ple": gain(ks[15], (L, D_MODEL)),
        "w_ple_gate": nrm(ks[16], (L, D_MODEL, D_MODEL), D_MODEL),
        "w_ple_proj": nrm(ks[17], (L, PLE_DIM, D_MODEL), PLE_DIM),
        "g_final": gain(ks[18], (D_MODEL,)),
    }


def reference(x, p, w_in, g_attn, g_q_lora, g_kv_lora, w_uq, w_ukv, w_o, g_ffn,
              w_ffn_gate, w_ffn_up, conv_w, conv_b, w_ffn_down, g_ple,
              w_ple_gate, w_ple_proj, g_final):
    B, S, _ = x.shape
    cos_r, sin_r = rope_tables(S, RET_HEAD_DIM)
    cos_m, sin_m = rope_tables(S, MLA_ROPE)
    h = x
    for i in range(DEPTH):
        hn = rms_norm(h, g_attn[i])
        proj = hn @ w_in[i]
        rq, rk, rv, rg, cq, ckv, kr = jnp.split(proj, IN_SPLITS, axis=-1)

        rq = apply_rope(rq.reshape(B, S, RET_HEADS, RET_HEAD_DIM), cos_r, sin_r)
        rk = apply_rope(rk.reshape(B, S, RET_HEADS, RET_HEAD_DIM), cos_r, sin_r)
        rv = rv.reshape(B, S, RET_HEADS, RET_HEAD_DIM)
        ro = head_norm(retention(rq, rk, rv)).reshape(B, S, RET_WIDTH)
        ro = jax.nn.silu(rg) * ro

        q = (rms_norm(cq, g_q_lora[i]) @ w_uq[i]).reshape(B, S, MLA_HEADS, MLA_NOPE + MLA_ROPE)
        q_nope = q[..., :MLA_NOPE]
        q_rope = apply_rope(q[..., MLA_NOPE:], cos_m, sin_m)
        kv = (rms_norm(ckv, g_kv_lora[i]) @ w_ukv[i]).reshape(B, S, MLA_HEADS, MLA_NOPE + MLA_V)
        k_nope = kv[..., :MLA_NOPE]
        v = kv[..., MLA_NOPE:]
        k_rope = apply_rope(kr[:, :, None, :], cos_m, sin_m)[:, :, 0, :]
        mo = mla_attention(q_nope, q_rope, k_nope, k_rope, v).reshape(B, S, MLA_HEADS * MLA_V)

        h = h + jnp.concatenate([ro, mo], axis=-1) @ w_o[i]

        hn = rms_norm(h, g_ffn[i])
        a = causal_dwconv(hn @ w_ffn_gate[i], conv_w[i], conv_b[i])
        h = h + (jax.nn.silu(a) * (hn @ w_ffn_up[i])) @ w_ffn_down[i]

        gate = jax.nn.sigmoid(rms_norm(h, g_ple[i]) @ w_ple_gate[i])
        h = h + gate * (p[i] @ w_ple_proj[i])
    return rms_norm(h, g_final)
```

```python
import functools
import math

import jax
import jax.numpy as jnp
from jax import lax
from jax.experimental import pallas as pl
from jax.experimental.pallas import tpu as pltpu

F32 = jnp.float32
BF16 = jnp.bfloat16

EPS = 1e-6
ROPE_BASE = 10000.0
CHUNK = 64
CHUNK_SHIFT = 6
RET_HEADS = 8
RET_HEAD_DIM = 256
MLA_HEADS = 16
MLA_NOPE = 128
MLA_ROPE = 64
MLA_V = 128
MLA_QK_PAD = 256
CONV_WIDTH = 3
RET_BLOCK = 256
LANES = 128
CONV_HALO = 8
MIB = 1024 * 1024
VMEM_LIMIT_CAP = 60 * MIB


def _vmem_limit(nbytes):
    return int(min(VMEM_LIMIT_CAP, nbytes + 16 * MIB))


def _nbytes(shape, dtype):
    return math.prod(shape) * jnp.dtype(dtype).itemsize


def _rms_rows(x_ref, g_ref, o_ref, n_rows, row_chunk=16):
    g = g_ref[...]

    def body(c, carry):
        r = pl.multiple_of(c * row_chunk, row_chunk)
        x = x_ref[pl.ds(r, row_chunk), :].astype(F32)
        ms = jnp.mean(x * x, axis=-1, keepdims=True)
        o_ref[pl.ds(r, row_chunk), :] = ((x * lax.rsqrt(ms + EPS)) * g).astype(o_ref.dtype)
        return carry

    lax.fori_loop(0, n_rows // row_chunk, body, 0)


def _silu(x):
    return x * (1.0 / (1.0 + jnp.exp(-x)))


def _norm_matmul_kernel(x_ref, g_ref, w_ref, o_ref, xn_ref):
    @pl.when(pl.program_id(1) == 0)
    def _():
        _rms_rows(x_ref, g_ref, xn_ref, x_ref.shape[0])

    o_ref[...] = jnp.dot(xn_ref[...], w_ref[...],
                         preferred_element_type=F32).astype(o_ref.dtype)


def norm_matmul(x, g, w, *, tm, tn, out_dtype):
    T, K = x.shape
    N = w.shape[1]
    est = (2 * _nbytes((tm, K), x.dtype) + _nbytes((tm, K), BF16)
           + 2 * _nbytes((K, tn), w.dtype) + 2 * _nbytes((tm, tn), out_dtype)
           + _nbytes((tm, tn), F32))
    return pl.pallas_call(
        _norm_matmul_kernel,
        grid=(T // tm, N // tn),
        in_specs=[pl.BlockSpec((tm, K), lambda i, j: (i, 0)),
                  pl.BlockSpec((1, K), lambda i, j: (0, 0)),
                  pl.BlockSpec((K, tn), lambda i, j: (0, j))],
        out_specs=pl.BlockSpec((tm, tn), lambda i, j: (i, j)),
        out_shape=jax.ShapeDtypeStruct((T, N), out_dtype),
        scratch_shapes=[pltpu.VMEM((tm, K), BF16)],
        compiler_params=pltpu.CompilerParams(
            dimension_semantics=("parallel", "arbitrary"),
            vmem_limit_bytes=_vmem_limit(est)),
        name="in_proj",
    )(x, g, w)


def _rmsnorm_kernel(x_ref, g_ref, o_ref):
    _rms_rows(x_ref, g_ref, o_ref, x_ref.shape[0])


def rmsnorm_bf16(x, g, *, tm):
    T, K = x.shape
    est = 2 * _nbytes((tm, K), x.dtype) + 2 * _nbytes((tm, K), BF16)
    return pl.pallas_call(
        _rmsnorm_kernel,
        grid=(T // tm,),
        in_specs=[pl.BlockSpec((tm, K), lambda i: (i, 0)),
                  pl.BlockSpec((1, K), lambda i: (0, 0))],
        out_specs=pl.BlockSpec((tm, K), lambda i: (i, 0)),
        out_shape=jax.ShapeDtypeStruct((T, K), BF16),
        compiler_params=pltpu.CompilerParams(
            dimension_semantics=("parallel",),
            vmem_limit_bytes=_vmem_limit(est)),
        name="rmsnorm",
    )(x, g)


def _retention_kernel(lg_ref, q_ref, k_ref, v_ref, gate_ref, cos_ref, sin_ref, o_ref,
                      state_ref, dmat_ref, qdec_ref, kdec_ref, sdec_ref, *, k_scale):
    L = RET_BLOCK
    dk = q_ref.shape[1]
    half = dk // 2
    lg = lg_ref[pl.program_id(1)]

    @pl.when(pl.program_id(2) == 0)
    def _init():
        state_ref[...] = jnp.zeros_like(state_ref)
        n = lax.broadcasted_iota(jnp.int32, (L, L), 0)
        m = lax.broadcasted_iota(jnp.int32, (L, L), 1)
        cn = n >> CHUNK_SHIFT
        cm = m >> CHUNK_SHIFT
        d = (n - m).astype(F32)
        expo = jnp.where(cn == cm, jnp.abs(d), d)
        visible = cm <= cn
        dmat_ref[...] = jnp.where(visible, jnp.exp(lg * jnp.where(visible, expo, 0.0)), 0.0)
        row = lax.broadcasted_iota(jnp.int32, (L, dk), 0).astype(F32)
        qdec_ref[...] = jnp.exp(lg * (row + 1.0))
        kdec_ref[...] = jnp.exp(lg * (float(L - 1) - row))
        sdec_ref[...] = jnp.exp(jnp.full(sdec_ref.shape, lg * float(L), F32))

    def rope(x, cos, sin):
        x1 = x[:, :half]
        x2 = x[:, half:]
        return jnp.concatenate([x1 * cos - x2 * sin, x2 * cos + x1 * sin], axis=1)

    for sub in range(q_ref.shape[0] // L):
        rows = pl.ds(sub * L, L)
        cos = cos_ref[rows, :]
        sin = sin_ref[rows, :]
        q = rope(q_ref[rows, :].astype(F32), cos, sin)
        k = rope(k_ref[rows, :].astype(F32), cos, sin) * k_scale
        v = v_ref[rows, :]
        qb = q.astype(BF16)
        kb = k.astype(BF16)
        scores = lax.dot_general(qb, kb, (((1,), (1,)), ((), ())),
                                 preferred_element_type=F32)
        scores = (scores * dmat_ref[...]).astype(BF16)
        state = state_ref[...]
        out = (jnp.dot(scores, v, preferred_element_type=F32)
               + jnp.dot((q * qdec_ref[...]).astype(BF16), state.astype(BF16),
                         preferred_element_type=F32))
        kd = (k * kdec_ref[...]).astype(BF16)
        state_ref[...] = state * sdec_ref[0:1, :] + lax.dot_general(
            kd, v, (((0,), (0,)), ((), ())), preferred_element_type=F32)
        mu = jnp.mean(out, axis=-1, keepdims=True)
        cen = out - mu
        var = jnp.mean(cen * cen, axis=-1, keepdims=True)
        gate = gate_ref[rows, :].astype(F32)
        o_ref[rows, :] = (_silu(gate) * (cen * lax.rsqrt(var + EPS))).astype(o_ref.dtype)


def retention_group(proj, log_g, cos, sin, *, batch, seq, rows):
    T = proj.shape[0]
    H, dk = RET_HEADS, RET_HEAD_DIM
    nblk = seq // rows

    def col(offset):
        return pl.BlockSpec((rows, dk), lambda b, h, r: (b * nblk + r, offset + h))

    tab = pl.BlockSpec((rows, dk // 2), lambda b, h, r: (r, 0))
    est = 10 * _nbytes((rows, dk), BF16) + 4 * _nbytes((rows, dk // 2), F32) \
        + 4 * _nbytes((RET_BLOCK, dk), F32)
    return pl.pallas_call(
        functools.partial(_retention_kernel, k_scale=dk ** -0.5),
        grid=(batch, H, nblk),
        in_specs=[pl.BlockSpec(memory_space=pltpu.SMEM),
                  col(0), col(H), col(2 * H), col(3 * H), tab, tab],
        out_specs=pl.BlockSpec((rows, dk), lambda b, h, r: (b * nblk + r, h)),
        out_shape=jax.ShapeDtypeStruct((T, H * dk), BF16),
        scratch_shapes=[pltpu.VMEM((dk, dk), F32),
                        pltpu.VMEM((RET_BLOCK, RET_BLOCK), F32),
                        pltpu.VMEM((RET_BLOCK, dk), F32),
                        pltpu.VMEM((RET_BLOCK, dk), F32),
                        pltpu.VMEM((8, dk), F32)],
        compiler_params=pltpu.CompilerParams(
            dimension_semantics=("parallel", "parallel", "arbitrary"),
            vmem_limit_bytes=_vmem_limit(est)),
        name="retention",
    )(log_g, proj, proj, proj, proj, cos, sin)


def _mla_proj_kernel(cq_ref, ckv_ref, kr_ref, gq_ref, gkv_ref, wq_ref, wk_ref, wv_ref,
                     rc_ref, rn_ref, rp_ref, q_ref, k_ref, v_ref, cqn_ref, ckvn_ref,
                     *, scale):
    tm = cq_ref.shape[0]
    _rms_rows(cq_ref, gq_ref, cqn_ref, tm)
    _rms_rows(ckv_ref, gkv_ref, ckvn_ref, tm)
    rc = rc_ref[...]
    rn = rn_ref[...]
    rp = rp_ref[...]

    def rope(x):
        return (x * rc + pltpu.roll(x, LANES - MLA_ROPE // 2, 1) * rn
                + pltpu.roll(x, MLA_ROPE // 2, 1) * rp)

    krp = rope(kr_ref[:, :LANES].astype(F32)).astype(k_ref.dtype)
    cqn = cqn_ref[...]
    ckvn = ckvn_ref[...]
    W = MLA_QK_PAD
    for h in range(MLA_HEADS):
        qh = jnp.dot(cqn, wq_ref[:, h * W:(h + 1) * W], preferred_element_type=F32) * scale
        q_ref[:, h * W:h * W + LANES] = qh[:, :LANES].astype(q_ref.dtype)
        q_ref[:, h * W + LANES:(h + 1) * W] = rope(qh[:, LANES:]).astype(q_ref.dtype)
    for c in range(MLA_HEADS // 2):
        kn = jnp.dot(ckvn, wk_ref[:, c * W:(c + 1) * W], preferred_element_type=F32)
        for s in range(2):
            h = 2 * c + s
            k_ref[:, h * W:h * W + LANES] = kn[:, s * LANES:(s + 1) * LANES].astype(k_ref.dtype)
            k_ref[:, h * W + LANES:(h + 1) * W] = krp
        v_ref[:, c * W:(c + 1) * W] = jnp.dot(
            ckvn, wv_ref[:, c * W:(c + 1) * W], preferred_element_type=F32).astype(v_ref.dtype)


def mla_projections(proj, g_q, g_kv, wq, wk, wv, rope_c, rope_n, rope_p, *, seq, tm,
                    cq_blk, ckv_blk, kr_blk):
    T = proj.shape[0]
    q_lora = wq.shape[0]
    kv_lora = wk.shape[0]
    nq = wq.shape[1]
    nv = wv.shape[1]
    nblk = seq // tm
    scale = (MLA_NOPE + MLA_ROPE) ** -0.5
    const = lambda i: (0, 0)
    tab = pl.BlockSpec((tm, LANES), lambda i: (i % nblk, 0))
    single = dict(pipeline_mode=pl.Buffered(1))
    est = (_nbytes(wq.shape, BF16) + _nbytes(wk.shape, BF16) + _nbytes(wv.shape, BF16)
           + 2 * (2 * _nbytes((tm, nq), BF16) + _nbytes((tm, nv), BF16))
           + 3 * _nbytes((tm, q_lora + kv_lora + 2 * LANES), BF16)
           + 6 * _nbytes((tm, LANES), F32) + 4 * _nbytes((tm, MLA_QK_PAD), F32))
    return pl.pallas_call(
        functools.partial(_mla_proj_kernel, scale=scale),
        grid=(T // tm,),
        in_specs=[pl.BlockSpec((tm, q_lora), lambda i: (i, cq_blk)),
                  pl.BlockSpec((tm, kv_lora), lambda i: (i, ckv_blk)),
                  pl.BlockSpec((tm, 2 * LANES), lambda i: (i, kr_blk)),
                  pl.BlockSpec((1, q_lora), const),
                  pl.BlockSpec((1, kv_lora), const),
                  pl.BlockSpec(wq.shape, const, **single),
                  pl.BlockSpec(wk.shape, const, **single),
                  pl.BlockSpec(wv.shape, const, **single),
                  tab, tab, tab],
        out_specs=[pl.BlockSpec((tm, nq), lambda i: (i, 0)),
                   pl.BlockSpec((tm, nq), lambda i: (i, 0)),
                   pl.BlockSpec((tm, nv), lambda i: (i, 0))],
        out_shape=[jax.ShapeDtypeStruct((T, nq), BF16),
                   jax.ShapeDtypeStruct((T, nq), BF16),
                   jax.ShapeDtypeStruct((T, nv), BF16)],
        scratch_shapes=[pltpu.VMEM((tm, q_lora), BF16),
                        pltpu.VMEM((tm, kv_lora), BF16)],
        compiler_params=pltpu.CompilerParams(
            dimension_semantics=("parallel",),
            vmem_limit_bytes=_vmem_limit(est)),
        name="mla_proj",
    )(proj, proj, proj, g_q, g_kv, wq, wk, wv, rope_c, rope_n, rope_p)


def _mla_attn_kernel(q_ref, k_ref, v_ref, o_ref, *, tile):
    qi = pl.program_id(2)
    q = q_ref[...]
    dv = v_ref.shape[1]

    def step(start, mask, carry):
        m, l, acc = carry
        k = k_ref[pl.ds(start, tile), :]
        v = v_ref[pl.ds(start, tile), :]
        s = lax.dot_general(q, k, (((1,), (1,)), ((), ())), preferred_element_type=F32)
        if mask is not None:
            s = jnp.where(mask, s, -jnp.inf)
        m_new = jnp.maximum(m, jnp.max(s, axis=-1, keepdims=True))
        alpha = jnp.exp(m - m_new)
        p = jnp.exp(s - m_new)
        l = alpha * l + jnp.sum(p, axis=-1, keepdims=True)
        acc = alpha * acc + jnp.dot(p.astype(v.dtype), v, preferred_element_type=F32)
        return m_new, l, acc

    init = (jnp.full((tile, 1), -jnp.inf, F32), jnp.zeros((tile, 1), F32),
            jnp.zeros((tile, dv), F32))
    carry = lax.fori_loop(
        0, qi, lambda j, c: step(pl.multiple_of(j * tile, tile), None, c), init)
    row_chunk = lax.broadcasted_iota(jnp.int32, (tile, tile), 0) >> CHUNK_SHIFT
    col_chunk = lax.broadcasted_iota(jnp.int32, (tile, tile), 1) >> CHUNK_SHIFT
    _, l, acc = step(pl.multiple_of(qi * tile, tile), col_chunk <= row_chunk, carry)
    o_ref[...] = (acc * (1.0 / l)).astype(o_ref.dtype)


def mla_attention(qc, kc, v, *, batch, seq, tile):
    T = qc.shape[0]
    H, W, dv = MLA_HEADS, MLA_QK_PAD, MLA_V
    nq = seq // tile
    est = (2 * _nbytes((seq, W), BF16) + 2 * _nbytes((seq, dv), BF16)
           + 2 * _nbytes((tile, W), BF16) + 2 * _nbytes((tile, dv), BF16)
           + 6 * _nbytes((tile, tile), F32))
    return pl.pallas_call(
        functools.partial(_mla_attn_kernel, tile=tile),
        grid=(batch, H, nq),
        in_specs=[pl.BlockSpec((tile, W), lambda b, h, i: (b * nq + i, h)),
                  pl.BlockSpec((seq, W), lambda b, h, i: (b, h)),
                  pl.BlockSpec((seq, dv), lambda b, h, i: (b, h))],
        out_specs=pl.BlockSpec((tile, dv), lambda b, h, i: (b * nq + i, h)),
        out_shape=jax.ShapeDtypeStruct((T, H * dv), BF16),
        compiler_params=pltpu.CompilerParams(
            dimension_semantics=("parallel", "parallel", "arbitrary"),
            vmem_limit_bytes=_vmem_limit(est)),
        name="mla_attn",
    )(qc, kc, v)


def _out_proj_kernel(ro_ref, mo_ref, wr_ref, wm_ref, x_ref, o_ref):
    acc = jnp.dot(ro_ref[...], wr_ref[...], preferred_element_type=F32)
    acc = acc + jnp.dot(mo_ref[...], wm_ref[...], preferred_element_type=F32)
    o_ref[...] = x_ref[...] + acc


def out_projection(ro, mo, w_o, x, *, tm, tn):
    T, kr = ro.shape
    km = mo.shape[1]
    assert kr == km
    N = w_o.shape[1]
    est = (4 * _nbytes((tm, kr), BF16) + 4 * _nbytes((kr, tn), BF16)
           + 5 * _nbytes((tm, tn), F32))
    return pl.pallas_call(
        _out_proj_kernel,
        grid=(T // tm, N // tn),
        in_specs=[pl.BlockSpec((tm, kr), lambda i, j: (i, 0)),
                  pl.BlockSpec((tm, km), lambda i, j: (i, 0)),
                  pl.BlockSpec((kr, tn), lambda i, j: (0, j)),
                  pl.BlockSpec((km, tn), lambda i, j: (1, j)),
                  pl.BlockSpec((tm, tn), lambda i, j: (i, j))],
        out_specs=pl.BlockSpec((tm, tn), lambda i, j: (i, j)),
        out_shape=jax.ShapeDtypeStruct((T, N), F32),
        compiler_params=pltpu.CompilerParams(
            dimension_semantics=("parallel", "arbitrary"),
            vmem_limit_bytes=_vmem_limit(est)),
        name="out_proj",
    )(ro, mo, w_o, w_o, x)


def _ffn_up_kernel(hn_ref, wg_ref, wu_ref, cw_ref, cb_ref, o_ref, g_ref, *, tiles_per_seq):
    tm = hn_ref.shape[0]
    halo = CONV_HALO

    @pl.when(pl.program_id(1) % tiles_per_seq == 0)
    def _():
        g_ref[0:halo, :] = jnp.zeros((halo, g_ref.shape[1]), F32)

    hn = hn_ref[...]
    g_ref[halo:halo + tm, :] = jnp.dot(hn, wg_ref[...], preferred_element_type=F32)
    up = jnp.dot(hn, wu_ref[...], preferred_element_type=F32)
    a = cb_ref[...]
    for j in range(CONV_WIDTH):
        shift = CONV_WIDTH - 1 - j
        a = a + g_ref[halo - shift:halo - shift + tm, :] * cw_ref[j:j + 1, :]
    o_ref[...] = (_silu(a) * up).astype(o_ref.dtype)
    g_ref[0:halo, :] = g_ref[tm:tm + halo, :]


def ffn_up(hn, w_gate, w_up, conv_w, conv_b, *, seq, tm, tf):
    T, K = hn.shape
    FF = w_gate.shape[1]
    est = (2 * _nbytes((tm, K), BF16) + 4 * _nbytes((K, tf), BF16)
           + 2 * _nbytes((tm, tf), BF16) + 4 * _nbytes((tm + CONV_HALO, tf), F32))
    return pl.pallas_call(
        functools.partial(_ffn_up_kernel, tiles_per_seq=seq // tm),
        grid=(FF // tf, T // tm),
        in_specs=[pl.BlockSpec((tm, K), lambda j, i: (i, 0)),
                  pl.BlockSpec((K, tf), lambda j, i: (0, j)),
                  pl.BlockSpec((K, tf), lambda j, i: (0, j)),
                  pl.BlockSpec((CONV_WIDTH, tf), lambda j, i: (0, j)),
                  pl.BlockSpec((1, tf), lambda j, i: (0, j))],
        out_specs=pl.BlockSpec((tm, tf), lambda j, i: (i, j)),
        out_shape=jax.ShapeDtypeStruct((T, FF), BF16),
        scratch_shapes=[pltpu.VMEM((tm + CONV_HALO, tf), F32)],
        compiler_params=pltpu.CompilerParams(
            dimension_semantics=("arbitrary", "arbitrary"),
            vmem_limit_bytes=_vmem_limit(est)),
        name="ffn_up",
    )(hn, w_gate, w_up, conv_w, conv_b)


def _ffn_down_kernel(a_ref, w_ref, h_ref, o_ref):
    o_ref[...] = h_ref[...] + jnp.dot(a_ref[...], w_ref[...], preferred_element_type=F32)


def ffn_down(hidden, w_down, h, *, tm, tn):
    T, K = hidden.shape
    N = w_down.shape[1]
    est = (2 * _nbytes((tm, K), BF16) + 2 * _nbytes((K, tn), BF16)
           + 5 * _nbytes((tm, tn), F32))
    return pl.pallas_call(
        _ffn_down_kernel,
        grid=(T // tm, N // tn),
        in_specs=[pl.BlockSpec((tm, K), lambda i, j: (i, 0)),
                  pl.BlockSpec((K, tn), lambda i, j: (0, j)),
                  pl.BlockSpec((tm, tn), lambda i, j: (i, j))],
        out_specs=pl.BlockSpec((tm, tn), lambda i, j: (i, j)),
        out_shape=jax.ShapeDtypeStruct((T, N), F32),
        compiler_params=pltpu.CompilerParams(
            dimension_semantics=("parallel", "arbitrary"),
            vmem_limit_bytes=_vmem_limit(est)),
        name="ffn_down",
    )(hidden, w_down, h)


def _ple_kernel(h_ref, g_ref, wg_ref, p_ref, wp_ref, gf_ref, o_ref, hn_ref):
    j = pl.program_id(1)
    tm = h_ref.shape[0]
    tn = wg_ref.shape[1]

    @pl.when(j == 0)
    def _():
        _rms_rows(h_ref, g_ref, hn_ref, tm)

    z = jnp.dot(hn_ref[...], wg_ref[...], preferred_element_type=F32)
    gate = 1.0 / (1.0 + jnp.exp(-z))
    emb = jnp.dot(p_ref[...].astype(BF16), wp_ref[...], preferred_element_type=F32)
    delta = gate * emb
    for jj in range(h_ref.shape[1] // tn):
        @pl.when(j == jj)
        def _(jj=jj):
            cols = slice(jj * tn, (jj + 1) * tn)
            o_ref[:, cols] = h_ref[:, cols] + delta

    @pl.when(j == pl.num_programs(1) - 1)
    def _():
        _rms_rows(o_ref, gf_ref, o_ref, tm)


def ple_final(h, g_ple, w_gate, p, w_proj, g_final, *, tm, tn):
    T, D = h.shape
    P = p.shape[1]
    est = (4 * _nbytes((tm, D), F32) + _nbytes((tm, D), BF16)
           + 2 * _nbytes((D, tn), BF16) + 2 * _nbytes((P, tn), BF16)
           + 2 * _nbytes((tm, P), F32) + 4 * _nbytes((tm, tn), F32))
    return pl.pallas_call(
        _ple_kernel,
        grid=(T // tm, D // tn),
        in_specs=[pl.BlockSpec((tm, D), lambda i, j: (i, 0)),
                  pl.BlockSpec((1, D), lambda i, j: (0, 0)),
                  pl.BlockSpec((D, tn), lambda i, j: (0, j)),
                  pl.BlockSpec((tm, P), lambda i, j: (i, 0)),
                  pl.BlockSpec((P, tn), lambda i, j: (0, j)),
                  pl.BlockSpec((1, D), lambda i, j: (0, 0))],
        out_specs=pl.BlockSpec((tm, D), lambda i, j: (i, 0)),
        out_shape=jax.ShapeDtypeStruct((T, D), F32),
        scratch_shapes=[pltpu.VMEM((tm, D), BF16)],
        compiler_params=pltpu.CompilerParams(
            dimension_semantics=("parallel", "arbitrary"),
            vmem_limit_bytes=_vmem_limit(est)),
        name="ple_final",
    )(h, g_ple, w_gate, p, w_proj, g_final)


def _rope_tables(seq, dim):
    inv = 1.0 / (ROPE_BASE ** (jnp.arange(0, dim, 2, dtype=F32) / dim))
    ang = jnp.arange(seq, dtype=F32)[:, None] * inv[None, :]
    return jnp.cos(ang), jnp.sin(ang)


def _pad_cols(w, n):
    return jnp.pad(w, ((0, 0), (0, n - w.shape[1])))


def _tile_config(seq):
    return dict(
        in_proj=dict(tm=min(512, seq), tn=768),
        retention_rows=min(1024, seq),
        mla_proj_tm=min(512, seq),
        attn_tile=min(512, seq),
        out_proj=dict(tm=min(1024, seq), tn=512),
        norm_tm=min(256, seq),
        ffn_up=dict(tm=min(1024, seq), tf=512),
        ffn_down=dict(tm=min(512, seq), tn=256),
        ple=dict(tm=min(512, seq), tn=512),
    )


def _layer(h, p_i, w_in, g_attn, g_q_lora, g_kv_lora, w_uq, w_ukv, w_o, g_ffn,
           w_ffn_gate, w_ffn_up, conv_w, conv_b, w_ffn_down, g_ple, w_ple_gate,
           w_ple_proj, g_out, *, batch, seq):
    cfg = _tile_config(seq)
    D = h.shape[1]
    ret_w = RET_HEADS * RET_HEAD_DIM
    q_lora = w_uq.shape[0]
    kv_lora = w_ukv.shape[0]
    d_ff = w_ffn_gate.shape[1]

    in_w = w_in.shape[1]
    tn_in = cfg["in_proj"]["tn"]
    in_pad = -(-in_w // tn_in) * tn_in
    w_in_b = _pad_cols(w_in, in_pad).astype(BF16)
    proj = norm_matmul(h, g_attn.reshape(1, D), w_in_b, out_dtype=BF16, **cfg["in_proj"])

    log_g = jnp.log1p(-jnp.exp2(-5.0 - jnp.arange(RET_HEADS, dtype=F32)))
    cos_r, sin_r = _rope_tables(seq, RET_HEAD_DIM)
    ro = retention_group(proj, log_g, cos_r, sin_r, batch=batch, seq=seq,
                         rows=cfg["retention_rows"])

    hq = MLA_NOPE + MLA_ROPE
    wq = w_uq.reshape(q_lora, MLA_HEADS, hq)
    wq = jnp.pad(wq, ((0, 0), (0, 0), (0, MLA_QK_PAD - hq)))
    wq = wq.reshape(q_lora, MLA_HEADS * MLA_QK_PAD).astype(BF16)
    wkv = w_ukv.reshape(kv_lora, MLA_HEADS, MLA_NOPE + MLA_V)
    wk = wkv[:, :, :MLA_NOPE].reshape(kv_lora, MLA_HEADS * MLA_NOPE).astype(BF16)
    wv = wkv[:, :, MLA_NOPE:].reshape(kv_lora, MLA_HEADS * MLA_V).astype(BF16)
    cos_m, sin_m = _rope_tables(seq, MLA_ROPE)
    zeros = jnp.zeros_like(cos_m)
    fill = jnp.zeros((seq, LANES - MLA_ROPE), F32)
    rope_c = jnp.concatenate([cos_m, cos_m, fill], axis=1)
    rope_n = jnp.concatenate([-sin_m, zeros, fill], axis=1)
    rope_p = jnp.concatenate([zeros, sin_m, fill], axis=1)
    cq_off = 4 * ret_w
    ckv_off = cq_off + q_lora
    kr_off = ckv_off + kv_lora
    assert cq_off % q_lora == 0 and ckv_off % kv_lora == 0 and kr_off % (2 * LANES) == 0
    assert in_pad - kr_off >= 2 * LANES
    qc, kc, vv = mla_projections(
        proj, g_q_lora.reshape(1, q_lora), g_kv_lora.reshape(1, kv_lora), wq, wk, wv,
        rope_c, rope_n, rope_p, seq=seq, tm=cfg["mla_proj_tm"],
        cq_blk=cq_off // q_lora, ckv_blk=ckv_off // kv_lora, kr_blk=kr_off // (2 * LANES))
    mo = mla_attention(qc, kc, vv, batch=batch, seq=seq, tile=cfg["attn_tile"])

    h1 = out_projection(ro, mo, w_o.astype(BF16), h, **cfg["out_proj"])

    tf = cfg["ffn_up"]["tf"]
    ff_pad = -(-d_ff // tf) * tf
    hn = rmsnorm_bf16(h1, g_ffn.reshape(1, D), tm=cfg["norm_tm"])
    hidden = ffn_up(hn, _pad_cols(w_ffn_gate, ff_pad).astype(BF16),
                    _pad_cols(w_ffn_up, ff_pad).astype(BF16),
                    _pad_cols(conv_w, ff_pad), _pad_cols(conv_b.reshape(1, d_ff), ff_pad),
                    seq=seq, **cfg["ffn_up"])
    w_down = jnp.pad(w_ffn_down, ((0, ff_pad - d_ff), (0, 0))).astype(BF16)
    h2 = ffn_down(hidden, w_down, h1, **cfg["ffn_down"])

    return ple_final(h2, g_ple.reshape(1, D), w_ple_gate.astype(BF16), p_i,
                     w_ple_proj.astype(BF16), g_out.reshape(1, D), **cfg["ple"])


def kernel(x, p, w_in, g_attn, g_q_lora, g_kv_lora, w_uq, w_ukv, w_o, g_ffn, w_ffn_gate,
           w_ffn_up, conv_w, conv_b, w_ffn_down, g_ple, w_ple_gate, w_ple_proj, g_final):
    B, S, D = x.shape
    depth = p.shape[0]
    assert depth == 1, "the final RMSNorm is fused into the layer's last kernel"
    h = x.reshape(B * S, D)
    out = _layer(h, p[0].reshape(B * S, -1), w_in[0], g_attn[0], g_q_lora[0], g_kv_lora[0],
                 w_uq[0], w_ukv[0], w_o[0], g_ffn[0], w_ffn_gate[0], w_ffn_up[0], conv_w[0],
                 conv_b[0], w_ffn_down[0], g_ple[0], w_ple_gate[0], w_ple_proj[0], g_final,
                 batch=B, seq=S)
    return out.reshape(B, S, D)
```

```python
import functools
import math

import jax
import jax.numpy as jnp
from jax import lax
from jax.experimental import pallas as pl
from jax.experimental.pallas import tpu as pltpu

F32 = jnp.float32
BF16 = jnp.bfloat16

EPS = 1e-6
ROPE_BASE = 10000.0
CHUNK = 64
CHUNK_SHIFT = 6
RET_HEADS = 8
RET_HEAD_DIM = 256
MLA_HEADS = 16
MLA_NOPE = 128
MLA_ROPE = 64
MLA_V = 128
MLA_QK_PAD = 256
CONV_WIDTH = 3
RET_BLOCK = 256
LANES = 128
CONV_HALO = 8
MIB = 1024 * 1024
VMEM_LIMIT_CAP = 60 * MIB


def _vmem_limit(nbytes):
    return int(min(VMEM_LIMIT_CAP, nbytes + 16 * MIB))


def _nbytes(shape, dtype):
    return math.prod(shape) * jnp.dtype(dtype).itemsize


def _rms_rows(x_ref, g_ref, o_ref, n_rows, row_chunk=16):
    g = g_ref[...]

    def body(c, carry):
        r = pl.multiple_of(c * row_chunk, row_chunk)
        x = x_ref[pl.ds(r, row_chunk), :].astype(F32)
        ms = jnp.mean(x * x, axis=-1, keepdims=True)
        o_ref[pl.ds(r, row_chunk), :] = ((x * lax.rsqrt(ms + EPS)) * g).astype(o_ref.dtype)
        return carry

    lax.fori_loop(0, n_rows // row_chunk, body, 0)


def _silu(x):
    return x * (1.0 / (1.0 + jnp.exp(-x)))


def _norm_matmul_kernel(x_ref, g_ref, w_ref, o_ref, xn_ref):
    @pl.when(pl.program_id(1) == 0)
    def _():
        _rms_rows(x_ref, g_ref, xn_ref, x_ref.shape[0])

    o_ref[...] = jnp.dot(xn_ref[...], w_ref[...],
                         preferred_element_type=F32).astype(o_ref.dtype)


def norm_matmul(x, g, w, *, n_out, tm, tn, out_dtype):
    T, K = x.shape
    N = n_out
    est = (2 * _nbytes((tm, K), x.dtype) + _nbytes((tm, K), BF16)
           + 2 * _nbytes((K, tn), w.dtype) + 2 * _nbytes((tm, tn), out_dtype)
           + _nbytes((tm, tn), F32))
    return pl.pallas_call(
        _norm_matmul_kernel,
        grid=(T // tm, N // tn),
        in_specs=[pl.BlockSpec((tm, K), lambda i, j: (i, 0)),
                  pl.BlockSpec((1, K), lambda i, j: (0, 0)),
                  pl.BlockSpec((K, tn), lambda i, j: (0, j))],
        out_specs=pl.BlockSpec((tm, tn), lambda i, j: (i, j)),
        out_shape=jax.ShapeDtypeStruct((T, N), out_dtype),
        scratch_shapes=[pltpu.VMEM((tm, K), BF16)],
        compiler_params=pltpu.CompilerParams(
            dimension_semantics=("parallel", "arbitrary"),
            vmem_limit_bytes=_vmem_limit(est)),
        name="in_proj",
    )(x, g, w)


def _rmsnorm_kernel(x_ref, g_ref, o_ref):
    _rms_rows(x_ref, g_ref, o_ref, x_ref.shape[0])


def rmsnorm_bf16(x, g, *, tm):
    T, K = x.shape
    est = 2 * _nbytes((tm, K), x.dtype) + 2 * _nbytes((tm, K), BF16)
    return pl.pallas_call(
        _rmsnorm_kernel,
        grid=(T // tm,),
        in_specs=[pl.BlockSpec((tm, K), lambda i: (i, 0)),
                  pl.BlockSpec((1, K), lambda i: (0, 0))],
        out_specs=pl.BlockSpec((tm, K), lambda i: (i, 0)),
        out_shape=jax.ShapeDtypeStruct((T, K), BF16),
        compiler_params=pltpu.CompilerParams(
            dimension_semantics=("parallel",),
            vmem_limit_bytes=_vmem_limit(est)),
        name="rmsnorm",
    )(x, g)


def _retention_kernel(lg_ref, q_ref, k_ref, v_ref, gate_ref, cos_ref, sin_ref, o_ref,
                      state_ref, dmat_ref, qdec_ref, kdec_ref, sdec_ref, *, k_scale):
    L = RET_BLOCK
    dk = q_ref.shape[1]
    half = dk // 2
    lg = lg_ref[pl.program_id(1)]

    @pl.when(pl.program_id(2) == 0)
    def _init():
        state_ref[...] = jnp.zeros_like(state_ref)
        n = lax.broadcasted_iota(jnp.int32, (L, L), 0)
        m = lax.broadcasted_iota(jnp.int32, (L, L), 1)
        cn = n >> CHUNK_SHIFT
        cm = m >> CHUNK_SHIFT
        d = (n - m).astype(F32)
        expo = jnp.where(cn == cm, jnp.abs(d), d)
        visible = cm <= cn
        dmat_ref[...] = jnp.where(visible, jnp.exp(lg * jnp.where(visible, expo, 0.0)), 0.0)
        row = lax.broadcasted_iota(jnp.int32, (L, dk), 0).astype(F32)
        qdec_ref[...] = jnp.exp(lg * (row + 1.0))
        kdec_ref[...] = jnp.exp(lg * (float(L - 1) - row))
        sdec_ref[...] = jnp.exp(jnp.full(sdec_ref.shape, lg * float(L), F32))

    def rope(x, cos, sin):
        x1 = x[:, :half]
        x2 = x[:, half:]
        return jnp.concatenate([x1 * cos - x2 * sin, x2 * cos + x1 * sin], axis=1)

    for sub in range(q_ref.shape[0] // L):
        rows = pl.ds(sub * L, L)
        cos = cos_ref[rows, :]
        sin = sin_ref[rows, :]
        q = rope(q_ref[rows, :].astype(F32), cos, sin)
        k = rope(k_ref[rows, :].astype(F32), cos, sin) * k_scale
        v = v_ref[rows, :]
        qb = q.astype(BF16)
        kb = k.astype(BF16)
        scores = lax.dot_general(qb, kb, (((1,), (1,)), ((), ())),
                                 preferred_element_type=F32)
        scores = (scores * dmat_ref[...]).astype(BF16)
        state = state_ref[...]
        out = (jnp.dot(scores, v, preferred_element_type=F32)
               + jnp.dot((q * qdec_ref[...]).astype(BF16), state.astype(BF16),
                         preferred_element_type=F32))
        kd = (k * kdec_ref[...]).astype(BF16)
        state_ref[...] = state * sdec_ref[0:1, :] + lax.dot_general(
            kd, v, (((0,), (0,)), ((), ())), preferred_element_type=F32)
        mu = jnp.mean(out, axis=-1, keepdims=True)
        cen = out - mu
        var = jnp.mean(cen * cen, axis=-1, keepdims=True)
        gate = gate_ref[rows, :].astype(F32)
        o_ref[rows, :] = (_silu(gate) * (cen * lax.rsqrt(var + EPS))).astype(o_ref.dtype)


def retention_group(proj, log_g, cos, sin, *, batch, seq, rows):
    T = proj.shape[0]
    H, dk = RET_HEADS, RET_HEAD_DIM
    nblk = seq // rows

    def col(offset):
        return pl.BlockSpec((rows, dk), lambda b, h, r: (b * nblk + r, offset + h))

    tab = pl.BlockSpec((rows, dk // 2), lambda b, h, r: (r, 0))
    est = 10 * _nbytes((rows, dk), BF16) + 4 * _nbytes((rows, dk // 2), F32) \
        + 4 * _nbytes((RET_BLOCK, dk), F32)
    return pl.pallas_call(
        functools.partial(_retention_kernel, k_scale=dk ** -0.5),
        grid=(batch, H, nblk),
        in_specs=[pl.BlockSpec(memory_space=pltpu.SMEM),
                  col(0), col(H), col(2 * H), col(3 * H), tab, tab],
        out_specs=pl.BlockSpec((rows, dk), lambda b, h, r: (b * nblk + r, h)),
        out_shape=jax.ShapeDtypeStruct((T, H * dk), BF16),
        scratch_shapes=[pltpu.VMEM((dk, dk), F32),
                        pltpu.VMEM((RET_BLOCK, RET_BLOCK), F32),
                        pltpu.VMEM((RET_BLOCK, dk), F32),
                        pltpu.VMEM((RET_BLOCK, dk), F32),
                        pltpu.VMEM((8, dk), F32)],
        compiler_params=pltpu.CompilerParams(
            dimension_semantics=("parallel", "parallel", "arbitrary"),
            vmem_limit_bytes=_vmem_limit(est)),
        name="retention",
    )(log_g, proj, proj, proj, proj, cos, sin)


def _mla_proj_kernel(cq_ref, ckv_ref, kr_ref, gq_ref, gkv_ref, wq_ref, wk_ref, wvt_ref,
                     rc_ref, rn_ref, rp_ref, q_ref, k_ref, vt_ref, cqn_ref, ckvn_ref,
                     *, scale):
    tm = cq_ref.shape[0]
    _rms_rows(cq_ref, gq_ref, cqn_ref, tm)
    _rms_rows(ckv_ref, gkv_ref, ckvn_ref, tm)
    rc = rc_ref[...]
    rn = rn_ref[...]
    rp = rp_ref[...]

    def rope(x):
        return (x * rc + pltpu.roll(x, LANES - MLA_ROPE // 2, 1) * rn
                + pltpu.roll(x, MLA_ROPE // 2, 1) * rp)

    lane = lax.broadcasted_iota(jnp.int32, (tm, LANES), 1)
    kr = jnp.where(lane < MLA_ROPE, kr_ref[:, :LANES].astype(F32), 0.0)
    krp = rope(kr).astype(k_ref.dtype)
    cqn = cqn_ref[...]
    ckvn = ckvn_ref[...]
    W = MLA_QK_PAD
    for h in range(MLA_HEADS):
        qh = jnp.dot(cqn, wq_ref[:, h * W:(h + 1) * W], preferred_element_type=F32) * scale
        q_ref[:, h * W:h * W + LANES] = qh[:, :LANES].astype(q_ref.dtype)
        q_ref[:, h * W + LANES:(h + 1) * W] = rope(qh[:, LANES:]).astype(q_ref.dtype)
    for c in range(MLA_HEADS // 2):
        kn = jnp.dot(ckvn, wk_ref[:, c * W:(c + 1) * W], preferred_element_type=F32)
        for s in range(2):
            h = 2 * c + s
            k_ref[:, h * W:h * W + LANES] = kn[:, s * LANES:(s + 1) * LANES].astype(k_ref.dtype)
            k_ref[:, h * W + LANES:(h + 1) * W] = krp
    vt_ref[0] = lax.dot_general(wvt_ref[...], ckvn, (((1,), (1,)), ((), ())),
                                preferred_element_type=F32).astype(vt_ref.dtype)


def mla_projections(proj, g_q, g_kv, wq, wk, wvt, rope_c, rope_n, rope_p, *, seq, tm,
                    kv_tile, cq_blk, ckv_blk, kr_blk):
    T = proj.shape[0]
    q_lora = wq.shape[0]
    kv_lora = wk.shape[0]
    nq = wq.shape[1]
    nv = wvt.shape[0]
    nblk = seq // tm
    per_tile = kv_tile // tm
    scale = (MLA_NOPE + MLA_ROPE) ** -0.5 * math.log2(math.e)
    const = lambda i: (0, 0)
    tab = pl.BlockSpec((tm, LANES), lambda i: (i % nblk, 0))
    est = (2 * (_nbytes(wq.shape, BF16) + _nbytes(wk.shape, BF16) + _nbytes(wvt.shape, BF16))
           + 2 * (2 * _nbytes((tm, nq), BF16) + _nbytes((tm, nv), BF16))
           + 3 * _nbytes((tm, q_lora + kv_lora + 2 * LANES), BF16)
           + 6 * _nbytes((tm, LANES), F32) + _nbytes((tm, nv), F32))
    return pl.pallas_call(
        functools.partial(_mla_proj_kernel, scale=scale),
        grid=(T // tm,),
        in_specs=[pl.BlockSpec((tm, q_lora), lambda i: (i, cq_blk)),
                  pl.BlockSpec((tm, kv_lora), lambda i: (i, ckv_blk)),
                  pl.BlockSpec((tm, 2 * LANES), lambda i: (i, kr_blk)),
                  pl.BlockSpec((1, q_lora), const),
                  pl.BlockSpec((1, kv_lora), const),
                  pl.BlockSpec(wq.shape, const),
                  pl.BlockSpec(wk.shape, const),
                  pl.BlockSpec(wvt.shape, const),
                  tab, tab, tab],
        out_specs=[pl.BlockSpec((tm, nq), lambda i: (i, 0)),
                   pl.BlockSpec((tm, nq), lambda i: (i, 0)),
                   pl.BlockSpec((1, nv, tm), lambda i: (i // per_tile, 0, i % per_tile))],
        out_shape=[jax.ShapeDtypeStruct((T, nq), BF16),
                   jax.ShapeDtypeStruct((T, nq), BF16),
                   jax.ShapeDtypeStruct((T // kv_tile, nv, kv_tile), BF16)],
        scratch_shapes=[pltpu.VMEM((tm, q_lora), BF16),
                        pltpu.VMEM((tm, kv_lora), BF16)],
        compiler_params=pltpu.CompilerParams(
            dimension_semantics=("parallel",),
            vmem_limit_bytes=_vmem_limit(est)),
        name="mla_proj",
    )(proj, proj, proj, g_q, g_kv, wq, wk, wvt, rope_c, rope_n, rope_p)


def _mla_attn_kernel(q_ref, k_ref, vt_ref, o_ref, *, tile, heads):
    qi = pl.program_id(2)
    W, dv = MLA_QK_PAD, MLA_V
    qs = [q_ref[:, h * W:(h + 1) * W] for h in range(heads)]

    def step(j, mask, carry):
        start = pl.multiple_of(j * tile, tile)
        scores = [lax.dot_general(k_ref[pl.ds(start, tile), h * W:(h + 1) * W], qs[h],
                                  (((1,), (1,)), ((), ())), preferred_element_type=F32)
                  for h in range(heads)]
        out = []
        for h in range(heads):
            m, l, acc = carry[h]
            vt = vt_ref[j, h * dv:(h + 1) * dv, :]
            s = scores[h]
            if mask is not None:
                s = jnp.where(mask, s, -jnp.inf)
            m_new = jnp.maximum(m, jnp.max(s, axis=0, keepdims=True))
            alpha = jnp.exp2(m - m_new)
            p = jnp.exp2(s - m_new)
            l = alpha * l + jnp.sum(p, axis=0, keepdims=True)
            acc = alpha * acc + jnp.dot(vt, p.astype(vt.dtype), preferred_element_type=F32)
            out.append((m_new, l, acc))
        return tuple(out)

    init = tuple((jnp.full((1, tile), -jnp.inf, F32), jnp.zeros((1, tile), F32),
                  jnp.zeros((dv, tile), F32)) for _ in range(heads))
    carry = lax.fori_loop(0, qi, lambda j, c: step(j, None, c), init)
    key_chunk = lax.broadcasted_iota(jnp.int32, (tile, tile), 0) >> CHUNK_SHIFT
    qry_chunk = lax.broadcasted_iota(jnp.int32, (tile, tile), 1) >> CHUNK_SHIFT
    carry = step(qi, key_chunk <= qry_chunk, carry)
    for h in range(heads):
        _, l, acc = carry[h]
        o_ref[:, h * dv:(h + 1) * dv] = (acc * (1.0 / l)).T.astype(o_ref.dtype)


def mla_attention(qc, kc, vt, *, batch, seq, tile, heads):
    T = qc.shape[0]
    H, W, dv = MLA_HEADS, MLA_QK_PAD, MLA_V
    nq = seq // tile
    est = (2 * heads * (_nbytes((seq, W), BF16) + _nbytes((seq, dv), BF16)
                        + _nbytes((tile, W), BF16) + _nbytes((tile, dv), BF16))
           + 4 * heads * _nbytes((tile, tile), F32))
    return pl.pallas_call(
        functools.partial(_mla_attn_kernel, tile=tile, heads=heads),
        grid=(batch, H // heads, nq),
        in_specs=[pl.BlockSpec((tile, heads * W), lambda b, h, i: (b * nq + i, h)),
                  pl.BlockSpec((seq, heads * W), lambda b, h, i: (b, h)),
                  pl.BlockSpec((nq, heads * dv, tile), lambda b, h, i: (b, h, 0))],
        out_specs=pl.BlockSpec((tile, heads * dv), lambda b, h, i: (b * nq + i, h)),
        out_shape=jax.ShapeDtypeStruct((T, H * dv), BF16),
        compiler_params=pltpu.CompilerParams(
            dimension_semantics=("parallel", "parallel", "arbitrary"),
            vmem_limit_bytes=_vmem_limit(est)),
        name="mla_attn",
    )(qc, kc, vt)


def _out_proj_kernel(ro_ref, mo_ref, wr_ref, wm_ref, x_ref, o_ref):
    acc = jnp.dot(ro_ref[...], wr_ref[...], preferred_element_type=F32)
    acc = acc + jnp.dot(mo_ref[...], wm_ref[...], preferred_element_type=F32)
    o_ref[...] = x_ref[...] + acc


def out_projection(ro, mo, w_o, x, *, tm, tn):
    T, kr = ro.shape
    km = mo.shape[1]
    assert kr == km
    N = w_o.shape[1]
    est = (4 * _nbytes((tm, kr), BF16) + 4 * _nbytes((kr, tn), BF16)
           + 5 * _nbytes((tm, tn), F32))
    return pl.pallas_call(
        _out_proj_kernel,
        grid=(T // tm, N // tn),
        in_specs=[pl.BlockSpec((tm, kr), lambda i, j: (i, 0)),
                  pl.BlockSpec((tm, km), lambda i, j: (i, 0)),
                  pl.BlockSpec((kr, tn), lambda i, j: (0, j)),
                  pl.BlockSpec((km, tn), lambda i, j: (1, j)),
                  pl.BlockSpec((tm, tn), lambda i, j: (i, j))],
        out_specs=pl.BlockSpec((tm, tn), lambda i, j: (i, j)),
        out_shape=jax.ShapeDtypeStruct((T, N), F32),
        compiler_params=pltpu.CompilerParams(
            dimension_semantics=("parallel", "arbitrary"),
            vmem_limit_bytes=_vmem_limit(est)),
        name="out_proj",
    )(ro, mo, w_o, w_o, x)


def _ffn_up_kernel(hn_ref, wg_ref, wu_ref, cw_ref, cb_ref, o_ref, g_ref, *, tiles_per_seq):
    tm = hn_ref.shape[0]
    halo = CONV_HALO

    @pl.when(pl.program_id(1) % tiles_per_seq == 0)
    def _():
        g_ref[0:halo, :] = jnp.zeros((halo, g_ref.shape[1]), F32)

    hn = hn_ref[...]
    g_ref[halo:halo + tm, :] = jnp.dot(hn, wg_ref[...], preferred_element_type=F32)
    up = jnp.dot(hn, wu_ref[...], preferred_element_type=F32)
    a = cb_ref[...]
    for j in range(CONV_WIDTH):
        shift = CONV_WIDTH - 1 - j
        a = a + g_ref[halo - shift:halo - shift + tm, :] * cw_ref[j:j + 1, :]
    o_ref[...] = (_silu(a) * up).astype(o_ref.dtype)
    g_ref[0:halo, :] = g_ref[tm:tm + halo, :]


def ffn_up(hn, w_gate, w_up, conv_w, conv_b, *, seq, tm, tf):
    T, K = hn.shape
    FF = w_gate.shape[1]
    est = (2 * _nbytes((tm, K), BF16) + 4 * _nbytes((K, tf), BF16)
           + 2 * _nbytes((tm, tf), BF16) + 4 * _nbytes((tm + CONV_HALO, tf), F32))
    return pl.pallas_call(
        functools.partial(_ffn_up_kernel, tiles_per_seq=seq // tm),
        grid=(pl.cdiv(FF, tf), T // tm),
        in_specs=[pl.BlockSpec((tm, K), lambda j, i: (i, 0)),
                  pl.BlockSpec((K, tf), lambda j, i: (0, j)),
                  pl.BlockSpec((K, tf), lambda j, i: (0, j)),
                  pl.BlockSpec((CONV_WIDTH, tf), lambda j, i: (0, j)),
                  pl.BlockSpec((1, tf), lambda j, i: (0, j))],
        out_specs=pl.BlockSpec((tm, tf), lambda j, i: (i, j)),
        out_shape=jax.ShapeDtypeStruct((T, FF), BF16),
        scratch_shapes=[pltpu.VMEM((tm + CONV_HALO, tf), F32)],
        compiler_params=pltpu.CompilerParams(
            dimension_semantics=("arbitrary", "arbitrary"),
            vmem_limit_bytes=_vmem_limit(est)),
        name="ffn_up",
    )(hn, w_gate, w_up, conv_w, conv_b)


def _ffn_down_kernel(a_ref, w_ref, h_ref, o_ref):
    o_ref[...] = h_ref[...] + jnp.dot(a_ref[...], w_ref[...], preferred_element_type=F32)


def ffn_down(hidden, w_down, h, *, tm, tn):
    T, K = hidden.shape
    N = w_down.shape[1]
    est = (2 * _nbytes((tm, K), BF16) + 2 * _nbytes((K, tn), BF16)
           + 5 * _nbytes((tm, tn), F32))
    return pl.pallas_call(
        _ffn_down_kernel,
        grid=(T // tm, N // tn),
        in_specs=[pl.BlockSpec((tm, K), lambda i, j: (i, 0)),
                  pl.BlockSpec((K, tn), lambda i, j: (0, j)),
                  pl.BlockSpec((tm, tn), lambda i, j: (i, j))],
        out_specs=pl.BlockSpec((tm, tn), lambda i, j: (i, j)),
        out_shape=jax.ShapeDtypeStruct((T, N), F32),
        compiler_params=pltpu.CompilerParams(
            dimension_semantics=("parallel", "arbitrary"),
            vmem_limit_bytes=_vmem_limit(est)),
        name="ffn_down",
    )(hidden, w_down, h)


def _ple_kernel(h_ref, g_ref, wg_ref, p_ref, wp_ref, gf_ref, o_ref, hn_ref):
    j = pl.program_id(1)
    tm = h_ref.shape[0]
    tn = wg_ref.shape[1]

    @pl.when(j == 0)
    def _():
        _rms_rows(h_ref, g_ref, hn_ref, tm)

    z = jnp.dot(hn_ref[...], wg_ref[...], preferred_element_type=F32)
    gate = 1.0 / (1.0 + jnp.exp(-z))
    emb = jnp.dot(p_ref[...].astype(BF16), wp_ref[...], preferred_element_type=F32)
    delta = gate * emb
    for jj in range(h_ref.shape[1] // tn):
        @pl.when(j == jj)
        def _(jj=jj):
            cols = slice(jj * tn, (jj + 1) * tn)
            o_ref[:, cols] = h_ref[:, cols] + delta

    @pl.when(j == pl.num_programs(1) - 1)
    def _():
        _rms_rows(o_ref, gf_ref, o_ref, tm)


def ple_final(h, g_ple, w_gate, p, w_proj, g_final, *, tm, tn):
    T, D = h.shape
    P = p.shape[1]
    est = (4 * _nbytes((tm, D), F32) + _nbytes((tm, D), BF16)
           + 2 * _nbytes((D, tn), BF16) + 2 * _nbytes((P, tn), BF16)
           + 2 * _nbytes((tm, P), F32) + 4 * _nbytes((tm, tn), F32))
    return pl.pallas_call(
        _ple_kernel,
        grid=(T // tm, D // tn),
        in_specs=[pl.BlockSpec((tm, D), lambda i, j: (i, 0)),
                  pl.BlockSpec((1, D), lambda i, j: (0, 0)),
                  pl.BlockSpec((D, tn), lambda i, j: (0, j)),
                  pl.BlockSpec((tm, P), lambda i, j: (i, 0)),
                  pl.BlockSpec((P, tn), lambda i, j: (0, j)),
                  pl.BlockSpec((1, D), lambda i, j: (0, 0))],
        out_specs=pl.BlockSpec((tm, D), lambda i, j: (i, 0)),
        out_shape=jax.ShapeDtypeStruct((T, D), F32),
        scratch_shapes=[pltpu.VMEM((tm, D), BF16)],
        compiler_params=pltpu.CompilerParams(
            dimension_semantics=("parallel", "arbitrary"),
            vmem_limit_bytes=_vmem_limit(est)),
        name="ple_final",
    )(h, g_ple, w_gate, p, w_proj, g_final)


def _rope_tables(seq, dim):
    inv = 1.0 / (ROPE_BASE ** (jnp.arange(0, dim, 2, dtype=F32) / dim))
    ang = jnp.arange(seq, dtype=F32)[:, None] * inv[None, :]
    return jnp.cos(ang), jnp.sin(ang)


def _tile_config(seq):
    return dict(
        in_proj=dict(tm=min(512, seq), tn=768),
        retention_rows=min(1024, seq),
        mla_proj_tm=min(256, seq),
        attn_tile=min(512, seq),
        attn_heads=4,
        out_proj=dict(tm=min(1024, seq), tn=512),
        norm_tm=min(256, seq),
        ffn_up=dict(tm=min(1024, seq), tf=512),
        ffn_down=dict(tm=min(512, seq), tn=512),
        ple=dict(tm=min(512, seq), tn=512),
    )


def _layer(h, p_i, w_in, g_attn, g_q_lora, g_kv_lora, w_uq, w_ukv, w_o, g_ffn,
           w_ffn_gate, w_ffn_up, conv_w, conv_b, w_ffn_down, g_ple, w_ple_gate,
           w_ple_proj, g_out, *, batch, seq):
    cfg = _tile_config(seq)
    D = h.shape[1]
    ret_w = RET_HEADS * RET_HEAD_DIM
    q_lora = w_uq.shape[0]
    kv_lora = w_ukv.shape[0]
    d_ff = w_ffn_gate.shape[1]

    in_w = w_in.shape[1]
    tn_in = cfg["in_proj"]["tn"]
    in_pad = pl.cdiv(in_w, tn_in) * tn_in
    proj = norm_matmul(h, g_attn.reshape(1, D), w_in.astype(BF16), n_out=in_pad,
                       out_dtype=BF16, **cfg["in_proj"])

    log_g = jnp.log1p(-jnp.exp2(-5.0 - jnp.arange(RET_HEADS, dtype=F32)))
    cos_r, sin_r = _rope_tables(seq, RET_HEAD_DIM)
    ro = retention_group(proj, log_g, cos_r, sin_r, batch=batch, seq=seq,
                         rows=cfg["retention_rows"])

    hq = MLA_NOPE + MLA_ROPE
    wq = w_uq.reshape(q_lora, MLA_HEADS, hq)
    wq = jnp.pad(wq, ((0, 0), (0, 0), (0, MLA_QK_PAD - hq)))
    wq = wq.reshape(q_lora, MLA_HEADS * MLA_QK_PAD).astype(BF16)
    wkv = w_ukv.reshape(kv_lora, MLA_HEADS, MLA_NOPE + MLA_V)
    wk = wkv[:, :, :MLA_NOPE].reshape(kv_lora, MLA_HEADS * MLA_NOPE).astype(BF16)
    wvt = wkv[:, :, MLA_NOPE:].reshape(kv_lora, MLA_HEADS * MLA_V).T.astype(BF16)
    cos_m, sin_m = _rope_tables(seq, MLA_ROPE)
    zeros = jnp.zeros_like(cos_m)
    fill = jnp.zeros((seq, LANES - MLA_ROPE), F32)
    rope_c = jnp.concatenate([cos_m, cos_m, fill], axis=1)
    rope_n = jnp.concatenate([-sin_m, zeros, fill], axis=1)
    rope_p = jnp.concatenate([zeros, sin_m, fill], axis=1)
    cq_off = 4 * ret_w
    ckv_off = cq_off + q_lora
    kr_off = ckv_off + kv_lora
    assert cq_off % q_lora == 0 and ckv_off % kv_lora == 0 and kr_off % (2 * LANES) == 0
    assert in_pad - kr_off >= 2 * LANES
    qc, kc, vt = mla_projections(
        proj, g_q_lora.reshape(1, q_lora), g_kv_lora.reshape(1, kv_lora), wq, wk, wvt,
        rope_c, rope_n, rope_p, seq=seq, tm=cfg["mla_proj_tm"], kv_tile=cfg["attn_tile"],
        cq_blk=cq_off // q_lora, ckv_blk=ckv_off // kv_lora, kr_blk=kr_off // (2 * LANES))
    mo = mla_attention(qc, kc, vt, batch=batch, seq=seq, tile=cfg["attn_tile"],
                       heads=cfg["attn_heads"])

    h1 = out_projection(ro, mo, w_o.astype(BF16), h, **cfg["out_proj"])

    hn = rmsnorm_bf16(h1, g_ffn.reshape(1, D), tm=cfg["norm_tm"])
    hidden = ffn_up(hn, w_ffn_gate.astype(BF16), w_ffn_up.astype(BF16), conv_w,
                    conv_b.reshape(1, d_ff), seq=seq, **cfg["ffn_up"])
    h2 = ffn_down(hidden, w_ffn_down.astype(BF16), h1, **cfg["ffn_down"])

    return ple_final(h2, g_ple.reshape(1, D), w_ple_gate.astype(BF16), p_i,
                     w_ple_proj.astype(BF16), g_out.reshape(1, D), **cfg["ple"])


def kernel(x, p, w_in, g_attn, g_q_lora, g_kv_lora, w_uq, w_ukv, w_o, g_ffn, w_ffn_gate,
           w_ffn_up, conv_w, conv_b, w_ffn_down, g_ple, w_ple_gate, w_ple_proj, g_final):
    B, S, D = x.shape
    depth = p.shape[0]
    assert depth == 1, "the final RMSNorm is fused into the layer's last kernel"
    h = x.reshape(B * S, D)
    out = _layer(h, p[0].reshape(B * S, -1), w_in[0], g_attn[0], g_q_lora[0], g_kv_lora[0],
                 w_uq[0], w_ukv[0], w_o[0], g_ffn[0], w_ffn_gate[0], w_ffn_up[0], conv_w[0],
                 conv_b[0], w_ffn_down[0], g_ple[0], w_ple_gate[0], w_ple_proj[0], g_final,
                 batch=B, seq=S)
    return out.reshape(B, S, D)
```

```python
import functools
import math

import jax
import jax.numpy as jnp
from jax import lax
from jax.experimental import pallas as pl
from jax.experimental.pallas import tpu as pltpu

F32 = jnp.float32
BF16 = jnp.bfloat16

EPS = 1e-6
ROPE_BASE = 10000.0
CHUNK = 64
CHUNK_SHIFT = 6
RET_HEADS = 8
RET_HEAD_DIM = 256
MLA_HEADS = 16
MLA_NOPE = 128
MLA_ROPE = 64
MLA_V = 128
MLA_QK_PAD = 256
CONV_WIDTH = 3
RET_BLOCK = 256
LANES = 128
CONV_HALO = 8
MIB = 1024 * 1024
VMEM_LIMIT_CAP = 60 * MIB


def _vmem_limit(nbytes):
    return int(min(VMEM_LIMIT_CAP, nbytes + 16 * MIB))


def _nbytes(shape, dtype):
    return math.prod(shape) * jnp.dtype(dtype).itemsize


def _rms_rows(x_ref, g_ref, o_ref, n_rows, row_chunk=16, unroll=4):
    g = g_ref[...]

    def body(c, carry):
        r = pl.multiple_of(c * row_chunk, row_chunk)
        x = x_ref[pl.ds(r, row_chunk), :].astype(F32)
        ms = jnp.mean(x * x, axis=-1, keepdims=True)
        o_ref[pl.ds(r, row_chunk), :] = ((x * lax.rsqrt(ms + EPS)) * g).astype(o_ref.dtype)
        return carry

    lax.fori_loop(0, n_rows // row_chunk, body, 0, unroll=unroll)


def _row_scale(ssq_ref, width):
    return lax.rsqrt(ssq_ref[...] * (1.0 / width) + EPS)


def _silu(x):
    return x * (1.0 / (1.0 + jnp.exp(-x)))


def _norm_matmul_kernel(x_ref, g_ref, w_ref, o_ref, xn_ref, *, edge_cols):
    j = pl.program_id(1)
    last = pl.num_programs(1) - 1

    @pl.when(j == 0)
    def _():
        _rms_rows(x_ref, g_ref, xn_ref, x_ref.shape[0])

    res = jnp.dot(xn_ref[...], w_ref[...], preferred_element_type=F32)
    if edge_cols == 0:
        o_ref[...] = res.astype(o_ref.dtype)
    else:
        @pl.when(j < last)
        def _():
            o_ref[...] = res.astype(o_ref.dtype)

        @pl.when(j == last)
        def _():
            col = lax.broadcasted_iota(jnp.int32, res.shape, 1)
            o_ref[...] = jnp.where(col < edge_cols, res, 0.0).astype(o_ref.dtype)


def norm_matmul(x, g, w, *, n_out, tm, tn, out_dtype):
    T, K = x.shape
    N = n_out
    assert N % tn == 0 and N - tn < w.shape[1] <= N
    est = (2 * _nbytes((tm, K), x.dtype) + _nbytes((tm, K), BF16)
           + 2 * _nbytes((K, tn), w.dtype) + 2 * _nbytes((tm, tn), out_dtype)
           + _nbytes((tm, tn), F32))
    return pl.pallas_call(
        functools.partial(_norm_matmul_kernel, edge_cols=w.shape[1] % tn),
        grid=(T // tm, N // tn),
        in_specs=[pl.BlockSpec((tm, K), lambda i, j: (i, 0)),
                  pl.BlockSpec((1, K), lambda i, j: (0, 0)),
                  pl.BlockSpec((K, tn), lambda i, j: (0, j))],
        out_specs=pl.BlockSpec((tm, tn), lambda i, j: (i, j)),
        out_shape=jax.ShapeDtypeStruct((T, N), out_dtype),
        scratch_shapes=[pltpu.VMEM((tm, K), BF16)],
        compiler_params=pltpu.CompilerParams(
            dimension_semantics=("parallel", "arbitrary"),
            vmem_limit_bytes=_vmem_limit(est)),
        name="in_proj",
    )(x, g, w)


def _retention_kernel(lg_ref, q_ref, k_ref, v_ref, gate_ref, cos_ref, sin_ref, o_ref,
                      state_ref, dmat_ref, qdec_ref, kdec_ref, sdec_ref, *, k_scale):
    L = RET_BLOCK
    dk = q_ref.shape[1]
    half = dk // 2
    lg = lg_ref[pl.program_id(1)]

    @pl.when(pl.program_id(2) == 0)
    def _init():
        state_ref[...] = jnp.zeros_like(state_ref)
        n = lax.broadcasted_iota(jnp.int32, (L, L), 0)
        m = lax.broadcasted_iota(jnp.int32, (L, L), 1)
        cn = n >> CHUNK_SHIFT
        cm = m >> CHUNK_SHIFT
        d = (n - m).astype(F32)
        expo = jnp.where(cn == cm, jnp.abs(d), d)
        visible = cm <= cn
        dmat_ref[...] = jnp.where(visible, jnp.exp(lg * jnp.where(visible, expo, 0.0)), 0.0)
        row = lax.broadcasted_iota(jnp.int32, (L, dk), 0).astype(F32)
        qdec_ref[...] = jnp.exp(lg * (row + 1.0))
        kdec_ref[...] = jnp.exp(lg * (float(L - 1) - row))
        sdec_ref[...] = jnp.exp(jnp.full(sdec_ref.shape, lg * float(L), F32))

    def rope(x, cos, sin):
        x1 = x[:, :half]
        x2 = x[:, half:]
        return jnp.concatenate([x1 * cos - x2 * sin, x2 * cos + x1 * sin], axis=1)

    for sub in range(q_ref.shape[0] // L):
        rows = pl.ds(sub * L, L)
        cos = cos_ref[rows, :]
        sin = sin_ref[rows, :]
        q = rope(q_ref[rows, :].astype(F32), cos, sin)
        k = rope(k_ref[rows, :].astype(F32), cos, sin) * k_scale
        v = v_ref[rows, :]
        qb = q.astype(BF16)
        kb = k.astype(BF16)
        scores = lax.dot_general(qb, kb, (((1,), (1,)), ((), ())),
                                 preferred_element_type=F32)
        scores = (scores * dmat_ref[...]).astype(BF16)
        state = state_ref[...]
        out = (jnp.dot(scores, v, preferred_element_type=F32)
               + jnp.dot((q * qdec_ref[...]).astype(BF16), state.astype(BF16),
                         preferred_element_type=F32))
        kd = (k * kdec_ref[...]).astype(BF16)
        state_ref[...] = state * sdec_ref[0:1, :] + lax.dot_general(
            kd, v, (((0,), (0,)), ((), ())), preferred_element_type=F32)
        mu = jnp.mean(out, axis=-1, keepdims=True)
        cen = out - mu
        var = jnp.mean(cen * cen, axis=-1, keepdims=True)
        gate = gate_ref[rows, :].astype(F32)
        o_ref[rows, :] = (_silu(gate) * (cen * lax.rsqrt(var + EPS))).astype(o_ref.dtype)


def retention_group(proj, log_g, cos, sin, *, batch, seq, rows):
    T = proj.shape[0]
    H, dk = RET_HEADS, RET_HEAD_DIM
    nblk = seq // rows

    def col(offset):
        return pl.BlockSpec((rows, dk), lambda b, h, r: (b * nblk + r, offset + h))

    tab = pl.BlockSpec((rows, dk // 2), lambda b, h, r: (r, 0))
    est = 10 * _nbytes((rows, dk), BF16) + 4 * _nbytes((rows, dk // 2), F32) \
        + 4 * _nbytes((RET_BLOCK, dk), F32)
    return pl.pallas_call(
        functools.partial(_retention_kernel, k_scale=dk ** -0.5),
        grid=(batch, H, nblk),
        in_specs=[pl.BlockSpec(memory_space=pltpu.SMEM),
                  col(0), col(H), col(2 * H), col(3 * H), tab, tab],
        out_specs=pl.BlockSpec((rows, dk), lambda b, h, r: (b * nblk + r, h)),
        out_shape=jax.ShapeDtypeStruct((T, H * dk), BF16),
        scratch_shapes=[pltpu.VMEM((dk, dk), F32),
                        pltpu.VMEM((RET_BLOCK, RET_BLOCK), F32),
                        pltpu.VMEM((RET_BLOCK, dk), F32),
                        pltpu.VMEM((RET_BLOCK, dk), F32),
                        pltpu.VMEM((8, dk), F32)],
        compiler_params=pltpu.CompilerParams(
            dimension_semantics=("parallel", "parallel", "arbitrary"),
            vmem_limit_bytes=_vmem_limit(est)),
        name="retention",
    )(log_g, proj, proj, proj, proj, cos, sin)


def _mla_proj_kernel(cq_ref, ckv_ref, kr_ref, gq_ref, gkv_ref, wq_ref, wk_ref, wvt_ref,
                     rc_ref, rn_ref, rp_ref, q_ref, k_ref, vt_ref, cqn_ref, ckvn_ref,
                     *, scale):
    tm = cq_ref.shape[0]
    _rms_rows(cq_ref, gq_ref, cqn_ref, tm)
    _rms_rows(ckv_ref, gkv_ref, ckvn_ref, tm)
    rc = rc_ref[...]
    rn = rn_ref[...]
    rp = rp_ref[...]

    def rope(x):
        return (x * rc + pltpu.roll(x, LANES - MLA_ROPE // 2, 1) * rn
                + pltpu.roll(x, MLA_ROPE // 2, 1) * rp)

    krp = rope(kr_ref[:, :LANES].astype(F32)).astype(k_ref.dtype)
    cqn = cqn_ref[...]
    ckvn = ckvn_ref[...]
    W = MLA_QK_PAD
    for h in range(MLA_HEADS):
        qh = jnp.dot(cqn, wq_ref[:, h * W:(h + 1) * W], preferred_element_type=F32) * scale
        q_ref[:, h * W:h * W + LANES] = qh[:, :LANES].astype(q_ref.dtype)
        q_ref[:, h * W + LANES:(h + 1) * W] = rope(qh[:, LANES:]).astype(q_ref.dtype)
    for c in range(MLA_HEADS // 2):
        kn = jnp.dot(ckvn, wk_ref[:, c * W:(c + 1) * W], preferred_element_type=F32)
        for s in range(2):
            h = 2 * c + s
            k_ref[:, h * W:h * W + LANES] = kn[:, s * LANES:(s + 1) * LANES].astype(k_ref.dtype)
            k_ref[:, h * W + LANES:(h + 1) * W] = krp
    vt_ref[0] = lax.dot_general(wvt_ref[...], ckvn, (((1,), (1,)), ((), ())),
                                preferred_element_type=F32).astype(vt_ref.dtype)


def mla_projections(proj, g_q, g_kv, wq, wk, wvt, rope_c, rope_n, rope_p, *, seq, tm,
                    kv_tile, cq_blk, ckv_blk, kr_blk):
    T = proj.shape[0]
    q_lora = wq.shape[0]
    kv_lora = wk.shape[0]
    nq = wq.shape[1]
    nv = wvt.shape[0]
    nblk = seq // tm
    per_tile = kv_tile // tm
    scale = (MLA_NOPE + MLA_ROPE) ** -0.5 * math.log2(math.e)
    const = lambda i: (0, 0)
    tab = pl.BlockSpec((tm, LANES), lambda i: (i % nblk, 0))
    est = (2 * (_nbytes(wq.shape, BF16) + _nbytes(wk.shape, BF16) + _nbytes(wvt.shape, BF16))
           + 2 * (2 * _nbytes((tm, nq), BF16) + _nbytes((tm, nv), BF16))
           + 3 * _nbytes((tm, q_lora + kv_lora + 2 * LANES), BF16)
           + 6 * _nbytes((tm, LANES), F32) + _nbytes((tm, nv), F32))
    return pl.pallas_call(
        functools.partial(_mla_proj_kernel, scale=scale),
        grid=(T // tm,),
        in_specs=[pl.BlockSpec((tm, q_lora), lambda i: (i, cq_blk)),
                  pl.BlockSpec((tm, kv_lora), lambda i: (i, ckv_blk)),
                  pl.BlockSpec((tm, 2 * LANES), lambda i: (i, kr_blk)),
                  pl.BlockSpec((1, q_lora), const),
                  pl.BlockSpec((1, kv_lora), const),
                  pl.BlockSpec(wq.shape, const),
                  pl.BlockSpec(wk.shape, const),
                  pl.BlockSpec(wvt.shape, const),
                  tab, tab, tab],
        out_specs=[pl.BlockSpec((tm, nq), lambda i: (i, 0)),
                   pl.BlockSpec((tm, nq), lambda i: (i, 0)),
                   pl.BlockSpec((1, nv, tm), lambda i: (i // per_tile, 0, i % per_tile))],
        out_shape=[jax.ShapeDtypeStruct((T, nq), BF16),
                   jax.ShapeDtypeStruct((T, nq), BF16),
                   jax.ShapeDtypeStruct((T // kv_tile, nv, kv_tile), BF16)],
        scratch_shapes=[pltpu.VMEM((tm, q_lora), BF16),
                        pltpu.VMEM((tm, kv_lora), BF16)],
        compiler_params=pltpu.CompilerParams(
            dimension_semantics=("parallel",),
            vmem_limit_bytes=_vmem_limit(est)),
        name="mla_proj",
    )(proj, proj, proj, g_q, g_kv, wq, wk, wvt, rope_c, rope_n, rope_p)


def _mla_attn_kernel(q_ref, k_ref, vt_ref, o_ref, m_ref, l_ref, acc_ref, *, tile, heads):
    qi = pl.program_id(2)
    W, dv = MLA_QK_PAD, MLA_V
    qs = [q_ref[:, h * W:(h + 1) * W] for h in range(heads)]
    m_ref[...] = jnp.full(m_ref.shape, -jnp.inf, F32)
    l_ref[...] = jnp.zeros(l_ref.shape, F32)
    acc_ref[...] = jnp.zeros(acc_ref.shape, F32)

    def process(tiles):
        scores = []
        for j, _ in tiles:
            start = pl.multiple_of(j * tile, tile)
            scores.append([
                lax.dot_general(k_ref[pl.ds(start, tile), h * W:(h + 1) * W], qs[h],
                                (((1,), (1,)), ((), ())), preferred_element_type=F32)
                for h in range(heads)])
        for t, (j, mask) in enumerate(tiles):
            for h in range(heads):
                s = scores[t][h]
                if mask is not None:
                    s = jnp.where(mask, s, -jnp.inf)
                m = m_ref[h]
                m_new = jnp.maximum(m, jnp.max(s, axis=0, keepdims=True))
                alpha = jnp.exp2(m - m_new)
                p = jnp.exp2(s - m_new)
                vt = vt_ref[j, h * dv:(h + 1) * dv, :]
                l_ref[h] = alpha * l_ref[h] + jnp.sum(p, axis=0, keepdims=True)
                acc_ref[h] = alpha * acc_ref[h] + jnp.dot(vt, p.astype(vt.dtype),
                                                         preferred_element_type=F32)
                m_ref[h] = m_new

    key_chunk = lax.broadcasted_iota(jnp.int32, (tile, tile), 0) >> CHUNK_SHIFT
    qry_chunk = lax.broadcasted_iota(jnp.int32, (tile, tile), 1) >> CHUNK_SHIFT
    diag_mask = key_chunk <= qry_chunk

    @pl.when(qi == 0)
    def _():
        process([(qi, diag_mask)])

    @pl.when(qi > 0)
    def _():
        lead = (qi + 1) % 2

        @pl.when(lead == 1)
        def _():
            process([(0, None)])

        def pair(i, carry):
            j = lead + 2 * i
            process([(j, None), (j + 1, None)])
            return carry

        lax.fori_loop(0, (qi + 1 - lead) // 2 - 1, pair, 0)
        process([(qi - 1, None), (qi, diag_mask)])

    for h in range(heads):
        o_ref[:, h * dv:(h + 1) * dv] = (acc_ref[h] * (1.0 / l_ref[h])).T.astype(o_ref.dtype)


def mla_attention(qc, kc, vt, *, batch, seq, tile, heads):
    T = qc.shape[0]
    H, W, dv = MLA_HEADS, MLA_QK_PAD, MLA_V
    nq = seq // tile
    est = (2 * heads * (_nbytes((seq, W), BF16) + _nbytes((seq, dv), BF16)
                        + _nbytes((tile, W), BF16) + _nbytes((tile, dv), BF16))
           + 4 * heads * _nbytes((tile, tile), F32))
    return pl.pallas_call(
        functools.partial(_mla_attn_kernel, tile=tile, heads=heads),
        grid=(batch, H // heads, nq),
        in_specs=[pl.BlockSpec((tile, heads * W), lambda b, h, i: (b * nq + i, h)),
                  pl.BlockSpec((seq, heads * W), lambda b, h, i: (b, h)),
                  pl.BlockSpec((nq, heads * dv, tile), lambda b, h, i: (b, h, 0))],
        out_specs=pl.BlockSpec((tile, heads * dv), lambda b, h, i: (b * nq + i, h)),
        out_shape=jax.ShapeDtypeStruct((T, H * dv), BF16),
        scratch_shapes=[pltpu.VMEM((heads, 1, tile), F32),
                        pltpu.VMEM((heads, 1, tile), F32),
                        pltpu.VMEM((heads, dv, tile), F32)],
        compiler_params=pltpu.CompilerParams(
            dimension_semantics=("parallel", "parallel", "arbitrary"),
            vmem_limit_bytes=_vmem_limit(est)),
        name="mla_attn",
    )(qc, kc, vt)


def _emit_residual(h, g_ref, o_ref, hg_ref, ssq_ref):
    o_ref[...] = h
    hg_ref[...] = (h * g_ref[...]).astype(hg_ref.dtype)
    part = jnp.sum(h * h, axis=-1, keepdims=True)

    @pl.when(pl.program_id(1) == 0)
    def _():
        ssq_ref[...] = part

    @pl.when(pl.program_id(1) > 0)
    def _():
        ssq_ref[...] += part


def _residual_out(T, N, tm, tn):
    specs = [pl.BlockSpec((tm, tn), lambda i, j: (i, j)),
             pl.BlockSpec((tm, tn), lambda i, j: (i, j)),
             pl.BlockSpec((tm, 1), lambda i, j: (i, 0))]
    shapes = [jax.ShapeDtypeStruct((T, N), F32), jax.ShapeDtypeStruct((T, N), BF16),
              jax.ShapeDtypeStruct((T, 1), F32)]
    return specs, shapes


def _out_proj_kernel(ro_ref, mo_ref, wr_ref, wm_ref, x_ref, g_ref, o_ref, hg_ref, ssq_ref):
    acc = jnp.dot(ro_ref[...], wr_ref[...], preferred_element_type=F32)
    acc = acc + jnp.dot(mo_ref[...], wm_ref[...], preferred_element_type=F32)
    _emit_residual(x_ref[...] + acc, g_ref, o_ref, hg_ref, ssq_ref)


def out_projection(ro, mo, w_o, x, g_next, *, tm, tn):
    T, kr = ro.shape
    km = mo.shape[1]
    assert kr == km
    N = w_o.shape[1]
    est = (4 * _nbytes((tm, kr), BF16) + 4 * _nbytes((kr, tn), BF16)
           + 7 * _nbytes((tm, tn), F32))
    out_specs, out_shape = _residual_out(T, N, tm, tn)
    return pl.pallas_call(
        _out_proj_kernel,
        grid=(T // tm, N // tn),
        in_specs=[pl.BlockSpec((tm, kr), lambda i, j: (i, 0)),
                  pl.BlockSpec((tm, km), lambda i, j: (i, 0)),
                  pl.BlockSpec((kr, tn), lambda i, j: (0, j)),
                  pl.BlockSpec((km, tn), lambda i, j: (1, j)),
                  pl.BlockSpec((tm, tn), lambda i, j: (i, j)),
                  pl.BlockSpec((1, tn), lambda i, j: (0, j))],
        out_specs=out_specs,
        out_shape=out_shape,
        compiler_params=pltpu.CompilerParams(
            dimension_semantics=("parallel", "arbitrary"),
            vmem_limit_bytes=_vmem_limit(est)),
        name="out_proj",
    )(ro, mo, w_o, w_o, x, g_next)


def _ffn_up_kernel(hg_ref, ssq_ref, wg_ref, wu_ref, cw_ref, cb_ref, o_ref, g_ref,
                   *, tiles_per_seq):
    tm = hg_ref.shape[0]
    halo = CONV_HALO

    @pl.when(pl.program_id(1) % tiles_per_seq == 0)
    def _():
        g_ref[0:halo, :] = jnp.zeros((halo, g_ref.shape[1]), F32)

    hg = hg_ref[...]
    r = _row_scale(ssq_ref, hg_ref.shape[1])
    g_ref[halo:halo + tm, :] = jnp.dot(hg, wg_ref[...], preferred_element_type=F32) * r
    up = jnp.dot(hg, wu_ref[...], preferred_element_type=F32) * r
    a = cb_ref[...]
    for j in range(CONV_WIDTH):
        shift = CONV_WIDTH - 1 - j
        a = a + g_ref[halo - shift:halo - shift + tm, :] * cw_ref[j:j + 1, :]
    o_ref[...] = (_silu(a) * up).astype(o_ref.dtype)
    g_ref[0:halo, :] = g_ref[tm:tm + halo, :]


def ffn_up(hg, ssq, w_gate, w_up, conv_w, conv_b, *, seq, tm, tf):
    T, K = hg.shape
    FF = w_gate.shape[1]
    est = (2 * _nbytes((tm, K), BF16) + 4 * _nbytes((K, tf), BF16)
           + 2 * _nbytes((tm, tf), BF16) + 4 * _nbytes((tm + CONV_HALO, tf), F32)
           + 2 * _nbytes((tm, LANES), F32))
    return pl.pallas_call(
        functools.partial(_ffn_up_kernel, tiles_per_seq=seq // tm),
        grid=(pl.cdiv(FF, tf), T // tm),
        in_specs=[pl.BlockSpec((tm, K), lambda j, i: (i, 0)),
                  pl.BlockSpec((tm, 1), lambda j, i: (i, 0)),
                  pl.BlockSpec((K, tf), lambda j, i: (0, j)),
                  pl.BlockSpec((K, tf), lambda j, i: (0, j)),
                  pl.BlockSpec((CONV_WIDTH, tf), lambda j, i: (0, j)),
                  pl.BlockSpec((1, tf), lambda j, i: (0, j))],
        out_specs=pl.BlockSpec((tm, tf), lambda j, i: (i, j)),
        out_shape=jax.ShapeDtypeStruct((T, FF), BF16),
        scratch_shapes=[pltpu.VMEM((tm + CONV_HALO, tf), F32)],
        compiler_params=pltpu.CompilerParams(
            dimension_semantics=("arbitrary", "arbitrary"),
            vmem_limit_bytes=_vmem_limit(est)),
        name="ffn_up",
    )(hg, ssq, w_gate, w_up, conv_w, conv_b)


def _ffn_down_kernel(a_ref, w_ref, h_ref, g_ref, o_ref, hg_ref, ssq_ref):
    acc = jnp.dot(a_ref[...], w_ref[...], preferred_element_type=F32)
    _emit_residual(h_ref[...] + acc, g_ref, o_ref, hg_ref, ssq_ref)


def ffn_down(hidden, w_down, h, g_next, *, tm, tn):
    T, K = hidden.shape
    N = w_down.shape[1]
    est = (2 * _nbytes((tm, K), BF16) + 2 * _nbytes((K, tn), BF16)
           + 7 * _nbytes((tm, tn), F32))
    out_specs, out_shape = _residual_out(T, N, tm, tn)
    return pl.pallas_call(
        _ffn_down_kernel,
        grid=(T // tm, N // tn),
        in_specs=[pl.BlockSpec((tm, K), lambda i, j: (i, 0)),
                  pl.BlockSpec((K, tn), lambda i, j: (0, j)),
                  pl.BlockSpec((tm, tn), lambda i, j: (i, j)),
                  pl.BlockSpec((1, tn), lambda i, j: (0, j))],
        out_specs=out_specs,
        out_shape=out_shape,
        compiler_params=pltpu.CompilerParams(
            dimension_semantics=("parallel", "arbitrary"),
            vmem_limit_bytes=_vmem_limit(est)),
        name="ffn_down",
    )(hidden, w_down, h, g_next)


def _ple_kernel(hg_ref, ssq_ref, h_ref, wg_ref, p_ref, wp_ref, gf_ref, o_ref, ssq3_ref):
    j = pl.program_id(1)
    tm, D = hg_ref.shape
    tn = wg_ref.shape[1]

    z = jnp.dot(hg_ref[...], wg_ref[...], preferred_element_type=F32) * _row_scale(ssq_ref, D)
    gate = 1.0 / (1.0 + jnp.exp(-z))
    emb = jnp.dot(p_ref[...].astype(BF16), wp_ref[...], preferred_element_type=F32)
    h3 = h_ref[...] + gate * emb
    part = jnp.sum(h3 * h3, axis=-1, keepdims=True)

    @pl.when(j == 0)
    def _():
        ssq3_ref[...] = part

    @pl.when(j > 0)
    def _():
        ssq3_ref[...] += part

    for jj in range(D // tn):
        @pl.when(j == jj)
        def _(jj=jj):
            o_ref[:, jj * tn:(jj + 1) * tn] = h3

    @pl.when(j == pl.num_programs(1) - 1)
    def _():
        o_ref[...] = (o_ref[...] * _row_scale(ssq3_ref, D)) * gf_ref[...]


def ple_final(hg, ssq, h, w_gate, p, w_proj, g_final, *, tm, tn):
    T, D = h.shape
    P = p.shape[1]
    est = (2 * _nbytes((tm, D), F32) + 2 * _nbytes((tm, D), BF16)
           + 2 * _nbytes((D, tn), BF16) + 2 * _nbytes((P, tn), BF16)
           + 2 * _nbytes((tm, P), F32) + 6 * _nbytes((tm, tn), F32))
    return pl.pallas_call(
        _ple_kernel,
        grid=(T // tm, D // tn),
        in_specs=[pl.BlockSpec((tm, D), lambda i, j: (i, 0)),
                  pl.BlockSpec((tm, 1), lambda i, j: (i, 0)),
                  pl.BlockSpec((tm, tn), lambda i, j: (i, j)),
                  pl.BlockSpec((D, tn), lambda i, j: (0, j)),
                  pl.BlockSpec((tm, P), lambda i, j: (i, 0)),
                  pl.BlockSpec((P, tn), lambda i, j: (0, j)),
                  pl.BlockSpec((1, D), lambda i, j: (0, 0))],
        out_specs=pl.BlockSpec((tm, D), lambda i, j: (i, 0)),
        out_shape=jax.ShapeDtypeStruct((T, D), F32),
        scratch_shapes=[pltpu.VMEM((tm, 1), F32)],
        compiler_params=pltpu.CompilerParams(
            dimension_semantics=("parallel", "arbitrary"),
            vmem_limit_bytes=_vmem_limit(est)),
        name="ple_final",
    )(hg, ssq, h, w_gate, p, w_proj, g_final)


def _rope_tables(seq, dim):
    inv = 1.0 / (ROPE_BASE ** (jnp.arange(0, dim, 2, dtype=F32) / dim))
    ang = jnp.arange(seq, dtype=F32)[:, None] * inv[None, :]
    return jnp.cos(ang), jnp.sin(ang)


def _tile_config(seq):
    return dict(
        in_proj=dict(tm=min(512, seq), tn=768),
        retention_rows=min(1024, seq),
        mla_proj_tm=min(256, seq),
        attn_tile=min(512, seq),
        attn_heads=4,
        out_proj=dict(tm=min(1024, seq), tn=512),
        ffn_up=dict(tm=min(1024, seq), tf=512),
        ffn_down=dict(tm=min(512, seq), tn=512),
        ple=dict(tm=min(512, seq), tn=512),
    )


def _layer(h, p_i, w_in, g_attn, g_q_lora, g_kv_lora, w_uq, w_ukv, w_o, g_ffn,
           w_ffn_gate, w_ffn_up, conv_w, conv_b, w_ffn_down, g_ple, w_ple_gate,
           w_ple_proj, g_out, *, batch, seq):
    cfg = _tile_config(seq)
    D = h.shape[1]
    ret_w = RET_HEADS * RET_HEAD_DIM
    q_lora = w_uq.shape[0]
    kv_lora = w_ukv.shape[0]
    d_ff = w_ffn_gate.shape[1]

    in_w = w_in.shape[1]
    tn_in = cfg["in_proj"]["tn"]
    in_pad = pl.cdiv(in_w, tn_in) * tn_in
    proj = norm_matmul(h, g_attn.reshape(1, D), w_in.astype(BF16), n_out=in_pad,
                       out_dtype=BF16, **cfg["in_proj"])

    log_g = jnp.log1p(-jnp.exp2(-5.0 - jnp.arange(RET_HEADS, dtype=F32)))
    cos_r, sin_r = _rope_tables(seq, RET_HEAD_DIM)
    ro = retention_group(proj, log_g, cos_r, sin_r, batch=batch, seq=seq,
                         rows=cfg["retention_rows"])

    hq = MLA_NOPE + MLA_ROPE
    wq = w_uq.reshape(q_lora, MLA_HEADS, hq)
    wq = jnp.pad(wq, ((0, 0), (0, 0), (0, MLA_QK_PAD - hq)))
    wq = wq.reshape(q_lora, MLA_HEADS * MLA_QK_PAD).astype(BF16)
    wkv = w_ukv.reshape(kv_lora, MLA_HEADS, MLA_NOPE + MLA_V)
    wk = wkv[:, :, :MLA_NOPE].reshape(kv_lora, MLA_HEADS * MLA_NOPE).astype(BF16)
    wvt = wkv[:, :, MLA_NOPE:].reshape(kv_lora, MLA_HEADS * MLA_V).T.astype(BF16)
    cos_m, sin_m = _rope_tables(seq, MLA_ROPE)
    zeros = jnp.zeros_like(cos_m)
    fill = jnp.zeros((seq, LANES - MLA_ROPE), F32)
    rope_c = jnp.concatenate([cos_m, cos_m, fill], axis=1)
    rope_n = jnp.concatenate([-sin_m, zeros, fill], axis=1)
    rope_p = jnp.concatenate([zeros, sin_m, fill], axis=1)
    cq_off = 4 * ret_w
    ckv_off = cq_off + q_lora
    kr_off = ckv_off + kv_lora
    assert cq_off % q_lora == 0 and ckv_off % kv_lora == 0 and kr_off % (2 * LANES) == 0
    assert in_pad - kr_off >= 2 * LANES
    qc, kc, vt = mla_projections(
        proj, g_q_lora.reshape(1, q_lora), g_kv_lora.reshape(1, kv_lora), wq, wk, wvt,
        rope_c, rope_n, rope_p, seq=seq, tm=cfg["mla_proj_tm"], kv_tile=cfg["attn_tile"],
        cq_blk=cq_off // q_lora, ckv_blk=ckv_off // kv_lora, kr_blk=kr_off // (2 * LANES))
    mo = mla_attention(qc, kc, vt, batch=batch, seq=seq, tile=cfg["attn_tile"],
                       heads=cfg["attn_heads"])

    h1, hg1, ssq1 = out_projection(ro, mo, w_o.astype(BF16), h, g_ffn.reshape(1, D),
                                   **cfg["out_proj"])

    hidden = ffn_up(hg1, ssq1, w_ffn_gate.astype(BF16), w_ffn_up.astype(BF16), conv_w,
                    conv_b.reshape(1, d_ff), seq=seq, **cfg["ffn_up"])
    h2, hg2, ssq2 = ffn_down(hidden, w_ffn_down.astype(BF16), h1, g_ple.reshape(1, D),
                             **cfg["ffn_down"])

    return ple_final(hg2, ssq2, h2, w_ple_gate.astype(BF16), p_i, w_ple_proj.astype(BF16),
                     g_out.reshape(1, D), **cfg["ple"])


def kernel(x, p, w_in, g_attn, g_q_lora, g_kv_lora, w_uq, w_ukv, w_o, g_ffn, w_ffn_gate,
           w_ffn_up, conv_w, conv_b, w_ffn_down, g_ple, w_ple_gate, w_ple_proj, g_final):
    B, S, D = x.shape
    depth = p.shape[0]
    assert depth == 1, "the final RMSNorm is fused into the layer's last kernel"
    h = x.reshape(B * S, D)
    out = _layer(h, p[0].reshape(B * S, -1), w_in[0], g_attn[0], g_q_lora[0], g_kv_lora[0],
                 w_uq[0], w_ukv[0], w_o[0], g_ffn[0], w_ffn_gate[0], w_ffn_up[0], conv_w[0],
                 conv_b[0], w_ffn_down[0], g_ple[0], w_ple_gate[0], w_ple_proj[0], g_final,
                 batch=B, seq=S)
    return out.reshape(B, S, D)
```

```python
import functools
import math

import jax
import jax.numpy as jnp
from jax import lax
from jax.experimental import pallas as pl
from jax.experimental.pallas import tpu as pltpu

F32 = jnp.float32
BF16 = jnp.bfloat16

EPS = 1e-6
ROPE_BASE = 10000.0
CHUNK = 64
CHUNK_SHIFT = 6
RET_HEADS = 8
RET_HEAD_DIM = 256
MLA_HEADS = 16
MLA_NOPE = 128
MLA_ROPE = 64
MLA_V = 128
MLA_QK_PAD = 256
CONV_WIDTH = 3
RET_BLOCK = 256
LANES = 128
CONV_HALO = 8
MIB = 1024 * 1024
VMEM_LIMIT_CAP = 60 * MIB


def _vmem_limit(nbytes):
    return int(min(VMEM_LIMIT_CAP, nbytes + 16 * MIB))


def _nbytes(shape, dtype):
    return math.prod(shape) * jnp.dtype(dtype).itemsize


def _rms_rows(x_ref, g_ref, o_ref, n_rows, row_chunk=16, unroll=4):
    g = g_ref[...]

    def body(c, carry):
        r = pl.multiple_of(c * row_chunk, row_chunk)
        x = x_ref[pl.ds(r, row_chunk), :].astype(F32)
        ms = jnp.mean(x * x, axis=-1, keepdims=True)
        o_ref[pl.ds(r, row_chunk), :] = ((x * lax.rsqrt(ms + EPS)) * g).astype(o_ref.dtype)
        return carry

    lax.fori_loop(0, n_rows // row_chunk, body, 0, unroll=unroll)


def _row_scale(ssq_ref, width):
    return lax.rsqrt(ssq_ref[...] * (1.0 / width) + EPS)


def _silu(x):
    return x * (1.0 / (1.0 + jnp.exp(-x)))


def _norm_matmul_kernel(x_ref, g_ref, w_ref, o_ref, xn_ref):
    @pl.when(pl.program_id(1) == 0)
    def _():
        _rms_rows(x_ref, g_ref, xn_ref, x_ref.shape[0])

    o_ref[...] = jnp.dot(xn_ref[...], w_ref[...],
                         preferred_element_type=F32).astype(o_ref.dtype)


def _col_tiles(w, tn):
    K, N = w.shape
    nt = pl.cdiv(N, tn)
    w = w.astype(BF16)
    if N % tn:
        w = jnp.pad(w, ((0, 0), (0, nt * tn - N)))
    return w.reshape(K, nt, tn).transpose(1, 0, 2)


def _tile_spec(K, tn):
    return pl.BlockSpec((None, K, tn), lambda i, j: (j, 0, 0))


def norm_matmul(x, g, w_tiles, *, tm, out_dtype):
    T, K = x.shape
    nt, _, tn = w_tiles.shape
    N = nt * tn
    est = (2 * _nbytes((tm, K), x.dtype) + _nbytes((tm, K), BF16)
           + 2 * _nbytes((K, tn), BF16) + 2 * _nbytes((tm, tn), out_dtype)
           + _nbytes((tm, tn), F32))
    return pl.pallas_call(
        _norm_matmul_kernel,
        grid=(T // tm, nt),
        in_specs=[pl.BlockSpec((tm, K), lambda i, j: (i, 0)),
                  pl.BlockSpec((1, K), lambda i, j: (0, 0)),
                  _tile_spec(K, tn)],
        out_specs=pl.BlockSpec((tm, tn), lambda i, j: (i, j)),
        out_shape=jax.ShapeDtypeStruct((T, N), out_dtype),
        scratch_shapes=[pltpu.VMEM((tm, K), BF16)],
        compiler_params=pltpu.CompilerParams(
            dimension_semantics=("parallel", "arbitrary"),
            vmem_limit_bytes=_vmem_limit(est)),
        name="in_proj",
    )(x, g, w_tiles)


def _retention_kernel(lg_ref, q_ref, k_ref, v_ref, gate_ref, cos_ref, sin_ref, o_ref,
                      state_ref, dmat_ref, qdec_ref, kdec_ref, sdec_ref, *, k_scale):
    L = RET_BLOCK
    dk = q_ref.shape[1]
    half = dk // 2
    lg = lg_ref[pl.program_id(1)]

    @pl.when(pl.program_id(2) == 0)
    def _init():
        state_ref[...] = jnp.zeros_like(state_ref)
        n = lax.broadcasted_iota(jnp.int32, (L, L), 0)
        m = lax.broadcasted_iota(jnp.int32, (L, L), 1)
        cn = n >> CHUNK_SHIFT
        cm = m >> CHUNK_SHIFT
        d = (n - m).astype(F32)
        expo = jnp.where(cn == cm, jnp.abs(d), d)
        visible = cm <= cn
        dmat_ref[...] = jnp.where(visible, jnp.exp(lg * jnp.where(visible, expo, 0.0)), 0.0)
        row = lax.broadcasted_iota(jnp.int32, (L, dk), 0).astype(F32)
        qdec_ref[...] = jnp.exp(lg * (row + 1.0))
        kdec_ref[...] = jnp.exp(lg * (float(L - 1) - row))
        sdec_ref[...] = jnp.exp(jnp.full(sdec_ref.shape, lg * float(L), F32))

    def rope(x, cos, sin):
        x1 = x[:, :half]
        x2 = x[:, half:]
        return jnp.concatenate([x1 * cos - x2 * sin, x2 * cos + x1 * sin], axis=1)

    for sub in range(q_ref.shape[0] // L):
        rows = pl.ds(sub * L, L)
        cos = cos_ref[rows, :]
        sin = sin_ref[rows, :]
        q = rope(q_ref[rows, :].astype(F32), cos, sin)
        k = rope(k_ref[rows, :].astype(F32), cos, sin) * k_scale
        v = v_ref[rows, :]
        qb = q.astype(BF16)
        kb = k.astype(BF16)
        scores = lax.dot_general(qb, kb, (((1,), (1,)), ((), ())),
                                 preferred_element_type=F32)
        state = state_ref[...]
        inter = jnp.dot((q * qdec_ref[...]).astype(BF16), state.astype(BF16),
                        preferred_element_type=F32)
        kd = (k * kdec_ref[...]).astype(BF16)
        state_ref[...] = state * sdec_ref[0:1, :] + lax.dot_general(
            kd, v, (((0,), (0,)), ((), ())), preferred_element_type=F32)
        scores = (scores * dmat_ref[...]).astype(BF16)
        out = jnp.dot(scores, v, preferred_element_type=F32) + inter
        mu = jnp.mean(out, axis=-1, keepdims=True)
        cen = out - mu
        var = jnp.mean(cen * cen, axis=-1, keepdims=True)
        gate = gate_ref[rows, :].astype(F32)
        o_ref[rows, :] = (_silu(gate) * (cen * lax.rsqrt(var + EPS))).astype(o_ref.dtype)


def retention_group(proj, log_g, cos, sin, *, batch, seq, rows):
    T = proj.shape[0]
    H, dk = RET_HEADS, RET_HEAD_DIM
    nblk = seq // rows

    def col(offset):
        return pl.BlockSpec((rows, dk), lambda b, h, r: (b * nblk + r, offset + h))

    tab = pl.BlockSpec((rows, dk // 2), lambda b, h, r: (r, 0))
    est = 10 * _nbytes((rows, dk), BF16) + 4 * _nbytes((rows, dk // 2), F32) \
        + 4 * _nbytes((RET_BLOCK, dk), F32)
    return pl.pallas_call(
        functools.partial(_retention_kernel, k_scale=dk ** -0.5),
        grid=(batch, H, nblk),
        in_specs=[pl.BlockSpec(memory_space=pltpu.SMEM),
                  col(0), col(H), col(2 * H), col(3 * H), tab, tab],
        out_specs=pl.BlockSpec((rows, dk), lambda b, h, r: (b * nblk + r, h)),
        out_shape=jax.ShapeDtypeStruct((T, H * dk), BF16),
        scratch_shapes=[pltpu.VMEM((dk, dk), F32),
                        pltpu.VMEM((RET_BLOCK, RET_BLOCK), F32),
                        pltpu.VMEM((RET_BLOCK, dk), F32),
                        pltpu.VMEM((RET_BLOCK, dk), F32),
                        pltpu.VMEM((8, dk), F32)],
        compiler_params=pltpu.CompilerParams(
            dimension_semantics=("parallel", "parallel", "arbitrary"),
            vmem_limit_bytes=_vmem_limit(est)),
        name="retention",
    )(log_g, proj, proj, proj, proj, cos, sin)


def _mla_proj_kernel(cq_ref, ckv_ref, kr_ref, gq_ref, gkv_ref, wq_ref, wk_ref, wvt_ref,
                     rc_ref, rn_ref, rp_ref, q_ref, k_ref, vt_ref, cqn_ref, ckvn_ref,
                     *, scale):
    tm = cq_ref.shape[0]
    _rms_rows(cq_ref, gq_ref, cqn_ref, tm)
    _rms_rows(ckv_ref, gkv_ref, ckvn_ref, tm)
    rc = rc_ref[...]
    rn = rn_ref[...]
    rp = rp_ref[...]

    def rope(x):
        return (x * rc + pltpu.roll(x, LANES - MLA_ROPE // 2, 1) * rn
                + pltpu.roll(x, MLA_ROPE // 2, 1) * rp)

    krp = rope(kr_ref[:, :LANES].astype(F32)).astype(k_ref.dtype)
    cqn = cqn_ref[...]
    ckvn = ckvn_ref[...]
    W = MLA_QK_PAD
    for h in range(MLA_HEADS):
        qh = jnp.dot(cqn, wq_ref[:, h * W:(h + 1) * W], preferred_element_type=F32) * scale
        q_ref[:, h * W:h * W + LANES] = qh[:, :LANES].astype(q_ref.dtype)
        q_ref[:, h * W + LANES:(h + 1) * W] = rope(qh[:, LANES:]).astype(q_ref.dtype)
    for c in range(MLA_HEADS // 2):
        kn = jnp.dot(ckvn, wk_ref[:, c * W:(c + 1) * W], preferred_element_type=F32)
        for s in range(2):
            h = 2 * c + s
            k_ref[:, h * W:h * W + LANES] = kn[:, s * LANES:(s + 1) * LANES].astype(k_ref.dtype)
            k_ref[:, h * W + LANES:(h + 1) * W] = krp
    vt_ref[0] = lax.dot_general(wvt_ref[...], ckvn, (((1,), (1,)), ((), ())),
                                preferred_element_type=F32).astype(vt_ref.dtype)


def mla_projections(proj, g_q, g_kv, wq, wk, wvt, rope_c, rope_n, rope_p, *, seq, tm,
                    kv_tile, cq_blk, ckv_blk, kr_blk):
    T = proj.shape[0]
    q_lora = wq.shape[0]
    kv_lora = wk.shape[0]
    nq = wq.shape[1]
    nv = wvt.shape[0]
    nblk = seq // tm
    per_tile = kv_tile // tm
    scale = (MLA_NOPE + MLA_ROPE) ** -0.5 * math.log2(math.e)
    const = lambda i: (0, 0)
    tab = pl.BlockSpec((tm, LANES), lambda i: (i % nblk, 0))
    est = (2 * (_nbytes(wq.shape, BF16) + _nbytes(wk.shape, BF16) + _nbytes(wvt.shape, BF16))
           + 2 * (2 * _nbytes((tm, nq), BF16) + _nbytes((tm, nv), BF16))
           + 3 * _nbytes((tm, q_lora + kv_lora + 2 * LANES), BF16)
           + 6 * _nbytes((tm, LANES), F32) + _nbytes((tm, nv), F32))
    return pl.pallas_call(
        functools.partial(_mla_proj_kernel, scale=scale),
        grid=(T // tm,),
        in_specs=[pl.BlockSpec((tm, q_lora), lambda i: (i, cq_blk)),
                  pl.BlockSpec((tm, kv_lora), lambda i: (i, ckv_blk)),
                  pl.BlockSpec((tm, 2 * LANES), lambda i: (i, kr_blk)),
                  pl.BlockSpec((1, q_lora), const),
                  pl.BlockSpec((1, kv_lora), const),
                  pl.BlockSpec(wq.shape, const),
                  pl.BlockSpec(wk.shape, const),
                  pl.BlockSpec(wvt.shape, const),
                  tab, tab, tab],
        out_specs=[pl.BlockSpec((tm, nq), lambda i: (i, 0)),
                   pl.BlockSpec((tm, nq), lambda i: (i, 0)),
                   pl.BlockSpec((1, nv, tm), lambda i: (i // per_tile, 0, i % per_tile))],
        out_shape=[jax.ShapeDtypeStruct((T, nq), BF16),
                   jax.ShapeDtypeStruct((T, nq), BF16),
                   jax.ShapeDtypeStruct((T // kv_tile, nv, kv_tile), BF16)],
        scratch_shapes=[pltpu.VMEM((tm, q_lora), BF16),
                        pltpu.VMEM((tm, kv_lora), BF16)],
        compiler_params=pltpu.CompilerParams(
            dimension_semantics=("parallel",),
            vmem_limit_bytes=_vmem_limit(est)),
        name="mla_proj",
    )(proj, proj, proj, g_q, g_kv, wq, wk, wvt, rope_c, rope_n, rope_p)


def _mla_attn_kernel(q_ref, k_ref, vt_ref, o_ref, m_ref, l_ref, acc_ref, *, tile, heads):
    qi = pl.program_id(2)
    W, dv = MLA_QK_PAD, MLA_V
    qs = [q_ref[:, h * W:(h + 1) * W] for h in range(heads)]
    m_ref[...] = jnp.full(m_ref.shape, -jnp.inf, F32)
    l_ref[...] = jnp.zeros(l_ref.shape, F32)
    acc_ref[...] = jnp.zeros(acc_ref.shape, F32)

    def process(tiles):
        scores = []
        for j, _ in tiles:
            start = pl.multiple_of(j * tile, tile)
            scores.append([
                lax.dot_general(k_ref[pl.ds(start, tile), h * W:(h + 1) * W], qs[h],
                                (((1,), (1,)), ((), ())), preferred_element_type=F32)
                for h in range(heads)])
        for t, (j, mask) in enumerate(tiles):
            for h in range(heads):
                s = scores[t][h]
                if mask is not None:
                    s = jnp.where(mask, s, -jnp.inf)
                m = m_ref[h]
                m_new = jnp.maximum(m, jnp.max(s, axis=0, keepdims=True))
                alpha = jnp.exp2(m - m_new)
                p = jnp.exp2(s - m_new)
                vt = vt_ref[j, h * dv:(h + 1) * dv, :]
                l_ref[h] = alpha * l_ref[h] + jnp.sum(p, axis=0, keepdims=True)
                acc_ref[h] = alpha * acc_ref[h] + jnp.dot(vt, p.astype(vt.dtype),
                                                         preferred_element_type=F32)
                m_ref[h] = m_new

    key_chunk = lax.broadcasted_iota(jnp.int32, (tile, tile), 0) >> CHUNK_SHIFT
    qry_chunk = lax.broadcasted_iota(jnp.int32, (tile, tile), 1) >> CHUNK_SHIFT
    diag_mask = key_chunk <= qry_chunk

    @pl.when(qi == 0)
    def _():
        process([(qi, diag_mask)])

    @pl.when(qi > 0)
    def _():
        lead = (qi + 1) % 2

        @pl.when(lead == 1)
        def _():
            process([(0, None)])

        def pair(i, carry):
            j = lead + 2 * i
            process([(j, None), (j + 1, None)])
            return carry

        lax.fori_loop(0, (qi + 1 - lead) // 2 - 1, pair, 0)
        process([(qi - 1, None), (qi, diag_mask)])

    for h in range(heads):
        o_ref[:, h * dv:(h + 1) * dv] = (acc_ref[h] * (1.0 / l_ref[h])).T.astype(o_ref.dtype)


def mla_attention(qc, kc, vt, *, batch, seq, tile, heads):
    T = qc.shape[0]
    H, W, dv = MLA_HEADS, MLA_QK_PAD, MLA_V
    nq = seq // tile
    est = (2 * heads * (_nbytes((seq, W), BF16) + _nbytes((seq, dv), BF16)
                        + _nbytes((tile, W), BF16) + _nbytes((tile, dv), BF16))
           + 4 * heads * _nbytes((tile, tile), F32))
    return pl.pallas_call(
        functools.partial(_mla_attn_kernel, tile=tile, heads=heads),
        grid=(batch, H // heads, nq),
        in_specs=[pl.BlockSpec((tile, heads * W), lambda b, h, i: (b * nq + i, h)),
                  pl.BlockSpec((seq, heads * W), lambda b, h, i: (b, h)),
                  pl.BlockSpec((nq, heads * dv, tile), lambda b, h, i: (b, h, 0))],
        out_specs=pl.BlockSpec((tile, heads * dv), lambda b, h, i: (b * nq + i, h)),
        out_shape=jax.ShapeDtypeStruct((T, H * dv), BF16),
        scratch_shapes=[pltpu.VMEM((heads, 1, tile), F32),
                        pltpu.VMEM((heads, 1, tile), F32),
                        pltpu.VMEM((heads, dv, tile), F32)],
        compiler_params=pltpu.CompilerParams(
            dimension_semantics=("parallel", "parallel", "arbitrary"),
            vmem_limit_bytes=_vmem_limit(est)),
        name="mla_attn",
    )(qc, kc, vt)


def _emit_residual(h, g_ref, o_ref, hg_ref, ssq_ref):
    o_ref[...] = h
    hg_ref[...] = (h * g_ref[...]).astype(hg_ref.dtype)
    part = jnp.sum(h * h, axis=-1, keepdims=True)

    @pl.when(pl.program_id(1) == 0)
    def _():
        ssq_ref[...] = part

    @pl.when(pl.program_id(1) > 0)
    def _():
        ssq_ref[...] += part


def _residual_out(T, N, tm, tn):
    specs = [pl.BlockSpec((tm, tn), lambda i, j: (i, j)),
             pl.BlockSpec((tm, tn), lambda i, j: (i, j)),
             pl.BlockSpec((tm, 1), lambda i, j: (i, 0))]
    shapes = [jax.ShapeDtypeStruct((T, N), F32), jax.ShapeDtypeStruct((T, N), BF16),
              jax.ShapeDtypeStruct((T, 1), F32)]
    return specs, shapes


def _out_proj_kernel(ro_ref, mo_ref, wr_ref, wm_ref, x_ref, g_ref, o_ref, hg_ref, ssq_ref):
    acc = jnp.dot(ro_ref[...], wr_ref[...], preferred_element_type=F32)
    acc = acc + jnp.dot(mo_ref[...], wm_ref[...], preferred_element_type=F32)
    _emit_residual(x_ref[...] + acc, g_ref, o_ref, hg_ref, ssq_ref)


def out_projection(ro, mo, wr_tiles, wm_tiles, x, g_next, *, tm):
    T, kr = ro.shape
    km = mo.shape[1]
    nt, _, tn = wr_tiles.shape
    N = nt * tn
    est = (4 * _nbytes((tm, kr), BF16) + 4 * _nbytes((kr, tn), BF16)
           + 7 * _nbytes((tm, tn), F32))
    out_specs, out_shape = _residual_out(T, N, tm, tn)
    return pl.pallas_call(
        _out_proj_kernel,
        grid=(T // tm, nt),
        in_specs=[pl.BlockSpec((tm, kr), lambda i, j: (i, 0)),
                  pl.BlockSpec((tm, km), lambda i, j: (i, 0)),
                  _tile_spec(kr, tn),
                  _tile_spec(km, tn),
                  pl.BlockSpec((tm, tn), lambda i, j: (i, j)),
                  pl.BlockSpec((1, tn), lambda i, j: (0, j))],
        out_specs=out_specs,
        out_shape=out_shape,
        compiler_params=pltpu.CompilerParams(
            dimension_semantics=("parallel", "arbitrary"),
            vmem_limit_bytes=_vmem_limit(est)),
        name="out_proj",
    )(ro, mo, wr_tiles, wm_tiles, x, g_next)


def _ffn_up_kernel(hg_ref, ssq_ref, wg_ref, wu_ref, cw_ref, cb_ref, o_ref, g_ref,
                   *, tiles_per_seq, edge_cols):
    tm, tf = o_ref.shape
    halo = CONV_HALO

    @pl.when(pl.program_id(1) % tiles_per_seq == 0)
    def _():
        g_ref[0:halo, :] = jnp.zeros((halo, g_ref.shape[1]), F32)

    hg = hg_ref[...]
    r = _row_scale(ssq_ref, hg_ref.shape[1])

    def compute(nc):
        g_ref[halo:halo + tm, :nc] = jnp.dot(hg, wg_ref[:, :nc],
                                             preferred_element_type=F32) * r
        up = jnp.dot(hg, wu_ref[:, :nc], preferred_element_type=F32) * r
        a = cb_ref[:, :nc]
        for j in range(CONV_WIDTH):
            shift = CONV_WIDTH - 1 - j
            a = a + g_ref[halo - shift:halo - shift + tm, :nc] * cw_ref[j:j + 1, :nc]
        o_ref[:, :nc] = (_silu(a) * up).astype(o_ref.dtype)

    if edge_cols == 0:
        compute(tf)
    else:
        last = pl.num_programs(0) - 1
        pl.when(pl.program_id(0) < last)(lambda: compute(tf))
        pl.when(pl.program_id(0) == last)(lambda: compute(edge_cols))
    g_ref[0:halo, :] = g_ref[tm:tm + halo, :]


def ffn_up(hg, ssq, w_gate, w_up, conv_w, conv_b, *, seq, tm, tf):
    T, K = hg.shape
    FF = w_gate.shape[1]
    est = (2 * _nbytes((tm, K), BF16) + 4 * _nbytes((K, tf), BF16)
           + 2 * _nbytes((tm, tf), BF16) + 4 * _nbytes((tm + CONV_HALO, tf), F32)
           + 2 * _nbytes((tm, LANES), F32))
    return pl.pallas_call(
        functools.partial(_ffn_up_kernel, tiles_per_seq=seq // tm, edge_cols=FF % tf),
        grid=(pl.cdiv(FF, tf), T // tm),
        in_specs=[pl.BlockSpec((tm, K), lambda j, i: (i, 0)),
                  pl.BlockSpec((tm, 1), lambda j, i: (i, 0)),
                  pl.BlockSpec((K, tf), lambda j, i: (0, j)),
                  pl.BlockSpec((K, tf), lambda j, i: (0, j)),
                  pl.BlockSpec((CONV_WIDTH, tf), lambda j, i: (0, j)),
                  pl.BlockSpec((1, tf), lambda j, i: (0, j))],
        out_specs=pl.BlockSpec((tm, tf), lambda j, i: (i, j)),
        out_shape=jax.ShapeDtypeStruct((T, FF), BF16),
        scratch_shapes=[pltpu.VMEM((tm + CONV_HALO, tf), F32)],
        compiler_params=pltpu.CompilerParams(
            dimension_semantics=("arbitrary", "arbitrary"),
            vmem_limit_bytes=_vmem_limit(est)),
        name="ffn_up",
    )(hg, ssq, w_gate, w_up, conv_w, conv_b)


def _ffn_down_kernel(a_ref, w_ref, h_ref, g_ref, o_ref, hg_ref, ssq_ref):
    acc = jnp.dot(a_ref[...], w_ref[...], preferred_element_type=F32)
    _emit_residual(h_ref[...] + acc, g_ref, o_ref, hg_ref, ssq_ref)


def ffn_down(hidden, w_tiles, h, g_next, *, tm):
    T, K = hidden.shape
    nt, _, tn = w_tiles.shape
    N = nt * tn
    est = (2 * _nbytes((tm, K), BF16) + 2 * _nbytes((K, tn), BF16)
           + 7 * _nbytes((tm, tn), F32))
    out_specs, out_shape = _residual_out(T, N, tm, tn)
    return pl.pallas_call(
        _ffn_down_kernel,
        grid=(T // tm, nt),
        in_specs=[pl.BlockSpec((tm, K), lambda i, j: (i, 0)),
                  _tile_spec(K, tn),
                  pl.BlockSpec((tm, tn), lambda i, j: (i, j)),
                  pl.BlockSpec((1, tn), lambda i, j: (0, j))],
        out_specs=out_specs,
        out_shape=out_shape,
        compiler_params=pltpu.CompilerParams(
            dimension_semantics=("parallel", "arbitrary"),
            vmem_limit_bytes=_vmem_limit(est)),
        name="ffn_down",
    )(hidden, w_tiles, h, g_next)


def _ple_kernel(hg_ref, ssq_ref, h_ref, wg_ref, p_ref, wp_ref, gf_ref, o_ref, ssq3_ref):
    j = pl.program_id(1)
    tm, D = hg_ref.shape
    tn = wg_ref.shape[1]

    z = jnp.dot(hg_ref[...], wg_ref[...], preferred_element_type=F32) * _row_scale(ssq_ref, D)
    gate = 1.0 / (1.0 + jnp.exp(-z))
    emb = jnp.dot(p_ref[...].astype(BF16), wp_ref[...], preferred_element_type=F32)
    h3 = h_ref[...] + gate * emb
    part = jnp.sum(h3 * h3, axis=-1, keepdims=True)

    @pl.when(j == 0)
    def _():
        ssq3_ref[...] = part

    @pl.when(j > 0)
    def _():
        ssq3_ref[...] += part

    for jj in range(D // tn):
        @pl.when(j == jj)
        def _(jj=jj):
            o_ref[:, jj * tn:(jj + 1) * tn] = h3

    @pl.when(j == pl.num_programs(1) - 1)
    def _():
        o_ref[...] = (o_ref[...] * _row_scale(ssq3_ref, D)) * gf_ref[...]


def ple_final(hg, ssq, h, w_tiles, p, w_proj, g_final, *, tm):
    T, D = h.shape
    P = p.shape[1]
    nt, _, tn = w_tiles.shape
    assert nt * tn == D
    est = (2 * _nbytes((tm, D), F32) + 2 * _nbytes((tm, D), BF16)
           + 2 * _nbytes((D, tn), BF16) + 2 * _nbytes((P, tn), BF16)
           + 2 * _nbytes((tm, P), F32) + 6 * _nbytes((tm, tn), F32))
    return pl.pallas_call(
        _ple_kernel,
        grid=(T // tm, nt),
        in_specs=[pl.BlockSpec((tm, D), lambda i, j: (i, 0)),
                  pl.BlockSpec((tm, 1), lambda i, j: (i, 0)),
                  pl.BlockSpec((tm, tn), lambda i, j: (i, j)),
                  _tile_spec(D, tn),
                  pl.BlockSpec((tm, P), lambda i, j: (i, 0)),
                  pl.BlockSpec((P, tn), lambda i, j: (0, j)),
                  pl.BlockSpec((1, D), lambda i, j: (0, 0))],
        out_specs=pl.BlockSpec((tm, D), lambda i, j: (i, 0)),
        out_shape=jax.ShapeDtypeStruct((T, D), F32),
        scratch_shapes=[pltpu.VMEM((tm, 1), F32)],
        compiler_params=pltpu.CompilerParams(
            dimension_semantics=("parallel", "arbitrary"),
            vmem_limit_bytes=_vmem_limit(est)),
        name="ple_final",
    )(hg, ssq, h, w_tiles, p, w_proj, g_final)


def _rope_tables(seq, dim):
    inv = 1.0 / (ROPE_BASE ** (jnp.arange(0, dim, 2, dtype=F32) / dim))
    ang = jnp.arange(seq, dtype=F32)[:, None] * inv[None, :]
    return jnp.cos(ang), jnp.sin(ang)


def _tile_config(seq):
    return dict(
        in_proj=dict(tm=min(512, seq), tn=768),
        retention_rows=min(1024, seq),
        mla_proj_tm=min(256, seq),
        attn_tile=min(512, seq),
        attn_heads=4,
        out_proj=dict(tm=min(1024, seq), tn=512),
        ffn_up=dict(tm=min(1024, seq), tf=512),
        ffn_down=dict(tm=min(512, seq), tn=512),
        ple=dict(tm=min(512, seq), tn=512),
    )


def _layer(h, p_i, w_in, g_attn, g_q_lora, g_kv_lora, w_uq, w_ukv, w_o, g_ffn,
           w_ffn_gate, w_ffn_up, conv_w, conv_b, w_ffn_down, g_ple, w_ple_gate,
           w_ple_proj, g_out, *, batch, seq):
    cfg = _tile_config(seq)
    D = h.shape[1]
    ret_w = RET_HEADS * RET_HEAD_DIM
    q_lora = w_uq.shape[0]
    kv_lora = w_ukv.shape[0]
    d_ff = w_ffn_gate.shape[1]

    in_w = w_in.shape[1]
    tn_in = cfg["in_proj"]["tn"]
    in_pad = pl.cdiv(in_w, tn_in) * tn_in
    proj = norm_matmul(h, g_attn.reshape(1, D), _col_tiles(w_in, tn_in), out_dtype=BF16,
                       tm=cfg["in_proj"]["tm"])

    log_g = jnp.log1p(-jnp.exp2(-5.0 - jnp.arange(RET_HEADS, dtype=F32)))
    cos_r, sin_r = _rope_tables(seq, RET_HEAD_DIM)
    ro = retention_group(proj, log_g, cos_r, sin_r, batch=batch, seq=seq,
                         rows=cfg["retention_rows"])

    hq = MLA_NOPE + MLA_ROPE
    wq = w_uq.reshape(q_lora, MLA_HEADS, hq)
    wq = jnp.pad(wq, ((0, 0), (0, 0), (0, MLA_QK_PAD - hq)))
    wq = wq.reshape(q_lora, MLA_HEADS * MLA_QK_PAD).astype(BF16)
    wkv = w_ukv.reshape(kv_lora, MLA_HEADS, MLA_NOPE + MLA_V)
    wk = wkv[:, :, :MLA_NOPE].reshape(kv_lora, MLA_HEADS * MLA_NOPE).astype(BF16)
    wvt = wkv[:, :, MLA_NOPE:].reshape(kv_lora, MLA_HEADS * MLA_V).T.astype(BF16)
    cos_m, sin_m = _rope_tables(seq, MLA_ROPE)
    zeros = jnp.zeros_like(cos_m)
    fill = jnp.zeros((seq, LANES - MLA_ROPE), F32)
    rope_c = jnp.concatenate([cos_m, cos_m, fill], axis=1)
    rope_n = jnp.concatenate([-sin_m, zeros, fill], axis=1)
    rope_p = jnp.concatenate([zeros, sin_m, fill], axis=1)
    cq_off = 4 * ret_w
    ckv_off = cq_off + q_lora
    kr_off = ckv_off + kv_lora
    assert cq_off % q_lora == 0 and ckv_off % kv_lora == 0 and kr_off % (2 * LANES) == 0
    assert in_pad - kr_off >= 2 * LANES
    qc, kc, vt = mla_projections(
        proj, g_q_lora.reshape(1, q_lora), g_kv_lora.reshape(1, kv_lora), wq, wk, wvt,
        rope_c, rope_n, rope_p, seq=seq, tm=cfg["mla_proj_tm"], kv_tile=cfg["attn_tile"],
        cq_blk=cq_off // q_lora, ckv_blk=ckv_off // kv_lora, kr_blk=kr_off // (2 * LANES))
    mo = mla_attention(qc, kc, vt, batch=batch, seq=seq, tile=cfg["attn_tile"],
                       heads=cfg["attn_heads"])

    tn_o = cfg["out_proj"]["tn"]
    h1, hg1, ssq1 = out_projection(ro, mo, _col_tiles(w_o[:ret_w], tn_o),
                                   _col_tiles(w_o[ret_w:], tn_o), h, g_ffn.reshape(1, D),
                                   tm=cfg["out_proj"]["tm"])

    hidden = ffn_up(hg1, ssq1, w_ffn_gate.astype(BF16), w_ffn_up.astype(BF16), conv_w,
                    conv_b.reshape(1, d_ff), seq=seq, **cfg["ffn_up"])
    h2, hg2, ssq2 = ffn_down(hidden, _col_tiles(w_ffn_down, cfg["ffn_down"]["tn"]), h1,
                             g_ple.reshape(1, D), tm=cfg["ffn_down"]["tm"])

    return ple_final(hg2, ssq2, h2, _col_tiles(w_ple_gate, cfg["ple"]["tn"]), p_i,
                     w_ple_proj.astype(BF16), g_out.reshape(1, D), tm=cfg["ple"]["tm"])


def kernel(x, p, w_in, g_attn, g_q_lora, g_kv_lora, w_uq, w_ukv, w_o, g_ffn, w_ffn_gate,
           w_ffn_up, conv_w, conv_b, w_ffn_down, g_ple, w_ple_gate, w_ple_proj, g_final):
    B, S, D = x.shape
    depth = p.shape[0]
    assert depth == 1, "the final RMSNorm is fused into the layer's last kernel"
    h = x.reshape(B * S, D)
    out = _layer(h, p[0].reshape(B * S, -1), w_in[0], g_attn[0], g_q_lora[0], g_kv_lora[0],
                 w_uq[0], w_ukv[0], w_o[0], g_ffn[0], w_ffn_gate[0], w_ffn_up[0], conv_w[0],
                 conv_b[0], w_ffn_down[0], g_ple[0], w_ple_gate[0], w_ple_proj[0], g_final,
                 batch=B, seq=S)
    return out.reshape(B, S, D)
```

```python
import functools
import math

import jax
import jax.numpy as jnp
from jax import lax
from jax.experimental import pallas as pl
from jax.experimental.pallas import tpu as pltpu

F32 = jnp.float32
BF16 = jnp.bfloat16

EPS = 1e-6
ROPE_BASE = 10000.0
CHUNK = 64
CHUNK_SHIFT = 6
RET_HEADS = 8
RET_HEAD_DIM = 256
MLA_HEADS = 16
MLA_NOPE = 128
MLA_ROPE = 64
MLA_V = 128
MLA_QK_PAD = 256
CONV_WIDTH = 3
RET_BLOCK = 256
LANES = 128
CONV_HALO = 8
MIB = 1024 * 1024
VMEM_LIMIT_CAP = 60 * MIB


def _vmem_limit(nbytes):
    return int(min(VMEM_LIMIT_CAP, nbytes + 16 * MIB))


def _nbytes(shape, dtype):
    return math.prod(shape) * jnp.dtype(dtype).itemsize


def _rms_rows(x_ref, g_ref, o_ref, n_rows, row_chunk=16, unroll=4):
    g = g_ref[...]

    def body(c, carry):
        r = pl.multiple_of(c * row_chunk, row_chunk)
        x = x_ref[pl.ds(r, row_chunk), :].astype(F32)
        ms = jnp.mean(x * x, axis=-1, keepdims=True)
        o_ref[pl.ds(r, row_chunk), :] = ((x * lax.rsqrt(ms + EPS)) * g).astype(o_ref.dtype)
        return carry

    lax.fori_loop(0, n_rows // row_chunk, body, 0, unroll=unroll)


def _row_scale(ssq_ref, width):
    return lax.rsqrt(ssq_ref[...] * (1.0 / width) + EPS)


def _silu(x):
    return x * (1.0 / (1.0 + jnp.exp(-x)))


def _norm_matmul_kernel(x_ref, g_ref, w_ref, o_ref, xn_ref, *, edge_cols):
    j = pl.program_id(1)
    last = pl.num_programs(1) - 1

    @pl.when(j == 0)
    def _():
        _rms_rows(x_ref, g_ref, xn_ref, x_ref.shape[0])

    res = jnp.dot(xn_ref[...], w_ref[...], preferred_element_type=F32)
    if edge_cols == 0:
        o_ref[...] = res.astype(o_ref.dtype)
    else:
        @pl.when(j < last)
        def _():
            o_ref[...] = res.astype(o_ref.dtype)

        @pl.when(j == last)
        def _():
            col = lax.broadcasted_iota(jnp.int32, res.shape, 1)
            o_ref[...] = jnp.where(col < edge_cols, res, 0.0).astype(o_ref.dtype)


def norm_matmul(x, g, w, *, n_out, tm, tn, out_dtype):
    T, K = x.shape
    N = n_out
    assert N % tn == 0 and N - tn < w.shape[1] <= N
    est = (2 * _nbytes((tm, K), x.dtype) + _nbytes((tm, K), BF16)
           + 2 * _nbytes((K, tn), w.dtype) + 2 * _nbytes((tm, tn), out_dtype)
           + _nbytes((tm, tn), F32))
    return pl.pallas_call(
        functools.partial(_norm_matmul_kernel, edge_cols=w.shape[1] % tn),
        grid=(T // tm, N // tn),
        in_specs=[pl.BlockSpec((tm, K), lambda i, j: (i, 0)),
                  pl.BlockSpec((1, K), lambda i, j: (0, 0)),
                  pl.BlockSpec((K, tn), lambda i, j: (0, j))],
        out_specs=pl.BlockSpec((tm, tn), lambda i, j: (i, j)),
        out_shape=jax.ShapeDtypeStruct((T, N), out_dtype),
        scratch_shapes=[pltpu.VMEM((tm, K), BF16)],
        compiler_params=pltpu.CompilerParams(
            dimension_semantics=("parallel", "arbitrary"),
            vmem_limit_bytes=_vmem_limit(est)),
        name="in_proj",
    )(x, g, w)


def _retention_kernel(lg_ref, q_ref, k_ref, v_ref, gate_ref, cos_ref, sin_ref, o_ref,
                      state_ref, dmat_ref, qdec_ref, kdec_ref, sdec_ref, *, k_scale):
    L = RET_BLOCK
    dk = q_ref.shape[1]
    half = dk // 2
    lg = lg_ref[pl.program_id(1)]

    @pl.when(pl.program_id(2) == 0)
    def _init():
        state_ref[...] = jnp.zeros_like(state_ref)
        n = lax.broadcasted_iota(jnp.int32, (L, L), 0)
        m = lax.broadcasted_iota(jnp.int32, (L, L), 1)
        cn = n >> CHUNK_SHIFT
        cm = m >> CHUNK_SHIFT
        d = (n - m).astype(F32)
        expo = jnp.where(cn == cm, jnp.abs(d), d)
        visible = cm <= cn
        dmat_ref[...] = jnp.where(visible, jnp.exp(lg * jnp.where(visible, expo, 0.0)), 0.0)
        row = lax.broadcasted_iota(jnp.int32, (L, dk), 0).astype(F32)
        qdec_ref[...] = jnp.exp(lg * (row + 1.0))
        kdec_ref[...] = jnp.exp(lg * (float(L - 1) - row))
        sdec_ref[...] = jnp.exp(jnp.full(sdec_ref.shape, lg * float(L), F32))

    def rope(x, cos, sin):
        x1 = x[:, :half]
        x2 = x[:, half:]
        return jnp.concatenate([x1 * cos - x2 * sin, x2 * cos + x1 * sin], axis=1)

    for sub in range(q_ref.shape[0] // L):
        rows = pl.ds(sub * L, L)
        cos = cos_ref[rows, :]
        sin = sin_ref[rows, :]
        q = rope(q_ref[rows, :].astype(F32), cos, sin)
        k = rope(k_ref[rows, :].astype(F32), cos, sin) * k_scale
        v = v_ref[rows, :]
        qb = q.astype(BF16)
        kb = k.astype(BF16)
        scores = lax.dot_general(qb, kb, (((1,), (1,)), ((), ())),
                                 preferred_element_type=F32)
        state = state_ref[...]
        inter = jnp.dot((q * qdec_ref[...]).astype(BF16), state.astype(BF16),
                        preferred_element_type=F32)
        kd = (k * kdec_ref[...]).astype(BF16)
        state_ref[...] = state * sdec_ref[0:1, :] + lax.dot_general(
            kd, v, (((0,), (0,)), ((), ())), preferred_element_type=F32)
        scores = (scores * dmat_ref[...]).astype(BF16)
        out = jnp.dot(scores, v, preferred_element_type=F32) + inter
        mu = jnp.mean(out, axis=-1, keepdims=True)
        cen = out - mu
        var = jnp.mean(cen * cen, axis=-1, keepdims=True)
        gate = gate_ref[rows, :].astype(F32)
        o_ref[rows, :] = (_silu(gate) * (cen * lax.rsqrt(var + EPS))).astype(o_ref.dtype)


def retention_group(proj, log_g, cos, sin, *, batch, seq, rows):
    T = proj.shape[0]
    H, dk = RET_HEADS, RET_HEAD_DIM
    nblk = seq // rows

    def col(offset):
        return pl.BlockSpec((rows, dk), lambda b, h, r: (b * nblk + r, offset + h))

    tab = pl.BlockSpec((rows, dk // 2), lambda b, h, r: (r, 0))
    est = 10 * _nbytes((rows, dk), BF16) + 4 * _nbytes((rows, dk // 2), F32) \
        + 4 * _nbytes((RET_BLOCK, dk), F32)
    return pl.pallas_call(
        functools.partial(_retention_kernel, k_scale=dk ** -0.5),
        grid=(batch, H, nblk),
        in_specs=[pl.BlockSpec(memory_space=pltpu.SMEM),
                  col(0), col(H), col(2 * H), col(3 * H), tab, tab],
        out_specs=pl.BlockSpec((rows, dk), lambda b, h, r: (b * nblk + r, h)),
        out_shape=jax.ShapeDtypeStruct((T, H * dk), BF16),
        scratch_shapes=[pltpu.VMEM((dk, dk), F32),
                        pltpu.VMEM((RET_BLOCK, RET_BLOCK), F32),
                        pltpu.VMEM((RET_BLOCK, dk), F32),
                        pltpu.VMEM((RET_BLOCK, dk), F32),
                        pltpu.VMEM((8, dk), F32)],
        compiler_params=pltpu.CompilerParams(
            dimension_semantics=("parallel", "parallel", "arbitrary"),
            vmem_limit_bytes=_vmem_limit(est)),
        name="retention",
    )(log_g, proj, proj, proj, proj, cos, sin)


def _mla_proj_kernel(cq_ref, ckv_ref, kr_ref, gq_ref, gkv_ref, wq_ref, wk_ref, wvt_ref,
                     rc_ref, rn_ref, rp_ref, q_ref, k_ref, vt_ref, cqn_ref, ckvn_ref,
                     *, scale):
    tm = cq_ref.shape[0]
    _rms_rows(cq_ref, gq_ref, cqn_ref, tm)
    _rms_rows(ckv_ref, gkv_ref, ckvn_ref, tm)
    rc = rc_ref[...]
    rn = rn_ref[...]
    rp = rp_ref[...]

    def rope(x):
        return (x * rc + pltpu.roll(x, LANES - MLA_ROPE // 2, 1) * rn
                + pltpu.roll(x, MLA_ROPE // 2, 1) * rp)

    krp = rope(kr_ref[:, :LANES].astype(F32)).astype(k_ref.dtype)
    cqn = cqn_ref[...]
    ckvn = ckvn_ref[...]
    W = MLA_QK_PAD
    for h in range(MLA_HEADS):
        qh = jnp.dot(cqn, wq_ref[:, h * W:(h + 1) * W], preferred_element_type=F32) * scale
        q_ref[:, h * W:h * W + LANES] = qh[:, :LANES].astype(q_ref.dtype)
        q_ref[:, h * W + LANES:(h + 1) * W] = rope(qh[:, LANES:]).astype(q_ref.dtype)
    for c in range(MLA_HEADS // 2):
        kn = jnp.dot(ckvn, wk_ref[:, c * W:(c + 1) * W], preferred_element_type=F32)
        for s in range(2):
            h = 2 * c + s
            k_ref[:, h * W:h * W + LANES] = kn[:, s * LANES:(s + 1) * LANES].astype(k_ref.dtype)
            k_ref[:, h * W + LANES:(h + 1) * W] = krp
    vt_ref[0] = lax.dot_general(wvt_ref[...], ckvn, (((1,), (1,)), ((), ())),
                                preferred_element_type=F32).astype(vt_ref.dtype)


def mla_projections(proj, g_q, g_kv, wq, wk, wvt, rope_c, rope_n, rope_p, *, seq, tm,
                    kv_tile, cq_blk, ckv_blk, kr_blk):
    T = proj.shape[0]
    q_lora = wq.shape[0]
    kv_lora = wk.shape[0]
    nq = wq.shape[1]
    nv = wvt.shape[0]
    nblk = seq // tm
    per_tile = kv_tile // tm
    scale = (MLA_NOPE + MLA_ROPE) ** -0.5 * math.log2(math.e)
    const = lambda i: (0, 0)
    tab = pl.BlockSpec((tm, LANES), lambda i: (i % nblk, 0))
    est = (2 * (_nbytes(wq.shape, BF16) + _nbytes(wk.shape, BF16) + _nbytes(wvt.shape, BF16))
           + 2 * (2 * _nbytes((tm, nq), BF16) + _nbytes((tm, nv), BF16))
           + 3 * _nbytes((tm, q_lora + kv_lora + 2 * LANES), BF16)
           + 6 * _nbytes((tm, LANES), F32) + _nbytes((tm, nv), F32))
    return pl.pallas_call(
        functools.partial(_mla_proj_kernel, scale=scale),
        grid=(T // tm,),
        in_specs=[pl.BlockSpec((tm, q_lora), lambda i: (i, cq_blk)),
                  pl.BlockSpec((tm, kv_lora), lambda i: (i, ckv_blk)),
                  pl.BlockSpec((tm, 2 * LANES), lambda i: (i, kr_blk)),
                  pl.BlockSpec((1, q_lora), const),
                  pl.BlockSpec((1, kv_lora), const),
                  pl.BlockSpec(wq.shape, const),
                  pl.BlockSpec(wk.shape, const),
                  pl.BlockSpec(wvt.shape, const),
                  tab, tab, tab],
        out_specs=[pl.BlockSpec((tm, nq), lambda i: (i, 0)),
                   pl.BlockSpec((tm, nq), lambda i: (i, 0)),
                   pl.BlockSpec((1, nv, tm), lambda i: (i // per_tile, 0, i % per_tile))],
        out_shape=[jax.ShapeDtypeStruct((T, nq), BF16),
                   jax.ShapeDtypeStruct((T, nq), BF16),
                   jax.ShapeDtypeStruct((T // kv_tile, nv, kv_tile), BF16)],
        scratch_shapes=[pltpu.VMEM((tm, q_lora), BF16),
                        pltpu.VMEM((tm, kv_lora), BF16)],
        compiler_params=pltpu.CompilerParams(
            dimension_semantics=("parallel",),
            vmem_limit_bytes=_vmem_limit(est)),
        name="mla_proj",
    )(proj, proj, proj, g_q, g_kv, wq, wk, wvt, rope_c, rope_n, rope_p)


def _mla_attn_kernel(q_ref, k_ref, vt_ref, o_ref, m_ref, l_ref, acc_ref, *, tile, heads):
    qi = pl.program_id(2)
    W, dv = MLA_QK_PAD, MLA_V
    qs = [q_ref[:, h * W:(h + 1) * W] for h in range(heads)]
    m_ref[...] = jnp.full(m_ref.shape, -jnp.inf, F32)
    l_ref[...] = jnp.zeros(l_ref.shape, F32)
    acc_ref[...] = jnp.zeros(acc_ref.shape, F32)

    def process(tiles):
        scores = []
        for j, _ in tiles:
            start = pl.multiple_of(j * tile, tile)
            scores.append([
                lax.dot_general(k_ref[pl.ds(start, tile), h * W:(h + 1) * W], qs[h],
                                (((1,), (1,)), ((), ())), preferred_element_type=F32)
                for h in range(heads)])
        for t, (j, mask) in enumerate(tiles):
            for h in range(heads):
                s = scores[t][h]
                if mask is not None:
                    s = jnp.where(mask, s, -jnp.inf)
                m = m_ref[h]
                m_new = jnp.maximum(m, jnp.max(s, axis=0, keepdims=True))
                alpha = jnp.exp2(m - m_new)
                p = jnp.exp2(s - m_new)
                vt = vt_ref[j, h * dv:(h + 1) * dv, :]
                l_ref[h] = alpha * l_ref[h] + jnp.sum(p, axis=0, keepdims=True)
                acc_ref[h] = alpha * acc_ref[h] + jnp.dot(vt, p.astype(vt.dtype),
                                                         preferred_element_type=F32)
                m_ref[h] = m_new

    key_chunk = lax.broadcasted_iota(jnp.int32, (tile, tile), 0) >> CHUNK_SHIFT
    qry_chunk = lax.broadcasted_iota(jnp.int32, (tile, tile), 1) >> CHUNK_SHIFT
    diag_mask = key_chunk <= qry_chunk

    @pl.when(qi == 0)
    def _():
        process([(qi, diag_mask)])

    @pl.when(qi > 0)
    def _():
        lead = (qi + 1) % 2

        @pl.when(lead == 1)
        def _():
            process([(0, None)])

        def pair(i, carry):
            j = lead + 2 * i
            process([(j, None), (j + 1, None)])
            return carry

        lax.fori_loop(0, (qi + 1 - lead) // 2 - 1, pair, 0)
        process([(qi - 1, None), (qi, diag_mask)])

    for h in range(heads):
        o_ref[:, h * dv:(h + 1) * dv] = (acc_ref[h] * (1.0 / l_ref[h])).T.astype(o_ref.dtype)


def mla_attention(qc, kc, vt, *, batch, seq, tile, heads):
    T = qc.shape[0]
    H, W, dv = MLA_HEADS, MLA_QK_PAD, MLA_V
    nq = seq // tile
    est = (2 * heads * (_nbytes((seq, W), BF16) + _nbytes((seq, dv), BF16)
                        + _nbytes((tile, W), BF16) + _nbytes((tile, dv), BF16))
           + 4 * heads * _nbytes((tile, tile), F32))
    return pl.pallas_call(
        functools.partial(_mla_attn_kernel, tile=tile, heads=heads),
        grid=(batch, H // heads, nq),
        in_specs=[pl.BlockSpec((tile, heads * W), lambda b, h, i: (b * nq + i, h)),
                  pl.BlockSpec((seq, heads * W), lambda b, h, i: (b, h)),
                  pl.BlockSpec((nq, heads * dv, tile), lambda b, h, i: (b, h, 0))],
        out_specs=pl.BlockSpec((tile, heads * dv), lambda b, h, i: (b * nq + i, h)),
        out_shape=jax.ShapeDtypeStruct((T, H * dv), BF16),
        scratch_shapes=[pltpu.VMEM((heads, 1, tile), F32),
                        pltpu.VMEM((heads, 1, tile), F32),
                        pltpu.VMEM((heads, dv, tile), F32)],
        compiler_params=pltpu.CompilerParams(
            dimension_semantics=("parallel", "parallel", "arbitrary"),
            vmem_limit_bytes=_vmem_limit(est)),
        name="mla_attn",
    )(qc, kc, vt)


def _emit_residual(h, g_ref, o_ref, hg_ref, ssq_ref):
    o_ref[...] = h
    hg_ref[...] = (h * g_ref[...]).astype(hg_ref.dtype)
    part = jnp.sum(h * h, axis=-1, keepdims=True)

    @pl.when(pl.program_id(1) == 0)
    def _():
        ssq_ref[...] = part

    @pl.when(pl.program_id(1) > 0)
    def _():
        ssq_ref[...] += part


def _residual_out(T, N, tm, tn):
    specs = [pl.BlockSpec((tm, tn), lambda i, j: (i, j)),
             pl.BlockSpec((tm, tn), lambda i, j: (i, j)),
             pl.BlockSpec((tm, 1), lambda i, j: (i, 0))]
    shapes = [jax.ShapeDtypeStruct((T, N), F32), jax.ShapeDtypeStruct((T, N), BF16),
              jax.ShapeDtypeStruct((T, 1), F32)]
    return specs, shapes


def _out_proj_kernel(ro_ref, mo_ref, wr_ref, wm_ref, x_ref, g_ref, o_ref, hg_ref, ssq_ref):
    acc = jnp.dot(ro_ref[...], wr_ref[...], preferred_element_type=F32)
    acc = acc + jnp.dot(mo_ref[...], wm_ref[...], preferred_element_type=F32)
    _emit_residual(x_ref[...] + acc, g_ref, o_ref, hg_ref, ssq_ref)


def out_projection(ro, mo, w_o, x, g_next, *, tm, tn):
    T, kr = ro.shape
    km = mo.shape[1]
    assert kr == km
    N = w_o.shape[1]
    est = (4 * _nbytes((tm, kr), BF16) + 4 * _nbytes((kr, tn), BF16)
           + 7 * _nbytes((tm, tn), F32))
    out_specs, out_shape = _residual_out(T, N, tm, tn)
    return pl.pallas_call(
        _out_proj_kernel,
        grid=(T // tm, N // tn),
        in_specs=[pl.BlockSpec((tm, kr), lambda i, j: (i, 0)),
                  pl.BlockSpec((tm, km), lambda i, j: (i, 0)),
                  pl.BlockSpec((kr, tn), lambda i, j: (0, j)),
                  pl.BlockSpec((km, tn), lambda i, j: (1, j)),
                  pl.BlockSpec((tm, tn), lambda i, j: (i, j)),
                  pl.BlockSpec((1, tn), lambda i, j: (0, j))],
        out_specs=out_specs,
        out_shape=out_shape,
        compiler_params=pltpu.CompilerParams(
            dimension_semantics=("parallel", "arbitrary"),
            vmem_limit_bytes=_vmem_limit(est)),
        name="out_proj",
    )(ro, mo, w_o, w_o, x, g_next)


def _ffn_up_kernel(hg_ref, ssq_ref, wg_ref, wu_ref, cw_ref, cb_ref, o_ref, g_ref,
                   *, tiles_per_seq):
    tm = hg_ref.shape[0]
    halo = CONV_HALO

    @pl.when(pl.program_id(1) % tiles_per_seq == 0)
    def _():
        g_ref[0:halo, :] = jnp.zeros((halo, g_ref.shape[1]), F32)

    hg = hg_ref[...]
    r = _row_scale(ssq_ref, hg_ref.shape[1])
    g_ref[halo:halo + tm, :] = jnp.dot(hg, wg_ref[...], preferred_element_type=F32) * r
    up = jnp.dot(hg, wu_ref[...], preferred_element_type=F32) * r
    a = cb_ref[...]
    for j in range(CONV_WIDTH):
        shift = CONV_WIDTH - 1 - j
        a = a + g_ref[halo - shift:halo - shift + tm, :] * cw_ref[j:j + 1, :]
    o_ref[...] = (_silu(a) * up).astype(o_ref.dtype)
    g_ref[0:halo, :] = g_ref[tm:tm + halo, :]


def ffn_up(hg, ssq, w_gate, w_up, conv_w, conv_b, *, seq, tm, tf, col_start, n_cols, name):
    T, K = hg.shape
    assert n_cols % tf == 0 and col_start % tf == 0
    first = col_start // tf
    est = (2 * _nbytes((tm, K), BF16) + 4 * _nbytes((K, tf), BF16)
           + 2 * _nbytes((tm, tf), BF16) + 4 * _nbytes((tm + CONV_HALO, tf), F32)
           + 2 * _nbytes((tm, LANES), F32))
    return pl.pallas_call(
        functools.partial(_ffn_up_kernel, tiles_per_seq=seq // tm),
        grid=(n_cols // tf, T // tm),
        in_specs=[pl.BlockSpec((tm, K), lambda j, i: (i, 0)),
                  pl.BlockSpec((tm, 1), lambda j, i: (i, 0)),
                  pl.BlockSpec((K, tf), lambda j, i: (0, first + j)),
                  pl.BlockSpec((K, tf), lambda j, i: (0, first + j)),
                  pl.BlockSpec((CONV_WIDTH, tf), lambda j, i: (0, first + j)),
                  pl.BlockSpec((1, tf), lambda j, i: (0, first + j))],
        out_specs=pl.BlockSpec((tm, tf), lambda j, i: (i, j)),
        out_shape=jax.ShapeDtypeStruct((T, n_cols), BF16),
        scratch_shapes=[pltpu.VMEM((tm + CONV_HALO, tf), F32)],
        compiler_params=pltpu.CompilerParams(
            dimension_semantics=("arbitrary", "arbitrary"),
            vmem_limit_bytes=_vmem_limit(est)),
        name=name,
    )(hg, ssq, w_gate, w_up, conv_w, conv_b)


def _ffn_down_kernel(a_ref, b_ref, wa_ref, wb_ref, h_ref, g_ref, o_ref, hg_ref, ssq_ref):
    acc = jnp.dot(a_ref[...], wa_ref[...], preferred_element_type=F32)
    acc = acc + jnp.dot(b_ref[...], wb_ref[...], preferred_element_type=F32)
    _emit_residual(h_ref[...] + acc, g_ref, o_ref, hg_ref, ssq_ref)


def ffn_down(hidden_main, hidden_tail, w_down, h, g_next, *, tm, tn):
    T, ka = hidden_main.shape
    kb = hidden_tail.shape[1]
    assert ka % kb == 0 and ka + kb == w_down.shape[0]
    N = w_down.shape[1]
    est = (2 * _nbytes((tm, ka + kb), BF16) + 2 * _nbytes((ka + kb, tn), BF16)
           + 7 * _nbytes((tm, tn), F32))
    out_specs, out_shape = _residual_out(T, N, tm, tn)
    return pl.pallas_call(
        _ffn_down_kernel,
        grid=(T // tm, N // tn),
        in_specs=[pl.BlockSpec((tm, ka), lambda i, j: (i, 0)),
                  pl.BlockSpec((tm, kb), lambda i, j: (i, 0)),
                  pl.BlockSpec((ka, tn), lambda i, j: (0, j)),
                  pl.BlockSpec((kb, tn), lambda i, j: (ka // kb, j)),
                  pl.BlockSpec((tm, tn), lambda i, j: (i, j)),
                  pl.BlockSpec((1, tn), lambda i, j: (0, j))],
        out_specs=out_specs,
        out_shape=out_shape,
        compiler_params=pltpu.CompilerParams(
            dimension_semantics=("parallel", "arbitrary"),
            vmem_limit_bytes=_vmem_limit(est)),
        name="ffn_down",
    )(hidden_main, hidden_tail, w_down, w_down, h, g_next)


def _ple_kernel(hg_ref, ssq_ref, h_ref, wg_ref, p_ref, wp_ref, gf_ref, o_ref, ssq3_ref):
    j = pl.program_id(1)
    tm, D = hg_ref.shape
    tn = wg_ref.shape[1]

    z = jnp.dot(hg_ref[...], wg_ref[...], preferred_element_type=F32) * _row_scale(ssq_ref, D)
    gate = 1.0 / (1.0 + jnp.exp(-z))
    emb = jnp.dot(p_ref[...].astype(BF16), wp_ref[...], preferred_element_type=F32)
    h3 = h_ref[...] + gate * emb
    part = jnp.sum(h3 * h3, axis=-1, keepdims=True)

    @pl.when(j == 0)
    def _():
        ssq3_ref[...] = part

    @pl.when(j > 0)
    def _():
        ssq3_ref[...] += part

    for jj in range(D // tn):
        @pl.when(j == jj)
        def _(jj=jj):
            o_ref[:, jj * tn:(jj + 1) * tn] = h3

    @pl.when(j == pl.num_programs(1) - 1)
    def _():
        o_ref[...] = (o_ref[...] * _row_scale(ssq3_ref, D)) * gf_ref[...]


def ple_final(hg, ssq, h, w_gate, p, w_proj, g_final, *, tm, tn):
    T, D = h.shape
    P = p.shape[1]
    est = (2 * _nbytes((tm, D), F32) + 2 * _nbytes((tm, D), BF16)
           + 2 * _nbytes((D, tn), BF16) + 2 * _nbytes((P, tn), BF16)
           + 2 * _nbytes((tm, P), F32) + 6 * _nbytes((tm, tn), F32))
    return pl.pallas_call(
        _ple_kernel,
        grid=(T // tm, D // tn),
        in_specs=[pl.BlockSpec((tm, D), lambda i, j: (i, 0)),
                  pl.BlockSpec((tm, 1), lambda i, j: (i, 0)),
                  pl.BlockSpec((tm, tn), lambda i, j: (i, j)),
                  pl.BlockSpec((D, tn), lambda i, j: (0, j)),
                  pl.BlockSpec((tm, P), lambda i, j: (i, 0)),
                  pl.BlockSpec((P, tn), lambda i, j: (0, j)),
                  pl.BlockSpec((1, D), lambda i, j: (0, 0))],
        out_specs=pl.BlockSpec((tm, D), lambda i, j: (i, 0)),
        out_shape=jax.ShapeDtypeStruct((T, D), F32),
        scratch_shapes=[pltpu.VMEM((tm, 1), F32)],
        compiler_params=pltpu.CompilerParams(
            dimension_semantics=("parallel", "arbitrary"),
            vmem_limit_bytes=_vmem_limit(est)),
        name="ple_final",
    )(hg, ssq, h, w_gate, p, w_proj, g_final)


def _rope_tables(seq, dim):
    inv = 1.0 / (ROPE_BASE ** (jnp.arange(0, dim, 2, dtype=F32) / dim))
    ang = jnp.arange(seq, dtype=F32)[:, None] * inv[None, :]
    return jnp.cos(ang), jnp.sin(ang)


def _tile_config(seq):
    return dict(
        in_proj=dict(tm=min(512, seq), tn=768),
        retention_rows=min(1024, seq),
        mla_proj_tm=min(256, seq),
        attn_tile=min(512, seq),
        attn_heads=4,
        out_proj=dict(tm=min(1024, seq), tn=512),
        ffn_up=dict(tm=min(1024, seq), tf=512),
        ffn_down=dict(tm=min(512, seq), tn=512),
        ple=dict(tm=min(512, seq), tn=512),
    )


def _layer(h, p_i, w_in, g_attn, g_q_lora, g_kv_lora, w_uq, w_ukv, w_o, g_ffn,
           w_ffn_gate, w_ffn_up, conv_w, conv_b, w_ffn_down, g_ple, w_ple_gate,
           w_ple_proj, g_out, *, batch, seq):
    cfg = _tile_config(seq)
    D = h.shape[1]
    ret_w = RET_HEADS * RET_HEAD_DIM
    q_lora = w_uq.shape[0]
    kv_lora = w_ukv.shape[0]
    d_ff = w_ffn_gate.shape[1]

    in_w = w_in.shape[1]
    tn_in = cfg["in_proj"]["tn"]
    in_pad = pl.cdiv(in_w, tn_in) * tn_in
    proj = norm_matmul(h, g_attn.reshape(1, D), w_in.astype(BF16), n_out=in_pad,
                       out_dtype=BF16, **cfg["in_proj"])

    log_g = jnp.log1p(-jnp.exp2(-5.0 - jnp.arange(RET_HEADS, dtype=F32)))
    cos_r, sin_r = _rope_tables(seq, RET_HEAD_DIM)
    ro = retention_group(proj, log_g, cos_r, sin_r, batch=batch, seq=seq,
                         rows=cfg["retention_rows"])

    hq = MLA_NOPE + MLA_ROPE
    wq = w_uq.reshape(q_lora, MLA_HEADS, hq)
    wq = jnp.pad(wq, ((0, 0), (0, 0), (0, MLA_QK_PAD - hq)))
    wq = wq.reshape(q_lora, MLA_HEADS * MLA_QK_PAD).astype(BF16)
    wkv = w_ukv.reshape(kv_lora, MLA_HEADS, MLA_NOPE + MLA_V)
    wk = wkv[:, :, :MLA_NOPE].reshape(kv_lora, MLA_HEADS * MLA_NOPE).astype(BF16)
    wvt = wkv[:, :, MLA_NOPE:].reshape(kv_lora, MLA_HEADS * MLA_V).T.astype(BF16)
    cos_m, sin_m = _rope_tables(seq, MLA_ROPE)
    zeros = jnp.zeros_like(cos_m)
    fill = jnp.zeros((seq, LANES - MLA_ROPE), F32)
    rope_c = jnp.concatenate([cos_m, cos_m, fill], axis=1)
    rope_n = jnp.concatenate([-sin_m, zeros, fill], axis=1)
    rope_p = jnp.concatenate([zeros, sin_m, fill], axis=1)
    cq_off = 4 * ret_w
    ckv_off = cq_off + q_lora
    kr_off = ckv_off + kv_lora
    assert cq_off % q_lora == 0 and ckv_off % kv_lora == 0 and kr_off % (2 * LANES) == 0
    assert in_pad - kr_off >= 2 * LANES
    qc, kc, vt = mla_projections(
        proj, g_q_lora.reshape(1, q_lora), g_kv_lora.reshape(1, kv_lora), wq, wk, wvt,
        rope_c, rope_n, rope_p, seq=seq, tm=cfg["mla_proj_tm"], kv_tile=cfg["attn_tile"],
        cq_blk=cq_off // q_lora, ckv_blk=ckv_off // kv_lora, kr_blk=kr_off // (2 * LANES))
    mo = mla_attention(qc, kc, vt, batch=batch, seq=seq, tile=cfg["attn_tile"],
                       heads=cfg["attn_heads"])

    h1, hg1, ssq1 = out_projection(ro, mo, w_o.astype(BF16), h, g_ffn.reshape(1, D),
                                   **cfg["out_proj"])

    up_args = (hg1, ssq1, w_ffn_gate.astype(BF16), w_ffn_up.astype(BF16), conv_w,
               conv_b.reshape(1, d_ff))
    tf = cfg["ffn_up"]["tf"]
    n_main = d_ff // tf * tf
    hidden_main = ffn_up(*up_args, seq=seq, tm=cfg["ffn_up"]["tm"], tf=tf, col_start=0,
                         n_cols=n_main, name="ffn_up")
    hidden_tail = ffn_up(*up_args, seq=seq, tm=cfg["ffn_up"]["tm"], tf=d_ff - n_main,
                         col_start=n_main, n_cols=d_ff - n_main, name="ffn_up_tail")
    h2, hg2, ssq2 = ffn_down(hidden_main, hidden_tail, w_ffn_down.astype(BF16), h1,
                             g_ple.reshape(1, D), **cfg["ffn_down"])

    return ple_final(hg2, ssq2, h2, w_ple_gate.astype(BF16), p_i, w_ple_proj.astype(BF16),
                     g_out.reshape(1, D), **cfg["ple"])


def kernel(x, p, w_in, g_attn, g_q_lora, g_kv_lora, w_uq, w_ukv, w_o, g_ffn, w_ffn_gate,
           w_ffn_up, conv_w, conv_b, w_ffn_down, g_ple, w_ple_gate, w_ple_proj, g_final):
    B, S, D = x.shape
    depth = p.shape[0]
    assert depth == 1, "the final RMSNorm is fused into the layer's last kernel"
    h = x.reshape(B * S, D)
    out = _layer(h, p[0].reshape(B * S, -1), w_in[0], g_attn[0], g_q_lora[0], g_kv_lora[0],
                 w_uq[0], w_ukv[0], w_o[0], g_ffn[0], w_ffn_gate[0], w_ffn_up[0], conv_w[0],
                 conv_b[0], w_ffn_down[0], g_ple[0], w_ple_gate[0], w_ple_proj[0], g_final,
                 batch=B, seq=S)
    return out.reshape(B, S, D)
```

```python
import functools
import math

import jax
import jax.numpy as jnp
from jax import lax
from jax.experimental import pallas as pl
from jax.experimental.pallas import tpu as pltpu

F32 = jnp.float32
BF16 = jnp.bfloat16

EPS = 1e-6
ROPE_BASE = 10000.0
CHUNK = 64
CHUNK_SHIFT = 6
RET_HEADS = 8
RET_HEAD_DIM = 256
MLA_HEADS = 16
MLA_NOPE = 128
MLA_ROPE = 64
MLA_V = 128
MLA_QK_PAD = 256
CONV_WIDTH = 3
RET_BLOCK = 256
LANES = 128
CONV_HALO = 8
MIB = 1024 * 1024
VMEM_LIMIT_CAP = 60 * MIB


def _vmem_limit(nbytes):
    return int(min(VMEM_LIMIT_CAP, nbytes + 16 * MIB))


def _nbytes(shape, dtype):
    return math.prod(shape) * jnp.dtype(dtype).itemsize


def _rms_rows(x_ref, g_ref, o_ref, n_rows, row_chunk=16, unroll=4):
    g = g_ref[...]

    def body(c, carry):
        r = pl.multiple_of(c * row_chunk, row_chunk)
        x = x_ref[pl.ds(r, row_chunk), :].astype(F32)
        ms = jnp.mean(x * x, axis=-1, keepdims=True)
        o_ref[pl.ds(r, row_chunk), :] = ((x * lax.rsqrt(ms + EPS)) * g).astype(o_ref.dtype)
        return carry

    lax.fori_loop(0, n_rows // row_chunk, body, 0, unroll=unroll)


def _row_scale(ssq_ref, width):
    return lax.rsqrt(ssq_ref[...] * (1.0 / width) + EPS)


def _silu(x):
    return x * (1.0 / (1.0 + jnp.exp(-x)))


def _prenorm_kernel(x_ref, g_ref, xg_ref, ssq_ref, *, row_chunk):
    g = g_ref[...]

    def body(c, carry):
        rows = pl.ds(pl.multiple_of(c * row_chunk, row_chunk), row_chunk)
        x = x_ref[rows, :]
        ssq_ref[rows, :] = jnp.sum(x * x, axis=-1, keepdims=True)
        xg_ref[rows, :] = (x * g).astype(xg_ref.dtype)
        return carry

    lax.fori_loop(0, x_ref.shape[0] // row_chunk, body, 0, unroll=4)


def prenorm(x, g, *, tm):
    T, K = x.shape
    est = 2 * _nbytes((tm, K), F32) + 2 * _nbytes((tm, K), BF16)
    return pl.pallas_call(
        functools.partial(_prenorm_kernel, row_chunk=16),
        grid=(T // tm,),
        in_specs=[pl.BlockSpec((tm, K), lambda i: (i, 0)),
                  pl.BlockSpec((1, K), lambda i: (0, 0))],
        out_specs=[pl.BlockSpec((tm, K), lambda i: (i, 0)),
                   pl.BlockSpec((tm, 1), lambda i: (i, 0))],
        out_shape=[jax.ShapeDtypeStruct((T, K), BF16), jax.ShapeDtypeStruct((T, 1), F32)],
        compiler_params=pltpu.CompilerParams(
            dimension_semantics=("parallel",),
            vmem_limit_bytes=_vmem_limit(est)),
        name="prenorm",
    )(x, g)


def _in_proj_kernel(xg_ref, ssq_ref, w_ref, o_ref, wb_ref, *, edge_cols):
    K, tn = w_ref.shape

    @pl.when(pl.program_id(1) == 0)
    def _():
        w = w_ref[...]
        if edge_cols:
            last = pl.program_id(0) == pl.num_programs(0) - 1
            valid = jnp.where(last, edge_cols, tn)
            col = lax.broadcasted_iota(jnp.int32, (1, tn), 1)
            w = jnp.where(col < valid, w, 0.0)
        wb_ref[...] = w.astype(wb_ref.dtype)

    res = jnp.dot(xg_ref[...], wb_ref[...], preferred_element_type=F32)
    o_ref[...] = (res * _row_scale(ssq_ref, K)).astype(o_ref.dtype)


def in_projection(xg, ssq, w, *, n_out, tm, tn, out_dtype):
    T, K = xg.shape
    N = n_out
    assert N % tn == 0 and N - tn < w.shape[1] <= N
    est = (2 * _nbytes((tm, K), BF16) + 2 * _nbytes((K, tn), w.dtype)
           + _nbytes((K, tn), BF16) + 2 * _nbytes((tm, tn), out_dtype)
           + _nbytes((tm, tn), F32))
    return pl.pallas_call(
        functools.partial(_in_proj_kernel, edge_cols=w.shape[1] % tn),
        grid=(N // tn, T // tm),
        in_specs=[pl.BlockSpec((tm, K), lambda j, i: (i, 0)),
                  pl.BlockSpec((tm, 1), lambda j, i: (i, 0)),
                  pl.BlockSpec((K, tn), lambda j, i: (0, j))],
        out_specs=pl.BlockSpec((tm, tn), lambda j, i: (i, j)),
        out_shape=jax.ShapeDtypeStruct((T, N), out_dtype),
        scratch_shapes=[pltpu.VMEM((K, tn), BF16)],
        compiler_params=pltpu.CompilerParams(
            dimension_semantics=("arbitrary", "arbitrary"),
            vmem_limit_bytes=_vmem_limit(est)),
        name="in_proj",
    )(xg, ssq, w)


def _retention_kernel(lg_ref, q_ref, k_ref, v_ref, gate_ref, cos_ref, sin_ref, o_ref,
                      state_ref, dmat_ref, qdec_ref, kdec_ref, sdec_ref, *, k_scale):
    L = RET_BLOCK
    dk = q_ref.shape[1]
    half = dk // 2
    lg = lg_ref[pl.program_id(1)]

    @pl.when(pl.program_id(2) == 0)
    def _init():
        state_ref[...] = jnp.zeros_like(state_ref)
        n = lax.broadcasted_iota(jnp.int32, (L, L), 0)
        m = lax.broadcasted_iota(jnp.int32, (L, L), 1)
        cn = n >> CHUNK_SHIFT
        cm = m >> CHUNK_SHIFT
        d = (n - m).astype(F32)
        expo = jnp.where(cn == cm, jnp.abs(d), d)
        visible = cm <= cn
        dmat_ref[...] = jnp.where(visible, jnp.exp(lg * jnp.where(visible, expo, 0.0)), 0.0)
        row = lax.broadcasted_iota(jnp.int32, (L, dk), 0).astype(F32)
        qdec_ref[...] = jnp.exp(lg * (row + 1.0))
        kdec_ref[...] = jnp.exp(lg * (float(L - 1) - row))
        sdec_ref[...] = jnp.exp(jnp.full(sdec_ref.shape, lg * float(L), F32))

    def rope(x, cos, sin):
        x1 = x[:, :half]
        x2 = x[:, half:]
        return jnp.concatenate([x1 * cos - x2 * sin, x2 * cos + x1 * sin], axis=1)

    for sub in range(q_ref.shape[0] // L):
        rows = pl.ds(sub * L, L)
        cos = cos_ref[rows, :]
        sin = sin_ref[rows, :]
        q = rope(q_ref[rows, :].astype(F32), cos, sin)
        k = rope(k_ref[rows, :].astype(F32), cos, sin) * k_scale
        v = v_ref[rows, :]
        qb = q.astype(BF16)
        kb = k.astype(BF16)
        scores = lax.dot_general(qb, kb, (((1,), (1,)), ((), ())),
                                 preferred_element_type=F32)
        state = state_ref[...]
        inter = jnp.dot((q * qdec_ref[...]).astype(BF16), state.astype(BF16),
                        preferred_element_type=F32)
        kd = (k * kdec_ref[...]).astype(BF16)
        state_ref[...] = state * sdec_ref[0:1, :] + lax.dot_general(
            kd, v, (((0,), (0,)), ((), ())), preferred_element_type=F32)
        scores = (scores * dmat_ref[...]).astype(BF16)
        out = jnp.dot(scores, v, preferred_element_type=F32) + inter
        mu = jnp.mean(out, axis=-1, keepdims=True)
        cen = out - mu
        var = jnp.mean(cen * cen, axis=-1, keepdims=True)
        gate = gate_ref[rows, :].astype(F32)
        o_ref[rows, :] = (_silu(gate) * (cen * lax.rsqrt(var + EPS))).astype(o_ref.dtype)


def retention_group(proj, log_g, cos, sin, *, batch, seq, rows):
    T = proj.shape[0]
    H, dk = RET_HEADS, RET_HEAD_DIM
    nblk = seq // rows

    def col(offset):
        return pl.BlockSpec((rows, dk), lambda b, h, r: (b * nblk + r, offset + h))

    tab = pl.BlockSpec((rows, dk // 2), lambda b, h, r: (r, 0))
    est = 10 * _nbytes((rows, dk), BF16) + 4 * _nbytes((rows, dk // 2), F32) \
        + 4 * _nbytes((RET_BLOCK, dk), F32)
    return pl.pallas_call(
        functools.partial(_retention_kernel, k_scale=dk ** -0.5),
        grid=(batch, H, nblk),
        in_specs=[pl.BlockSpec(memory_space=pltpu.SMEM),
                  col(0), col(H), col(2 * H), col(3 * H), tab, tab],
        out_specs=pl.BlockSpec((rows, dk), lambda b, h, r: (b * nblk + r, h)),
        out_shape=jax.ShapeDtypeStruct((T, H * dk), BF16),
        scratch_shapes=[pltpu.VMEM((dk, dk), F32),
                        pltpu.VMEM((RET_BLOCK, RET_BLOCK), F32),
                        pltpu.VMEM((RET_BLOCK, dk), F32),
                        pltpu.VMEM((RET_BLOCK, dk), F32),
                        pltpu.VMEM((8, dk), F32)],
        compiler_params=pltpu.CompilerParams(
            dimension_semantics=("parallel", "parallel", "arbitrary"),
            vmem_limit_bytes=_vmem_limit(est)),
        name="retention",
    )(log_g, proj, proj, proj, proj, cos, sin)


def _mla_proj_kernel(cq_ref, ckv_ref, kr_ref, gq_ref, gkv_ref, wq_ref, wk_ref, wvt_ref,
                     rc_ref, rn_ref, rp_ref, q_ref, k_ref, vt_ref, cqn_ref, ckvn_ref,
                     *, scale):
    tm = cq_ref.shape[0]
    _rms_rows(cq_ref, gq_ref, cqn_ref, tm)
    _rms_rows(ckv_ref, gkv_ref, ckvn_ref, tm)
    rc = rc_ref[...]
    rn = rn_ref[...]
    rp = rp_ref[...]

    def rope(x):
        return (x * rc + pltpu.roll(x, LANES - MLA_ROPE // 2, 1) * rn
                + pltpu.roll(x, MLA_ROPE // 2, 1) * rp)

    krp = rope(kr_ref[:, :LANES].astype(F32)).astype(k_ref.dtype)
    cqn = cqn_ref[...]
    ckvn = ckvn_ref[...]
    W = MLA_QK_PAD
    for h in range(MLA_HEADS):
        qh = jnp.dot(cqn, wq_ref[:, h * W:(h + 1) * W], preferred_element_type=F32) * scale
        q_ref[:, h * W:h * W + LANES] = qh[:, :LANES].astype(q_ref.dtype)
        q_ref[:, h * W + LANES:(h + 1) * W] = rope(qh[:, LANES:]).astype(q_ref.dtype)
    for c in range(MLA_HEADS // 2):
        kn = jnp.dot(ckvn, wk_ref[:, c * W:(c + 1) * W], preferred_element_type=F32)
        for s in range(2):
            h = 2 * c + s
            k_ref[:, h * W:h * W + LANES] = kn[:, s * LANES:(s + 1) * LANES].astype(k_ref.dtype)
            k_ref[:, h * W + LANES:(h + 1) * W] = krp
    vt_ref[0] = lax.dot_general(wvt_ref[...], ckvn, (((1,), (1,)), ((), ())),
                                preferred_element_type=F32).astype(vt_ref.dtype)


def mla_projections(proj, g_q, g_kv, wq, wk, wvt, rope_c, rope_n, rope_p, *, seq, tm,
                    kv_tile, cq_blk, ckv_blk, kr_blk):
    T = proj.shape[0]
    q_lora = wq.shape[0]
    kv_lora = wk.shape[0]
    nq = wq.shape[1]
    nv = wvt.shape[0]
    nblk = seq // tm
    per_tile = kv_tile // tm
    scale = (MLA_NOPE + MLA_ROPE) ** -0.5 * math.log2(math.e)
    const = lambda i: (0, 0)
    tab = pl.BlockSpec((tm, LANES), lambda i: (i % nblk, 0))
    est = (2 * (_nbytes(wq.shape, BF16) + _nbytes(wk.shape, BF16) + _nbytes(wvt.shape, BF16))
           + 2 * (2 * _nbytes((tm, nq), BF16) + _nbytes((tm, nv), BF16))
           + 3 * _nbytes((tm, q_lora + kv_lora + 2 * LANES), BF16)
           + 6 * _nbytes((tm, LANES), F32) + _nbytes((tm, nv), F32))
    return pl.pallas_call(
        functools.partial(_mla_proj_kernel, scale=scale),
        grid=(T // tm,),
        in_specs=[pl.BlockSpec((tm, q_lora), lambda i: (i, cq_blk)),
                  pl.BlockSpec((tm, kv_lora), lambda i: (i, ckv_blk)),
                  pl.BlockSpec((tm, 2 * LANES), lambda i: (i, kr_blk)),
                  pl.BlockSpec((1, q_lora), const),
                  pl.BlockSpec((1, kv_lora), const),
                  pl.BlockSpec(wq.shape, const),
                  pl.BlockSpec(wk.shape, const),
                  pl.BlockSpec(wvt.shape, const),
                  tab, tab, tab],
        out_specs=[pl.BlockSpec((tm, nq), lambda i: (i, 0)),
                   pl.BlockSpec((tm, nq), lambda i: (i, 0)),
                   pl.BlockSpec((1, nv, tm), lambda i: (i // per_tile, 0, i % per_tile))],
        out_shape=[jax.ShapeDtypeStruct((T, nq), BF16),
                   jax.ShapeDtypeStruct((T, nq), BF16),
                   jax.ShapeDtypeStruct((T // kv_tile, nv, kv_tile), BF16)],
        scratch_shapes=[pltpu.VMEM((tm, q_lora), BF16),
                        pltpu.VMEM((tm, kv_lora), BF16)],
        compiler_params=pltpu.CompilerParams(
            dimension_semantics=("parallel",),
            vmem_limit_bytes=_vmem_limit(est)),
        name="mla_proj",
    )(proj, proj, proj, g_q, g_kv, wq, wk, wvt, rope_c, rope_n, rope_p)


def _mla_attn_kernel(q_ref, k_ref, vt_ref, o_ref, m_ref, l_ref, acc_ref, *, tile, heads):
    qi = pl.program_id(2)
    W, dv = MLA_QK_PAD, MLA_V
    qs = [q_ref[:, h * W:(h + 1) * W] for h in range(heads)]
    m_ref[...] = jnp.full(m_ref.shape, -jnp.inf, F32)
    l_ref[...] = jnp.zeros(l_ref.shape, F32)
    acc_ref[...] = jnp.zeros(acc_ref.shape, F32)

    def process(tiles):
        scores = []
        for j, _ in tiles:
            start = pl.multiple_of(j * tile, tile)
            scores.append([
                lax.dot_general(k_ref[pl.ds(start, tile), h * W:(h + 1) * W], qs[h],
                                (((1,), (1,)), ((), ())), preferred_element_type=F32)
                for h in range(heads)])
        for t, (j, mask) in enumerate(tiles):
            for h in range(heads):
                s = scores[t][h]
                if mask is not None:
                    s = jnp.where(mask, s, -jnp.inf)
                m = m_ref[h]
                m_new = jnp.maximum(m, jnp.max(s, axis=0, keepdims=True))
                alpha = jnp.exp2(m - m_new)
                p = jnp.exp2(s - m_new)
                vt = vt_ref[j, h * dv:(h + 1) * dv, :]
                l_ref[h] = alpha * l_ref[h] + jnp.sum(p, axis=0, keepdims=True)
                acc_ref[h] = alpha * acc_ref[h] + jnp.dot(vt, p.astype(vt.dtype),
                                                         preferred_element_type=F32)
                m_ref[h] = m_new

    key_chunk = lax.broadcasted_iota(jnp.int32, (tile, tile), 0) >> CHUNK_SHIFT
    qry_chunk = lax.broadcasted_iota(jnp.int32, (tile, tile), 1) >> CHUNK_SHIFT
    diag_mask = key_chunk <= qry_chunk

    @pl.when(qi == 0)
    def _():
        process([(qi, diag_mask)])

    @pl.when(qi > 0)
    def _():
        lead = (qi + 1) % 2

        @pl.when(lead == 1)
        def _():
            process([(0, None)])

        def pair(i, carry):
            j = lead + 2 * i
            process([(j, None), (j + 1, None)])
            return carry

        lax.fori_loop(0, (qi + 1 - lead) // 2 - 1, pair, 0)
        process([(qi - 1, None), (qi, diag_mask)])

    for h in range(heads):
        o_ref[:, h * dv:(h + 1) * dv] = (acc_ref[h] * (1.0 / l_ref[h])).T.astype(o_ref.dtype)


def mla_attention(qc, kc, vt, *, batch, seq, tile, heads):
    T = qc.shape[0]
    H, W, dv = MLA_HEADS, MLA_QK_PAD, MLA_V
    nq = seq // tile
    est = (2 * heads * (_nbytes((seq, W), BF16) + _nbytes((seq, dv), BF16)
                        + _nbytes((tile, W), BF16) + _nbytes((tile, dv), BF16))
           + 4 * heads * _nbytes((tile, tile), F32))
    return pl.pallas_call(
        functools.partial(_mla_attn_kernel, tile=tile, heads=heads),
        grid=(batch, H // heads, nq),
        in_specs=[pl.BlockSpec((tile, heads * W), lambda b, h, i: (b * nq + i, h)),
                  pl.BlockSpec((seq, heads * W), lambda b, h, i: (b, h)),
                  pl.BlockSpec((nq, heads * dv, tile), lambda b, h, i: (b, h, 0))],
        out_specs=pl.BlockSpec((tile, heads * dv), lambda b, h, i: (b * nq + i, h)),
        out_shape=jax.ShapeDtypeStruct((T, H * dv), BF16),
        scratch_shapes=[pltpu.VMEM((heads, 1, tile), F32),
                        pltpu.VMEM((heads, 1, tile), F32),
                        pltpu.VMEM((heads, dv, tile), F32)],
        compiler_params=pltpu.CompilerParams(
            dimension_semantics=("parallel", "parallel", "arbitrary"),
            vmem_limit_bytes=_vmem_limit(est)),
        name="mla_attn",
    )(qc, kc, vt)


def _emit_residual(h, g_ref, o_ref, hg_ref, ssq_ref):
    o_ref[...] = h
    hg_ref[...] = (h * g_ref[...]).astype(hg_ref.dtype)
    part = jnp.sum(h * h, axis=-1, keepdims=True)

    @pl.when(pl.program_id(1) == 0)
    def _():
        ssq_ref[...] = part

    @pl.when(pl.program_id(1) > 0)
    def _():
        ssq_ref[...] += part


def _residual_out(T, N, tm, tn):
    specs = [pl.BlockSpec((tm, tn), lambda i, j: (i, j)),
             pl.BlockSpec((tm, tn), lambda i, j: (i, j)),
             pl.BlockSpec((tm, 1), lambda i, j: (i, 0))]
    shapes = [jax.ShapeDtypeStruct((T, N), F32), jax.ShapeDtypeStruct((T, N), BF16),
              jax.ShapeDtypeStruct((T, 1), F32)]
    return specs, shapes


def _out_proj_kernel(ro_ref, mo_ref, wr_ref, wm_ref, x_ref, g_ref, o_ref, hg_ref, ssq_ref):
    acc = jnp.dot(ro_ref[...], wr_ref[...], preferred_element_type=F32)
    acc = acc + jnp.dot(mo_ref[...], wm_ref[...], preferred_element_type=F32)
    _emit_residual(x_ref[...] + acc, g_ref, o_ref, hg_ref, ssq_ref)


def out_projection(ro, mo, w_o, x, g_next, *, tm, tn):
    T, kr = ro.shape
    km = mo.shape[1]
    assert kr == km
    N = w_o.shape[1]
    est = (4 * _nbytes((tm, kr), BF16) + 4 * _nbytes((kr, tn), BF16)
           + 7 * _nbytes((tm, tn), F32))
    out_specs, out_shape = _residual_out(T, N, tm, tn)
    return pl.pallas_call(
        _out_proj_kernel,
        grid=(T // tm, N // tn),
        in_specs=[pl.BlockSpec((tm, kr), lambda i, j: (i, 0)),
                  pl.BlockSpec((tm, km), lambda i, j: (i, 0)),
                  pl.BlockSpec((kr, tn), lambda i, j: (0, j)),
                  pl.BlockSpec((km, tn), lambda i, j: (1, j)),
                  pl.BlockSpec((tm, tn), lambda i, j: (i, j)),
                  pl.BlockSpec((1, tn), lambda i, j: (0, j))],
        out_specs=out_specs,
        out_shape=out_shape,
        compiler_params=pltpu.CompilerParams(
            dimension_semantics=("parallel", "arbitrary"),
            vmem_limit_bytes=_vmem_limit(est)),
        name="out_proj",
    )(ro, mo, w_o, w_o, x, g_next)


def _ffn_up_kernel(hg_ref, ssq_ref, wg_ref, wu_ref, cw_ref, cb_ref, o_ref, g_ref,
                   *, tiles_per_seq):
    tm = hg_ref.shape[0]
    halo = CONV_HALO

    @pl.when(pl.program_id(1) % tiles_per_seq == 0)
    def _():
        g_ref[0:halo, :] = jnp.zeros((halo, g_ref.shape[1]), F32)

    hg = hg_ref[...]
    r = _row_scale(ssq_ref, hg_ref.shape[1])
    g_ref[halo:halo + tm, :] = jnp.dot(hg, wg_ref[...], preferred_element_type=F32) * r
    up = jnp.dot(hg, wu_ref[...], preferred_element_type=F32) * r
    a = cb_ref[...]
    for j in range(CONV_WIDTH):
        shift = CONV_WIDTH - 1 - j
        a = a + g_ref[halo - shift:halo - shift + tm, :] * cw_ref[j:j + 1, :]
    o_ref[...] = (_silu(a) * up).astype(o_ref.dtype)
    g_ref[0:halo, :] = g_ref[tm:tm + halo, :]


def ffn_up(hg, ssq, w_gate, w_up, conv_w, conv_b, *, seq, tm, tf, col_start, n_cols, name):
    T, K = hg.shape
    assert n_cols % tf == 0 and col_start % tf == 0
    first = col_start // tf
    est = (2 * _nbytes((tm, K), BF16) + 4 * _nbytes((K, tf), BF16)
           + 2 * _nbytes((tm, tf), BF16) + 4 * _nbytes((tm + CONV_HALO, tf), F32)
           + 2 * _nbytes((tm, LANES), F32))
    return pl.pallas_call(
        functools.partial(_ffn_up_kernel, tiles_per_seq=seq // tm),
        grid=(n_cols // tf, T // tm),
        in_specs=[pl.BlockSpec((tm, K), lambda j, i: (i, 0)),
                  pl.BlockSpec((tm, 1), lambda j, i: (i, 0)),
                  pl.BlockSpec((K, tf), lambda j, i: (0, first + j)),
                  pl.BlockSpec((K, tf), lambda j, i: (0, first + j)),
                  pl.BlockSpec((CONV_WIDTH, tf), lambda j, i: (0, first + j)),
                  pl.BlockSpec((1, tf), lambda j, i: (0, first + j))],
        out_specs=pl.BlockSpec((tm, tf), lambda j, i: (i, j)),
        out_shape=jax.ShapeDtypeStruct((T, n_cols), BF16),
        scratch_shapes=[pltpu.VMEM((tm + CONV_HALO, tf), F32)],
        compiler_params=pltpu.CompilerParams(
            dimension_semantics=("arbitrary", "arbitrary"),
            vmem_limit_bytes=_vmem_limit(est)),
        name=name,
    )(hg, ssq, w_gate, w_up, conv_w, conv_b)


def _ffn_down_kernel(a_ref, b_ref, wa_ref, wb_ref, h_ref, g_ref, o_ref, hg_ref, ssq_ref):
    acc = jnp.dot(a_ref[...], wa_ref[...], preferred_element_type=F32)
    acc = acc + jnp.dot(b_ref[...], wb_ref[...], preferred_element_type=F32)
    _emit_residual(h_ref[...] + acc, g_ref, o_ref, hg_ref, ssq_ref)


def ffn_down(hidden_main, hidden_tail, w_down, h, g_next, *, tm, tn):
    T, ka = hidden_main.shape
    kb = hidden_tail.shape[1]
    assert ka % kb == 0 and ka + kb == w_down.shape[0]
    N = w_down.shape[1]
    est = (2 * _nbytes((tm, ka + kb), BF16) + 2 * _nbytes((ka + kb, tn), BF16)
           + 7 * _nbytes((tm, tn), F32))
    out_specs, out_shape = _residual_out(T, N, tm, tn)
    return pl.pallas_call(
        _ffn_down_kernel,
        grid=(T // tm, N // tn),
        in_specs=[pl.BlockSpec((tm, ka), lambda i, j: (i, 0)),
                  pl.BlockSpec((tm, kb), lambda i, j: (i, 0)),
                  pl.BlockSpec((ka, tn), lambda i, j: (0, j)),
                  pl.BlockSpec((kb, tn), lambda i, j: (ka // kb, j)),
                  pl.BlockSpec((tm, tn), lambda i, j: (i, j)),
                  pl.BlockSpec((1, tn), lambda i, j: (0, j))],
        out_specs=out_specs,
        out_shape=out_shape,
        compiler_params=pltpu.CompilerParams(
            dimension_semantics=("parallel", "arbitrary"),
            vmem_limit_bytes=_vmem_limit(est)),
        name="ffn_down",
    )(hidden_main, hidden_tail, w_down, w_down, h, g_next)


def _ple_kernel(hg_ref, ssq_ref, h_ref, wg_ref, p_ref, wp_ref, gf_ref, o_ref, ssq3_ref):
    j = pl.program_id(1)
    tm, D = hg_ref.shape
    tn = wg_ref.shape[1]

    z = jnp.dot(hg_ref[...], wg_ref[...], preferred_element_type=F32) * _row_scale(ssq_ref, D)
    gate = 1.0 / (1.0 + jnp.exp(-z))
    emb = jnp.dot(p_ref[...].astype(BF16), wp_ref[...], preferred_element_type=F32)
    h3 = h_ref[...] + gate * emb
    part = jnp.sum(h3 * h3, axis=-1, keepdims=True)

    @pl.when(j == 0)
    def _():
        ssq3_ref[...] = part

    @pl.when(j > 0)
    def _():
        ssq3_ref[...] += part

    for jj in range(D // tn):
        @pl.when(j == jj)
        def _(jj=jj):
            o_ref[:, jj * tn:(jj + 1) * tn] = h3

    @pl.when(j == pl.num_programs(1) - 1)
    def _():
        o_ref[...] = (o_ref[...] * _row_scale(ssq3_ref, D)) * gf_ref[...]


def ple_final(hg, ssq, h, w_gate, p, w_proj, g_final, *, tm, tn):
    T, D = h.shape
    P = p.shape[1]
    est = (2 * _nbytes((tm, D), F32) + 2 * _nbytes((tm, D), BF16)
           + 2 * _nbytes((D, tn), BF16) + 2 * _nbytes((P, tn), BF16)
           + 2 * _nbytes((tm, P), F32) + 6 * _nbytes((tm, tn), F32))
    return pl.pallas_call(
        _ple_kernel,
        grid=(T // tm, D // tn),
        in_specs=[pl.BlockSpec((tm, D), lambda i, j: (i, 0)),
                  pl.BlockSpec((tm, 1), lambda i, j: (i, 0)),
                  pl.BlockSpec((tm, tn), lambda i, j: (i, j)),
                  pl.BlockSpec((D, tn), lambda i, j: (0, j)),
                  pl.BlockSpec((tm, P), lambda i, j: (i, 0)),
                  pl.BlockSpec((P, tn), lambda i, j: (0, j)),
                  pl.BlockSpec((1, D), lambda i, j: (0, 0))],
        out_specs=pl.BlockSpec((tm, D), lambda i, j: (i, 0)),
        out_shape=jax.ShapeDtypeStruct((T, D), F32),
        scratch_shapes=[pltpu.VMEM((tm, 1), F32)],
        compiler_params=pltpu.CompilerParams(
            dimension_semantics=("parallel", "arbitrary"),
            vmem_limit_bytes=_vmem_limit(est)),
        name="ple_final",
    )(hg, ssq, h, w_gate, p, w_proj, g_final)


def _rope_tables(seq, dim):
    inv = 1.0 / (ROPE_BASE ** (jnp.arange(0, dim, 2, dtype=F32) / dim))
    ang = jnp.arange(seq, dtype=F32)[:, None] * inv[None, :]
    return jnp.cos(ang), jnp.sin(ang)


def _tile_config(seq):
    return dict(
        prenorm_tm=min(512, seq),
        in_proj=dict(tm=min(1024, seq), tn=768),
        retention_rows=min(1024, seq),
        mla_proj_tm=min(256, seq),
        attn_tile=min(512, seq),
        attn_heads=4,
        out_proj=dict(tm=min(1024, seq), tn=512),
        ffn_up=dict(tm=min(1024, seq), tf=512),
        ffn_down=dict(tm=min(512, seq), tn=512),
        ple=dict(tm=min(512, seq), tn=512),
    )


def _layer(h, p_i, w_in, g_attn, g_q_lora, g_kv_lora, w_uq, w_ukv, w_o, g_ffn,
           w_ffn_gate, w_ffn_up, conv_w, conv_b, w_ffn_down, g_ple, w_ple_gate,
           w_ple_proj, g_out, *, batch, seq):
    cfg = _tile_config(seq)
    D = h.shape[1]
    ret_w = RET_HEADS * RET_HEAD_DIM
    q_lora = w_uq.shape[0]
    kv_lora = w_ukv.shape[0]
    d_ff = w_ffn_gate.shape[1]

    in_w = w_in.shape[1]
    tn_in = cfg["in_proj"]["tn"]
    in_pad = pl.cdiv(in_w, tn_in) * tn_in
    xg, ssq0 = prenorm(h, g_attn.reshape(1, D), tm=cfg["prenorm_tm"])
    proj = in_projection(xg, ssq0, w_in, n_out=in_pad, out_dtype=BF16, **cfg["in_proj"])

    log_g = jnp.log1p(-jnp.exp2(-5.0 - jnp.arange(RET_HEADS, dtype=F32)))
    cos_r, sin_r = _rope_tables(seq, RET_HEAD_DIM)
    ro = retention_group(proj, log_g, cos_r, sin_r, batch=batch, seq=seq,
                         rows=cfg["retention_rows"])

    hq = MLA_NOPE + MLA_ROPE
    wq = w_uq.reshape(q_lora, MLA_HEADS, hq)
    wq = jnp.pad(wq, ((0, 0), (0, 0), (0, MLA_QK_PAD - hq)))
    wq = wq.reshape(q_lora, MLA_HEADS * MLA_QK_PAD).astype(BF16)
    wkv = w_ukv.reshape(kv_lora, MLA_HEADS, MLA_NOPE + MLA_V)
    wk = wkv[:, :, :MLA_NOPE].reshape(kv_lora, MLA_HEADS * MLA_NOPE).astype(BF16)
    wvt = wkv[:, :, MLA_NOPE:].reshape(kv_lora, MLA_HEADS * MLA_V).T.astype(BF16)
    cos_m, sin_m = _rope_tables(seq, MLA_ROPE)
    zeros = jnp.zeros_like(cos_m)
    fill = jnp.zeros((seq, LANES - MLA_ROPE), F32)
    rope_c = jnp.concatenate([cos_m, cos_m, fill], axis=1)
    rope_n = jnp.concatenate([-sin_m, zeros, fill], axis=1)
    rope_p = jnp.concatenate([zeros, sin_m, fill], axis=1)
    cq_off = 4 * ret_w
    ckv_off = cq_off + q_lora
    kr_off = ckv_off + kv_lora
    assert cq_off % q_lora == 0 and ckv_off % kv_lora == 0 and kr_off % (2 * LANES) == 0
    assert in_pad - kr_off >= 2 * LANES
    qc, kc, vt = mla_projections(
        proj, g_q_lora.reshape(1, q_lora), g_kv_lora.reshape(1, kv_lora), wq, wk, wvt,
        rope_c, rope_n, rope_p, seq=seq, tm=cfg["mla_proj_tm"], kv_tile=cfg["attn_tile"],
        cq_blk=cq_off // q_lora, ckv_blk=ckv_off // kv_lora, kr_blk=kr_off // (2 * LANES))
    mo = mla_attention(qc, kc, vt, batch=batch, seq=seq, tile=cfg["attn_tile"],
                       heads=cfg["attn_heads"])

    h1, hg1, ssq1 = out_projection(ro, mo, w_o.astype(BF16), h, g_ffn.reshape(1, D),
                                   **cfg["out_proj"])

    up_args = (hg1, ssq1, w_ffn_gate.astype(BF16), w_ffn_up.astype(BF16), conv_w,
               conv_b.reshape(1, d_ff))
    tf = cfg["ffn_up"]["tf"]
    n_main = d_ff // tf * tf
    hidden_main = ffn_up(*up_args, seq=seq, tm=cfg["ffn_up"]["tm"], tf=tf, col_start=0,
                         n_cols=n_main, name="ffn_up")
    hidden_tail = ffn_up(*up_args, seq=seq, tm=cfg["ffn_up"]["tm"], tf=d_ff - n_main,
                         col_start=n_main, n_cols=d_ff - n_main, name="ffn_up_tail")
    h2, hg2, ssq2 = ffn_down(hidden_main, hidden_tail, w_ffn_down.astype(BF16), h1,
                             g_ple.reshape(1, D), **cfg["ffn_down"])

    return ple_final(hg2, ssq2, h2, w_ple_gate.astype(BF16), p_i, w_ple_proj.astype(BF16),
                     g_out.reshape(1, D), **cfg["ple"])


def kernel(x, p, w_in, g_attn, g_q_lora, g_kv_lora, w_uq, w_ukv, w_o, g_ffn, w_ffn_gate,
           w_ffn_up, conv_w, conv_b, w_ffn_down, g_ple, w_ple_gate, w_ple_proj, g_final):
    B, S, D = x.shape
    depth = p.shape[0]
    assert depth == 1, "the final RMSNorm is fused into the layer's last kernel"
    h = x.reshape(B * S, D)
    out = _layer(h, p[0].reshape(B * S, -1), w_in[0], g_attn[0], g_q_lora[0], g_kv_lora[0],
                 w_uq[0], w_ukv[0], w_o[0], g_ffn[0], w_ffn_gate[0], w_ffn_up[0], conv_w[0],
                 conv_b[0], w_ffn_down[0], g_ple[0], w_ple_gate[0], w_ple_proj[0], g_final,
                 batch=B, seq=S)
    return out.reshape(B, S, D)
```

```python
import functools
import math

import jax
import jax.numpy as jnp
from jax import lax
from jax.experimental import pallas as pl
from jax.experimental.pallas import tpu as pltpu

F32 = jnp.float32
BF16 = jnp.bfloat16

EPS = 1e-6
ROPE_BASE = 10000.0
CHUNK = 64
CHUNK_SHIFT = 6
RET_HEADS = 8
RET_HEAD_DIM = 256
MLA_HEADS = 16
MLA_NOPE = 128
MLA_ROPE = 64
MLA_V = 128
MLA_QK_PAD = 256
CONV_WIDTH = 3
RET_BLOCK = 256
LANES = 128
CONV_HALO = 8
MIB = 1024 * 1024
VMEM_LIMIT_CAP = 60 * MIB


def _vmem_limit(nbytes):
    return int(min(VMEM_LIMIT_CAP, nbytes + 16 * MIB))


def _nbytes(shape, dtype):
    return math.prod(shape) * jnp.dtype(dtype).itemsize


def _rms_rows(x_ref, g_ref, o_ref, n_rows, row_chunk=16, unroll=4):
    g = g_ref[...]

    def body(c, carry):
        r = pl.multiple_of(c * row_chunk, row_chunk)
        x = x_ref[pl.ds(r, row_chunk), :].astype(F32)
        ms = jnp.mean(x * x, axis=-1, keepdims=True)
        o_ref[pl.ds(r, row_chunk), :] = ((x * lax.rsqrt(ms + EPS)) * g).astype(o_ref.dtype)
        return carry

    lax.fori_loop(0, n_rows // row_chunk, body, 0, unroll=unroll)


def _row_scale(ssq_ref, width):
    return lax.rsqrt(ssq_ref[...] * (1.0 / width) + EPS)


def _silu(x):
    return x * (1.0 / (1.0 + jnp.exp(-x)))


def _prenorm_kernel(x_ref, g_ref, xg_ref, ssq_ref, *, row_chunk):
    g = g_ref[...]

    def body(c, carry):
        rows = pl.ds(pl.multiple_of(c * row_chunk, row_chunk), row_chunk)
        x = x_ref[rows, :]
        ssq_ref[rows, :] = jnp.sum(x * x, axis=-1, keepdims=True)
        xg_ref[rows, :] = (x * g).astype(xg_ref.dtype)
        return carry

    lax.fori_loop(0, x_ref.shape[0] // row_chunk, body, 0, unroll=4)


def prenorm(x, g, *, tm):
    T, K = x.shape
    est = 2 * _nbytes((tm, K), F32) + 2 * _nbytes((tm, K), BF16)
    return pl.pallas_call(
        functools.partial(_prenorm_kernel, row_chunk=16),
        grid=(T // tm,),
        in_specs=[pl.BlockSpec((tm, K), lambda i: (i, 0)),
                  pl.BlockSpec((1, K), lambda i: (0, 0))],
        out_specs=[pl.BlockSpec((tm, K), lambda i: (i, 0)),
                   pl.BlockSpec((tm, 1), lambda i: (i, 0))],
        out_shape=[jax.ShapeDtypeStruct((T, K), BF16), jax.ShapeDtypeStruct((T, 1), F32)],
        compiler_params=pltpu.CompilerParams(
            dimension_semantics=("parallel",),
            vmem_limit_bytes=_vmem_limit(est)),
        name="prenorm",
    )(x, g)


def _in_proj_kernel(xg_ref, ssq_ref, wt_ref, o_ref, wb_ref, *, edge_rows):
    tn, K = wt_ref.shape

    @pl.when(pl.program_id(1) == 0)
    def _():
        w = wt_ref[...]
        if edge_rows:
            last = pl.program_id(0) == pl.num_programs(0) - 1
            valid = jnp.where(last, edge_rows, tn)
            row = lax.broadcasted_iota(jnp.int32, (tn, 1), 0)
            w = jnp.where(row < valid, w, 0.0)
        wb_ref[...] = w.astype(wb_ref.dtype)

    res = lax.dot_general(xg_ref[...], wb_ref[...], (((1,), (1,)), ((), ())),
                          preferred_element_type=F32)
    o_ref[...] = (res * _row_scale(ssq_ref, K)).astype(o_ref.dtype)


def in_projection(xg, ssq, w_t, *, n_out, tm, tn, out_dtype):
    T, K = xg.shape
    N = n_out
    assert N % tn == 0 and N - tn < w_t.shape[0] <= N
    est = (2 * _nbytes((tm, K), BF16) + 2 * _nbytes((tn, K), w_t.dtype)
           + _nbytes((tn, K), BF16) + 2 * _nbytes((tm, tn), out_dtype)
           + _nbytes((tm, tn), F32))
    return pl.pallas_call(
        functools.partial(_in_proj_kernel, edge_rows=w_t.shape[0] % tn),
        grid=(N // tn, T // tm),
        in_specs=[pl.BlockSpec((tm, K), lambda j, i: (i, 0)),
                  pl.BlockSpec((tm, 1), lambda j, i: (i, 0)),
                  pl.BlockSpec((tn, K), lambda j, i: (j, 0))],
        out_specs=pl.BlockSpec((tm, tn), lambda j, i: (i, j)),
        out_shape=jax.ShapeDtypeStruct((T, N), out_dtype),
        scratch_shapes=[pltpu.VMEM((tn, K), BF16)],
        compiler_params=pltpu.CompilerParams(
            dimension_semantics=("arbitrary", "arbitrary"),
            vmem_limit_bytes=_vmem_limit(est)),
        name="in_proj",
    )(xg, ssq, w_t)


def _retention_kernel(lg_ref, q_ref, k_ref, v_ref, gate_ref, cos_ref, sin_ref, o_ref,
                      state_ref, dmat_ref, qdec_ref, kdec_ref, sdec_ref, *, k_scale):
    L = RET_BLOCK
    dk = q_ref.shape[1]
    half = dk // 2
    lg = lg_ref[pl.program_id(1)]

    @pl.when(pl.program_id(2) == 0)
    def _init():
        state_ref[...] = jnp.zeros_like(state_ref)
        n = lax.broadcasted_iota(jnp.int32, (L, L), 0)
        m = lax.broadcasted_iota(jnp.int32, (L, L), 1)
        cn = n >> CHUNK_SHIFT
        cm = m >> CHUNK_SHIFT
        d = (n - m).astype(F32)
        expo = jnp.where(cn == cm, jnp.abs(d), d)
        visible = cm <= cn
        dmat_ref[...] = jnp.where(visible, jnp.exp(lg * jnp.where(visible, expo, 0.0)), 0.0)
        row = lax.broadcasted_iota(jnp.int32, (L, dk), 0).astype(F32)
        qdec_ref[...] = jnp.exp(lg * (row + 1.0))
        kdec_ref[...] = jnp.exp(lg * (float(L - 1) - row))
        sdec_ref[...] = jnp.exp(jnp.full(sdec_ref.shape, lg * float(L), F32))

    def rope(x, cos, sin):
        x1 = x[:, :half]
        x2 = x[:, half:]
        return jnp.concatenate([x1 * cos - x2 * sin, x2 * cos + x1 * sin], axis=1)

    for sub in range(q_ref.shape[0] // L):
        rows = pl.ds(sub * L, L)
        cos = cos_ref[rows, :]
        sin = sin_ref[rows, :]
        q = rope(q_ref[rows, :].astype(F32), cos, sin)
        k = rope(k_ref[rows, :].astype(F32), cos, sin) * k_scale
        v = v_ref[rows, :]
        qb = q.astype(BF16)
        kb = k.astype(BF16)
        scores = lax.dot_general(qb, kb, (((1,), (1,)), ((), ())),
                                 preferred_element_type=F32)
        state = state_ref[...]
        inter = jnp.dot((q * qdec_ref[...]).astype(BF16), state.astype(BF16),
                        preferred_element_type=F32)
        kd = (k * kdec_ref[...]).astype(BF16)
        state_ref[...] = state * sdec_ref[0:1, :] + lax.dot_general(
            kd, v, (((0,), (0,)), ((), ())), preferred_element_type=F32)
        scores = (scores * dmat_ref[...]).astype(BF16)
        out = jnp.dot(scores, v, preferred_element_type=F32) + inter
        mu = jnp.mean(out, axis=-1, keepdims=True)
        cen = out - mu
        var = jnp.mean(cen * cen, axis=-1, keepdims=True)
        gate = gate_ref[rows, :].astype(F32)
        o_ref[rows, :] = (_silu(gate) * (cen * lax.rsqrt(var + EPS))).astype(o_ref.dtype)


def retention_group(proj, log_g, cos, sin, *, batch, seq, rows):
    T = proj.shape[0]
    H, dk = RET_HEADS, RET_HEAD_DIM
    nblk = seq // rows

    def col(offset):
        return pl.BlockSpec((rows, dk), lambda b, h, r: (b * nblk + r, offset + h))

    tab = pl.BlockSpec((rows, dk // 2), lambda b, h, r: (r, 0))
    est = 10 * _nbytes((rows, dk), BF16) + 4 * _nbytes((rows, dk // 2), F32) \
        + 4 * _nbytes((RET_BLOCK, dk), F32)
    return pl.pallas_call(
        functools.partial(_retention_kernel, k_scale=dk ** -0.5),
        grid=(batch, H, nblk),
        in_specs=[pl.BlockSpec(memory_space=pltpu.SMEM),
                  col(0), col(H), col(2 * H), col(3 * H), tab, tab],
        out_specs=pl.BlockSpec((rows, dk), lambda b, h, r: (b * nblk + r, h)),
        out_shape=jax.ShapeDtypeStruct((T, H * dk), BF16),
        scratch_shapes=[pltpu.VMEM((dk, dk), F32),
                        pltpu.VMEM((RET_BLOCK, RET_BLOCK), F32),
                        pltpu.VMEM((RET_BLOCK, dk), F32),
                        pltpu.VMEM((RET_BLOCK, dk), F32),
                        pltpu.VMEM((8, dk), F32)],
        compiler_params=pltpu.CompilerParams(
            dimension_semantics=("parallel", "parallel", "arbitrary"),
            vmem_limit_bytes=_vmem_limit(est)),
        name="retention",
    )(log_g, proj, proj, proj, proj, cos, sin)


def _mla_proj_kernel(cq_ref, ckv_ref, kr_ref, gq_ref, gkv_ref, wq_ref, wk_ref, wvt_ref,
                     rc_ref, rn_ref, rp_ref, q_ref, k_ref, vt_ref, cqn_ref, ckvn_ref,
                     *, scale):
    tm = cq_ref.shape[0]
    _rms_rows(cq_ref, gq_ref, cqn_ref, tm)
    _rms_rows(ckv_ref, gkv_ref, ckvn_ref, tm)
    rc = rc_ref[...]
    rn = rn_ref[...]
    rp = rp_ref[...]

    def rope(x):
        return (x * rc + pltpu.roll(x, LANES - MLA_ROPE // 2, 1) * rn
                + pltpu.roll(x, MLA_ROPE // 2, 1) * rp)

    krp = rope(kr_ref[:, :LANES].astype(F32)).astype(k_ref.dtype)
    cqn = cqn_ref[...]
    ckvn = ckvn_ref[...]
    W = MLA_QK_PAD
    for h in range(MLA_HEADS):
        qh = jnp.dot(cqn, wq_ref[:, h * W:(h + 1) * W], preferred_element_type=F32) * scale
        q_ref[:, h * W:h * W + LANES] = qh[:, :LANES].astype(q_ref.dtype)
        q_ref[:, h * W + LANES:(h + 1) * W] = rope(qh[:, LANES:]).astype(q_ref.dtype)
    for c in range(MLA_HEADS // 2):
        kn = jnp.dot(ckvn, wk_ref[:, c * W:(c + 1) * W], preferred_element_type=F32)
        for s in range(2):
            h = 2 * c + s
            k_ref[:, h * W:h * W + LANES] = kn[:, s * LANES:(s + 1) * LANES].astype(k_ref.dtype)
            k_ref[:, h * W + LANES:(h + 1) * W] = krp
    vt_ref[0] = lax.dot_general(wvt_ref[...], ckvn, (((1,), (1,)), ((), ())),
                                preferred_element_type=F32).astype(vt_ref.dtype)


def mla_projections(proj, g_q, g_kv, wq, wk, wvt, rope_c, rope_n, rope_p, *, seq, tm,
                    kv_tile, cq_blk, ckv_blk, kr_blk):
    T = proj.shape[0]
    q_lora = wq.shape[0]
    kv_lora = wk.shape[0]
    nq = wq.shape[1]
    nv = wvt.shape[0]
    nblk = seq // tm
    per_tile = kv_tile // tm
    scale = (MLA_NOPE + MLA_ROPE) ** -0.5 * math.log2(math.e)
    const = lambda i: (0, 0)
    tab = pl.BlockSpec((tm, LANES), lambda i: (i % nblk, 0))
    est = (2 * (_nbytes(wq.shape, BF16) + _nbytes(wk.shape, BF16) + _nbytes(wvt.shape, BF16))
           + 2 * (2 * _nbytes((tm, nq), BF16) + _nbytes((tm, nv), BF16))
           + 3 * _nbytes((tm, q_lora + kv_lora + 2 * LANES), BF16)
           + 6 * _nbytes((tm, LANES), F32) + _nbytes((tm, nv), F32))
    return pl.pallas_call(
        functools.partial(_mla_proj_kernel, scale=scale),
        grid=(T // tm,),
        in_specs=[pl.BlockSpec((tm, q_lora), lambda i: (i, cq_blk)),
                  pl.BlockSpec((tm, kv_lora), lambda i: (i, ckv_blk)),
                  pl.BlockSpec((tm, 2 * LANES), lambda i: (i, kr_blk)),
                  pl.BlockSpec((1, q_lora), const),
                  pl.BlockSpec((1, kv_lora), const),
                  pl.BlockSpec(wq.shape, const),
                  pl.BlockSpec(wk.shape, const),
                  pl.BlockSpec(wvt.shape, const),
                  tab, tab, tab],
        out_specs=[pl.BlockSpec((tm, nq), lambda i: (i, 0)),
                   pl.BlockSpec((tm, nq), lambda i: (i, 0)),
                   pl.BlockSpec((1, nv, tm), lambda i: (i // per_tile, 0, i % per_tile))],
        out_shape=[jax.ShapeDtypeStruct((T, nq), BF16),
                   jax.ShapeDtypeStruct((T, nq), BF16),
                   jax.ShapeDtypeStruct((T // kv_tile, nv, kv_tile), BF16)],
        scratch_shapes=[pltpu.VMEM((tm, q_lora), BF16),
                        pltpu.VMEM((tm, kv_lora), BF16)],
        compiler_params=pltpu.CompilerParams(
            dimension_semantics=("parallel",),
            vmem_limit_bytes=_vmem_limit(est)),
        name="mla_proj",
    )(proj, proj, proj, g_q, g_kv, wq, wk, wvt, rope_c, rope_n, rope_p)


def _mla_attn_kernel(q_ref, k_ref, vt_ref, o_ref, m_ref, l_ref, acc_ref, *, tile, heads):
    qi = pl.program_id(2)
    W, dv = MLA_QK_PAD, MLA_V
    qs = [q_ref[:, h * W:(h + 1) * W] for h in range(heads)]
    m_ref[...] = jnp.full(m_ref.shape, -jnp.inf, F32)
    l_ref[...] = jnp.zeros(l_ref.shape, F32)
    acc_ref[...] = jnp.zeros(acc_ref.shape, F32)

    def process(tiles):
        scores = []
        for j, _ in tiles:
            start = pl.multiple_of(j * tile, tile)
            scores.append([
                lax.dot_general(k_ref[pl.ds(start, tile), h * W:(h + 1) * W], qs[h],
                                (((1,), (1,)), ((), ())), preferred_element_type=F32)
                for h in range(heads)])
        for t, (j, mask) in enumerate(tiles):
            for h in range(heads):
                s = scores[t][h]
                if mask is not None:
                    s = jnp.where(mask, s, -jnp.inf)
                m = m_ref[h]
                m_new = jnp.maximum(m, jnp.max(s, axis=0, keepdims=True))
                alpha = jnp.exp2(m - m_new)
                p = jnp.exp2(s - m_new)
                vt = vt_ref[j, h * dv:(h + 1) * dv, :]
                l_ref[h] = alpha * l_ref[h] + jnp.sum(p, axis=0, keepdims=True)
                acc_ref[h] = alpha * acc_ref[h] + jnp.dot(vt, p.astype(vt.dtype),
                                                         preferred_element_type=F32)
                m_ref[h] = m_new

    key_chunk = lax.broadcasted_iota(jnp.int32, (tile, tile), 0) >> CHUNK_SHIFT
    qry_chunk = lax.broadcasted_iota(jnp.int32, (tile, tile), 1) >> CHUNK_SHIFT
    diag_mask = key_chunk <= qry_chunk

    @pl.when(qi == 0)
    def _():
        process([(qi, diag_mask)])

    @pl.when(qi > 0)
    def _():
        lead = (qi + 1) % 2

        @pl.when(lead == 1)
        def _():
            process([(0, None)])

        def pair(i, carry):
            j = lead + 2 * i
            process([(j, None), (j + 1, None)])
            return carry

        lax.fori_loop(0, (qi + 1 - lead) // 2 - 1, pair, 0)
        process([(qi - 1, None), (qi, diag_mask)])

    for h in range(heads):
        o_ref[:, h * dv:(h + 1) * dv] = (acc_ref[h] * (1.0 / l_ref[h])).T.astype(o_ref.dtype)


def mla_attention(qc, kc, vt, *, batch, seq, tile, heads):
    T = qc.shape[0]
    H, W, dv = MLA_HEADS, MLA_QK_PAD, MLA_V
    nq = seq // tile
    est = (2 * heads * (_nbytes((seq, W), BF16) + _nbytes((seq, dv), BF16)
                        + _nbytes((tile, W), BF16) + _nbytes((tile, dv), BF16))
           + 4 * heads * _nbytes((tile, tile), F32))
    return pl.pallas_call(
        functools.partial(_mla_attn_kernel, tile=tile, heads=heads),
        grid=(batch, H // heads, nq),
        in_specs=[pl.BlockSpec((tile, heads * W), lambda b, h, i: (b * nq + i, h)),
                  pl.BlockSpec((seq, heads * W), lambda b, h, i: (b, h)),
                  pl.BlockSpec((nq, heads * dv, tile), lambda b, h, i: (b, h, 0))],
        out_specs=pl.BlockSpec((tile, heads * dv), lambda b, h, i: (b * nq + i, h)),
        out_shape=jax.ShapeDtypeStruct((T, H * dv), BF16),
        scratch_shapes=[pltpu.VMEM((heads, 1, tile), F32),
                        pltpu.VMEM((heads, 1, tile), F32),
                        pltpu.VMEM((heads, dv, tile), F32)],
        compiler_params=pltpu.CompilerParams(
            dimension_semantics=("parallel", "parallel", "arbitrary"),
            vmem_limit_bytes=_vmem_limit(est)),
        name="mla_attn",
    )(qc, kc, vt)


def _emit_residual(h, g_ref, o_ref, hg_ref, ssq_ref):
    o_ref[...] = h
    hg_ref[...] = (h * g_ref[...]).astype(hg_ref.dtype)
    part = jnp.sum(h * h, axis=-1, keepdims=True)

    @pl.when(pl.program_id(1) == 0)
    def _():
        ssq_ref[...] = part

    @pl.when(pl.program_id(1) > 0)
    def _():
        ssq_ref[...] += part


def _residual_out(T, N, tm, tn):
    specs = [pl.BlockSpec((tm, tn), lambda i, j: (i, j)),
             pl.BlockSpec((tm, tn), lambda i, j: (i, j)),
             pl.BlockSpec((tm, 1), lambda i, j: (i, 0))]
    shapes = [jax.ShapeDtypeStruct((T, N), F32), jax.ShapeDtypeStruct((T, N), BF16),
              jax.ShapeDtypeStruct((T, 1), F32)]
    return specs, shapes


def _out_proj_kernel(ro_ref, mo_ref, wr_ref, wm_ref, x_ref, g_ref, o_ref, hg_ref, ssq_ref):
    acc = jnp.dot(ro_ref[...], wr_ref[...], preferred_element_type=F32)
    acc = acc + jnp.dot(mo_ref[...], wm_ref[...], preferred_element_type=F32)
    _emit_residual(x_ref[...] + acc, g_ref, o_ref, hg_ref, ssq_ref)


def out_projection(ro, mo, w_o, x, g_next, *, tm, tn):
    T, kr = ro.shape
    km = mo.shape[1]
    assert kr == km
    N = w_o.shape[1]
    est = (4 * _nbytes((tm, kr), BF16) + 4 * _nbytes((kr, tn), BF16)
           + 7 * _nbytes((tm, tn), F32))
    out_specs, out_shape = _residual_out(T, N, tm, tn)
    return pl.pallas_call(
        _out_proj_kernel,
        grid=(T // tm, N // tn),
        in_specs=[pl.BlockSpec((tm, kr), lambda i, j: (i, 0)),
                  pl.BlockSpec((tm, km), lambda i, j: (i, 0)),
                  pl.BlockSpec((kr, tn), lambda i, j: (0, j)),
                  pl.BlockSpec((km, tn), lambda i, j: (1, j)),
                  pl.BlockSpec((tm, tn), lambda i, j: (i, j)),
                  pl.BlockSpec((1, tn), lambda i, j: (0, j))],
        out_specs=out_specs,
        out_shape=out_shape,
        compiler_params=pltpu.CompilerParams(
            dimension_semantics=("parallel", "arbitrary"),
            vmem_limit_bytes=_vmem_limit(est)),
        name="out_proj",
    )(ro, mo, w_o, w_o, x, g_next)


def _ffn_up_kernel(hg_ref, ssq_ref, wg_ref, wu_ref, cw_ref, cb_ref, o_ref, g_ref,
                   *, tiles_per_seq):
    tm = hg_ref.shape[0]
    halo = CONV_HALO

    @pl.when(pl.program_id(1) % tiles_per_seq == 0)
    def _():
        g_ref[0:halo, :] = jnp.zeros((halo, g_ref.shape[1]), F32)

    hg = hg_ref[...]
    r = _row_scale(ssq_ref, hg_ref.shape[1])
    g_ref[halo:halo + tm, :] = jnp.dot(hg, wg_ref[...], preferred_element_type=F32) * r
    up = jnp.dot(hg, wu_ref[...], preferred_element_type=F32) * r
    a = cb_ref[...]
    for j in range(CONV_WIDTH):
        shift = CONV_WIDTH - 1 - j
        a = a + g_ref[halo - shift:halo - shift + tm, :] * cw_ref[j:j + 1, :]
    o_ref[...] = (_silu(a) * up).astype(o_ref.dtype)
    g_ref[0:halo, :] = g_ref[tm:tm + halo, :]


def ffn_up(hg, ssq, w_gate, w_up, conv_w, conv_b, *, seq, tm, tf, col_start, n_cols, name):
    T, K = hg.shape
    assert n_cols % tf == 0 and col_start % tf == 0
    first = col_start // tf
    est = (2 * _nbytes((tm, K), BF16) + 4 * _nbytes((K, tf), BF16)
           + 2 * _nbytes((tm, tf), BF16) + 4 * _nbytes((tm + CONV_HALO, tf), F32)
           + 2 * _nbytes((tm, LANES), F32))
    return pl.pallas_call(
        functools.partial(_ffn_up_kernel, tiles_per_seq=seq // tm),
        grid=(n_cols // tf, T // tm),
        in_specs=[pl.BlockSpec((tm, K), lambda j, i: (i, 0)),
                  pl.BlockSpec((tm, 1), lambda j, i: (i, 0)),
                  pl.BlockSpec((K, tf), lambda j, i: (0, first + j)),
                  pl.BlockSpec((K, tf), lambda j, i: (0, first + j)),
                  pl.BlockSpec((CONV_WIDTH, tf), lambda j, i: (0, first + j)),
                  pl.BlockSpec((1, tf), lambda j, i: (0, first + j))],
        out_specs=pl.BlockSpec((tm, tf), lambda j, i: (i, j)),
        out_shape=jax.ShapeDtypeStruct((T, n_cols), BF16),
        scratch_shapes=[pltpu.VMEM((tm + CONV_HALO, tf), F32)],
        compiler_params=pltpu.CompilerParams(
            dimension_semantics=("arbitrary", "arbitrary"),
            vmem_limit_bytes=_vmem_limit(est)),
        name=name,
    )(hg, ssq, w_gate, w_up, conv_w, conv_b)


def _ffn_down_kernel(a_ref, b_ref, wa_ref, wb_ref, h_ref, g_ref, o_ref, hg_ref, ssq_ref):
    acc = jnp.dot(a_ref[...], wa_ref[...], preferred_element_type=F32)
    acc = acc + jnp.dot(b_ref[...], wb_ref[...], preferred_element_type=F32)
    _emit_residual(h_ref[...] + acc, g_ref, o_ref, hg_ref, ssq_ref)


def ffn_down(hidden_main, hidden_tail, w_down, h, g_next, *, tm, tn):
    T, ka = hidden_main.shape
    kb = hidden_tail.shape[1]
    assert ka % kb == 0 and ka + kb == w_down.shape[0]
    N = w_down.shape[1]
    est = (2 * _nbytes((tm, ka + kb), BF16) + 2 * _nbytes((ka + kb, tn), BF16)
           + 7 * _nbytes((tm, tn), F32))
    out_specs, out_shape = _residual_out(T, N, tm, tn)
    return pl.pallas_call(
        _ffn_down_kernel,
        grid=(T // tm, N // tn),
        in_specs=[pl.BlockSpec((tm, ka), lambda i, j: (i, 0)),
                  pl.BlockSpec((tm, kb), lambda i, j: (i, 0)),
                  pl.BlockSpec((ka, tn), lambda i, j: (0, j)),
                  pl.BlockSpec((kb, tn), lambda i, j: (ka // kb, j)),
                  pl.BlockSpec((tm, tn), lambda i, j: (i, j)),
                  pl.BlockSpec((1, tn), lambda i, j: (0, j))],
        out_specs=out_specs,
        out_shape=out_shape,
        compiler_params=pltpu.CompilerParams(
            dimension_semantics=("parallel", "arbitrary"),
            vmem_limit_bytes=_vmem_limit(est)),
        name="ffn_down",
    )(hidden_main, hidden_tail, w_down, w_down, h, g_next)


def _ple_kernel(hg_ref, ssq_ref, h_ref, wg_ref, p_ref, wp_ref, gf_ref, o_ref, ssq3_ref):
    j = pl.program_id(1)
    tm, D = hg_ref.shape
    tn = wg_ref.shape[1]

    z = jnp.dot(hg_ref[...], wg_ref[...], preferred_element_type=F32) * _row_scale(ssq_ref, D)
    gate = 1.0 / (1.0 + jnp.exp(-z))
    emb = jnp.dot(p_ref[...].astype(BF16), wp_ref[...], preferred_element_type=F32)
    h3 = h_ref[...] + gate * emb
    part = jnp.sum(h3 * h3, axis=-1, keepdims=True)

    @pl.when(j == 0)
    def _():
        ssq3_ref[...] = part

    @pl.when(j > 0)
    def _():
        ssq3_ref[...] += part

    for jj in range(D // tn):
        @pl.when(j == jj)
        def _(jj=jj):
            o_ref[:, jj * tn:(jj + 1) * tn] = h3

    @pl.when(j == pl.num_programs(1) - 1)
    def _():
        o_ref[...] = (o_ref[...] * _row_scale(ssq3_ref, D)) * gf_ref[...]


def ple_final(hg, ssq, h, w_gate, p, w_proj, g_final, *, tm, tn):
    T, D = h.shape
    P = p.shape[1]
    est = (2 * _nbytes((tm, D), F32) + 2 * _nbytes((tm, D), BF16)
           + 2 * _nbytes((D, tn), BF16) + 2 * _nbytes((P, tn), BF16)
           + 2 * _nbytes((tm, P), F32) + 6 * _nbytes((tm, tn), F32))
    return pl.pallas_call(
        _ple_kernel,
        grid=(T // tm, D // tn),
        in_specs=[pl.BlockSpec((tm, D), lambda i, j: (i, 0)),
                  pl.BlockSpec((tm, 1), lambda i, j: (i, 0)),
                  pl.BlockSpec((tm, tn), lambda i, j: (i, j)),
                  pl.BlockSpec((D, tn), lambda i, j: (0, j)),
                  pl.BlockSpec((tm, P), lambda i, j: (i, 0)),
                  pl.BlockSpec((P, tn), lambda i, j: (0, j)),
                  pl.BlockSpec((1, D), lambda i, j: (0, 0))],
        out_specs=pl.BlockSpec((tm, D), lambda i, j: (i, 0)),
        out_shape=jax.ShapeDtypeStruct((T, D), F32),
        scratch_shapes=[pltpu.VMEM((tm, 1), F32)],
        compiler_params=pltpu.CompilerParams(
            dimension_semantics=("parallel", "arbitrary"),
            vmem_limit_bytes=_vmem_limit(est)),
        name="ple_final",
    )(hg, ssq, h, w_gate, p, w_proj, g_final)


def _rope_tables(seq, dim):
    inv = 1.0 / (ROPE_BASE ** (jnp.arange(0, dim, 2, dtype=F32) / dim))
    ang = jnp.arange(seq, dtype=F32)[:, None] * inv[None, :]
    return jnp.cos(ang), jnp.sin(ang)


def _tile_config(seq):
    return dict(
        prenorm_tm=min(512, seq),
        in_proj=dict(tm=min(1024, seq), tn=768),
        retention_rows=min(2048, seq),
        mla_proj_tm=min(512, seq),
        attn_tile=min(512, seq),
        attn_heads=4,
        out_proj=dict(tm=min(1024, seq), tn=512),
        ffn_up=dict(tm=min(1024, seq), tf=512),
        ffn_down=dict(tm=min(512, seq), tn=512),
        ple=dict(tm=min(512, seq), tn=512),
    )


def _layer(h, p_i, w_in, g_attn, g_q_lora, g_kv_lora, w_uq, w_ukv, w_o, g_ffn,
           w_ffn_gate, w_ffn_up, conv_w, conv_b, w_ffn_down, g_ple, w_ple_gate,
           w_ple_proj, g_out, *, batch, seq):
    cfg = _tile_config(seq)
    D = h.shape[1]
    ret_w = RET_HEADS * RET_HEAD_DIM
    q_lora = w_uq.shape[0]
    kv_lora = w_ukv.shape[0]
    d_ff = w_ffn_gate.shape[1]

    in_w = w_in.shape[1]
    tn_in = cfg["in_proj"]["tn"]
    in_pad = pl.cdiv(in_w, tn_in) * tn_in
    xg, ssq0 = prenorm(h, g_attn.reshape(1, D), tm=cfg["prenorm_tm"])
    proj = in_projection(xg, ssq0, w_in.T, n_out=in_pad, out_dtype=BF16, **cfg["in_proj"])

    log_g = jnp.log1p(-jnp.exp2(-5.0 - jnp.arange(RET_HEADS, dtype=F32)))
    cos_r, sin_r = _rope_tables(seq, RET_HEAD_DIM)
    ro = retention_group(proj, log_g, cos_r, sin_r, batch=batch, seq=seq,
                         rows=cfg["retention_rows"])

    hq = MLA_NOPE + MLA_ROPE
    wq = w_uq.reshape(q_lora, MLA_HEADS, hq)
    wq = jnp.pad(wq, ((0, 0), (0, 0), (0, MLA_QK_PAD - hq)))
    wq = wq.reshape(q_lora, MLA_HEADS * MLA_QK_PAD).astype(BF16)
    wkv = w_ukv.reshape(kv_lora, MLA_HEADS, MLA_NOPE + MLA_V)
    wk = wkv[:, :, :MLA_NOPE].reshape(kv_lora, MLA_HEADS * MLA_NOPE).astype(BF16)
    wvt = wkv[:, :, MLA_NOPE:].reshape(kv_lora, MLA_HEADS * MLA_V).T.astype(BF16)
    cos_m, sin_m = _rope_tables(seq, MLA_ROPE)
    zeros = jnp.zeros_like(cos_m)
    fill = jnp.zeros((seq, LANES - MLA_ROPE), F32)
    rope_c = jnp.concatenate([cos_m, cos_m, fill], axis=1)
    rope_n = jnp.concatenate([-sin_m, zeros, fill], axis=1)
    rope_p = jnp.concatenate([zeros, sin_m, fill], axis=1)
    cq_off = 4 * ret_w
    ckv_off = cq_off + q_lora
    kr_off = ckv_off + kv_lora
    assert cq_off % q_lora == 0 and ckv_off % kv_lora == 0 and kr_off % (2 * LANES) == 0
    assert in_pad - kr_off >= 2 * LANES
    qc, kc, vt = mla_projections(
        proj, g_q_lora.reshape(1, q_lora), g_kv_lora.reshape(1, kv_lora), wq, wk, wvt,
        rope_c, rope_n, rope_p, seq=seq, tm=cfg["mla_proj_tm"], kv_tile=cfg["attn_tile"],
        cq_blk=cq_off // q_lora, ckv_blk=ckv_off // kv_lora, kr_blk=kr_off // (2 * LANES))
    mo = mla_attention(qc, kc, vt, batch=batch, seq=seq, tile=cfg["attn_tile"],
                       heads=cfg["attn_heads"])

    h1, hg1, ssq1 = out_projection(ro, mo, w_o.astype(BF16), h, g_ffn.reshape(1, D),
                                   **cfg["out_proj"])

    up_args = (hg1, ssq1, w_ffn_gate.astype(BF16), w_ffn_up.astype(BF16), conv_w,
               conv_b.reshape(1, d_ff))
    tf = cfg["ffn_up"]["tf"]
    n_main = d_ff // tf * tf
    hidden_main = ffn_up(*up_args, seq=seq, tm=cfg["ffn_up"]["tm"], tf=tf, col_start=0,
                         n_cols=n_main, name="ffn_up")
    hidden_tail = ffn_up(*up_args, seq=seq, tm=cfg["ffn_up"]["tm"], tf=d_ff - n_main,
                         col_start=n_main, n_cols=d_ff - n_main, name="ffn_up_tail")
    h2, hg2, ssq2 = ffn_down(hidden_main, hidden_tail, w_ffn_down.astype(BF16), h1,
                             g_ple.reshape(1, D), **cfg["ffn_down"])

    return ple_final(hg2, ssq2, h2, w_ple_gate.astype(BF16), p_i, w_ple_proj.astype(BF16),
                     g_out.reshape(1, D), **cfg["ple"])


def kernel(x, p, w_in, g_attn, g_q_lora, g_kv_lora, w_uq, w_ukv, w_o, g_ffn, w_ffn_gate,
           w_ffn_up, conv_w, conv_b, w_ffn_down, g_ple, w_ple_gate, w_ple_proj, g_final):
    B, S, D = x.shape
    depth = p.shape[0]
    assert depth == 1, "the final RMSNorm is fused into the layer's last kernel"
    h = x.reshape(B * S, D)
    out = _layer(h, p[0].reshape(B * S, -1), w_in[0], g_attn[0], g_q_lora[0], g_kv_lora[0],
                 w_uq[0], w_ukv[0], w_o[0], g_ffn[0], w_ffn_gate[0], w_ffn_up[0], conv_w[0],
                 conv_b[0], w_ffn_down[0], g_ple[0], w_ple_gate[0], w_ple_proj[0], g_final,
                 batch=B, seq=S)
    return out.reshape(B, S, D)
```

```python
import functools
import math

import jax
import jax.numpy as jnp
from jax import lax
from jax.experimental import pallas as pl
from jax.experimental.pallas import tpu as pltpu

F32 = jnp.float32
BF16 = jnp.bfloat16

EPS = 1e-6
ROPE_BASE = 10000.0
CHUNK = 64
CHUNK_SHIFT = 6
RET_HEADS = 8
RET_HEAD_DIM = 256
MLA_HEADS = 16
MLA_NOPE = 128
MLA_ROPE = 64
MLA_V = 128
MLA_QK_PAD = 256
CONV_WIDTH = 3
RET_BLOCK = 256
LANES = 128
CONV_HALO = 8
MIB = 1024 * 1024
VMEM_LIMIT_CAP = 60 * MIB


def _vmem_limit(nbytes):
    return int(min(VMEM_LIMIT_CAP, nbytes + 16 * MIB))


def _nbytes(shape, dtype):
    return math.prod(shape) * jnp.dtype(dtype).itemsize


def _rms_rows(x_ref, g_ref, o_ref, n_rows, row_chunk=16, unroll=4):
    g = g_ref[...]

    def body(c, carry):
        r = pl.multiple_of(c * row_chunk, row_chunk)
        x = x_ref[pl.ds(r, row_chunk), :].astype(F32)
        ms = jnp.mean(x * x, axis=-1, keepdims=True)
        o_ref[pl.ds(r, row_chunk), :] = ((x * lax.rsqrt(ms + EPS)) * g).astype(o_ref.dtype)
        return carry

    lax.fori_loop(0, n_rows // row_chunk, body, 0, unroll=unroll)


def _row_scale(ssq_ref, width):
    return lax.rsqrt(ssq_ref[...] * (1.0 / width) + EPS)


def _silu(x):
    return x * (1.0 / (1.0 + jnp.exp(-x)))


def _prenorm_kernel(x_ref, g_ref, xg_ref, ssq_ref, *, row_chunk):
    g = g_ref[...]

    def body(c, carry):
        rows = pl.ds(pl.multiple_of(c * row_chunk, row_chunk), row_chunk)
        x = x_ref[rows, :]
        ssq_ref[rows, :] = jnp.sum(x * x, axis=-1, keepdims=True)
        xg_ref[rows, :] = (x * g).astype(xg_ref.dtype)
        return carry

    lax.fori_loop(0, x_ref.shape[0] // row_chunk, body, 0, unroll=4)


def prenorm(x, g, *, tm):
    T, K = x.shape
    est = 2 * _nbytes((tm, K), F32) + 2 * _nbytes((tm, K), BF16)
    return pl.pallas_call(
        functools.partial(_prenorm_kernel, row_chunk=16),
        grid=(T // tm,),
        in_specs=[pl.BlockSpec((tm, K), lambda i: (i, 0)),
                  pl.BlockSpec((1, K), lambda i: (0, 0))],
        out_specs=[pl.BlockSpec((tm, K), lambda i: (i, 0)),
                   pl.BlockSpec((tm, 1), lambda i: (i, 0))],
        out_shape=[jax.ShapeDtypeStruct((T, K), BF16), jax.ShapeDtypeStruct((T, 1), F32)],
        compiler_params=pltpu.CompilerParams(
            dimension_semantics=("parallel",),
            vmem_limit_bytes=_vmem_limit(est)),
        name="prenorm",
    )(x, g)


def _in_proj_kernel(xg_ref, ssq_ref, wt_ref, o_ref, wb_ref, *, edge_rows):
    tn, K = wt_ref.shape

    @pl.when(pl.program_id(1) == 0)
    def _():
        w = wt_ref[...]
        if edge_rows:
            last = pl.program_id(0) == pl.num_programs(0) - 1
            valid = jnp.where(last, edge_rows, tn)
            row = lax.broadcasted_iota(jnp.int32, (tn, 1), 0)
            w = jnp.where(row < valid, w, 0.0)
        wb_ref[...] = w.astype(wb_ref.dtype)

    res = lax.dot_general(xg_ref[...], wb_ref[...], (((1,), (1,)), ((), ())),
                          preferred_element_type=F32)
    o_ref[...] = (res * _row_scale(ssq_ref, K)).astype(o_ref.dtype)


def in_projection(xg, ssq, w_t, *, n_out, tm, tn, out_dtype):
    T, K = xg.shape
    N = n_out
    assert N % tn == 0 and N - tn < w_t.shape[0] <= N
    est = (2 * _nbytes((tm, K), BF16) + 2 * _nbytes((tn, K), w_t.dtype)
           + _nbytes((tn, K), BF16) + 2 * _nbytes((tm, tn), out_dtype)
           + _nbytes((tm, tn), F32))
    return pl.pallas_call(
        functools.partial(_in_proj_kernel, edge_rows=w_t.shape[0] % tn),
        grid=(N // tn, T // tm),
        in_specs=[pl.BlockSpec((tm, K), lambda j, i: (i, 0)),
                  pl.BlockSpec((tm, 1), lambda j, i: (i, 0)),
                  pl.BlockSpec((tn, K), lambda j, i: (j, 0))],
        out_specs=pl.BlockSpec((tm, tn), lambda j, i: (i, j)),
        out_shape=jax.ShapeDtypeStruct((T, N), out_dtype),
        scratch_shapes=[pltpu.VMEM((tn, K), BF16)],
        compiler_params=pltpu.CompilerParams(
            dimension_semantics=("arbitrary", "arbitrary"),
            vmem_limit_bytes=_vmem_limit(est)),
        name="in_proj",
    )(xg, ssq, w_t)


def _retention_kernel(lg_ref, q_ref, k_ref, v_ref, gate_ref, cos_ref, sin_ref, o_ref,
                      state_ref, dmat_ref, qdec_ref, kdec_ref, sdec_ref, *, k_scale):
    L = RET_BLOCK
    dk = q_ref.shape[1]
    half = dk // 2
    lg = lg_ref[pl.program_id(1)]

    @pl.when(pl.program_id(2) == 0)
    def _init():
        state_ref[...] = jnp.zeros_like(state_ref)
        n = lax.broadcasted_iota(jnp.int32, (L, L), 0)
        m = lax.broadcasted_iota(jnp.int32, (L, L), 1)
        cn = n >> CHUNK_SHIFT
        cm = m >> CHUNK_SHIFT
        d = (n - m).astype(F32)
        expo = jnp.where(cn == cm, jnp.abs(d), d)
        visible = cm <= cn
        dmat_ref[...] = jnp.where(visible, jnp.exp(lg * jnp.where(visible, expo, 0.0)), 0.0)
        row = lax.broadcasted_iota(jnp.int32, (L, dk), 0).astype(F32)
        qdec_ref[...] = jnp.exp(lg * (row + 1.0))
        kdec_ref[...] = jnp.exp(lg * (float(L - 1) - row))
        sdec_ref[...] = jnp.exp(jnp.full(sdec_ref.shape, lg * float(L), F32))

    def rope(x, cos, sin):
        x1 = x[:, :half]
        x2 = x[:, half:]
        return jnp.concatenate([x1 * cos - x2 * sin, x2 * cos + x1 * sin], axis=1)

    for sub in range(q_ref.shape[0] // L):
        rows = pl.ds(sub * L, L)
        cos = cos_ref[rows, :]
        sin = sin_ref[rows, :]
        q = rope(q_ref[rows, :].astype(F32), cos, sin)
        k = rope(k_ref[rows, :].astype(F32), cos, sin) * k_scale
        v = v_ref[rows, :]
        qb = q.astype(BF16)
        kb = k.astype(BF16)
        scores = lax.dot_general(qb, kb, (((1,), (1,)), ((), ())),
                                 preferred_element_type=F32)
        state = state_ref[...]
        inter = jnp.dot((q * qdec_ref[...]).astype(BF16), state.astype(BF16),
                        preferred_element_type=F32)
        kd = (k * kdec_ref[...]).astype(BF16)
        state_ref[...] = state * sdec_ref[0:1, :] + lax.dot_general(
            kd, v, (((0,), (0,)), ((), ())), preferred_element_type=F32)
        scores = (scores * dmat_ref[...]).astype(BF16)
        out = jnp.dot(scores, v, preferred_element_type=F32) + inter
        mu = jnp.mean(out, axis=-1, keepdims=True)
        cen = out - mu
        var = jnp.mean(cen * cen, axis=-1, keepdims=True)
        gate = gate_ref[rows, :].astype(F32)
        o_ref[rows, :] = (_silu(gate) * (cen * lax.rsqrt(var + EPS))).astype(o_ref.dtype)


def retention_group(proj, log_g, cos, sin, *, batch, seq, rows):
    T = proj.shape[0]
    H, dk = RET_HEADS, RET_HEAD_DIM
    nblk = seq // rows

    def col(offset):
        return pl.BlockSpec((rows, dk), lambda b, h, r: (b * nblk + r, offset + h))

    tab = pl.BlockSpec((rows, dk // 2), lambda b, h, r: (r, 0))
    est = 10 * _nbytes((rows, dk), BF16) + 4 * _nbytes((rows, dk // 2), F32) \
        + 4 * _nbytes((RET_BLOCK, dk), F32)
    return pl.pallas_call(
        functools.partial(_retention_kernel, k_scale=dk ** -0.5),
        grid=(batch, H, nblk),
        in_specs=[pl.BlockSpec(memory_space=pltpu.SMEM),
                  col(0), col(H), col(2 * H), col(3 * H), tab, tab],
        out_specs=pl.BlockSpec((rows, dk), lambda b, h, r: (b * nblk + r, h)),
        out_shape=jax.ShapeDtypeStruct((T, H * dk), BF16),
        scratch_shapes=[pltpu.VMEM((dk, dk), F32),
                        pltpu.VMEM((RET_BLOCK, RET_BLOCK), F32),
                        pltpu.VMEM((RET_BLOCK, dk), F32),
                        pltpu.VMEM((RET_BLOCK, dk), F32),
                        pltpu.VMEM((8, dk), F32)],
        compiler_params=pltpu.CompilerParams(
            dimension_semantics=("parallel", "parallel", "arbitrary"),
            vmem_limit_bytes=_vmem_limit(est)),
        name="retention",
    )(log_g, proj, proj, proj, proj, cos, sin)


def _mla_proj_kernel(cq_ref, ckv_ref, kr_ref, gq_ref, gkv_ref, wq_ref, wk_ref, wvt_ref,
                     rc_ref, rn_ref, rp_ref, q_ref, k_ref, vt_ref, cqn_ref, ckvn_ref,
                     *, scale):
    tm = cq_ref.shape[0]
    _rms_rows(cq_ref, gq_ref, cqn_ref, tm)
    _rms_rows(ckv_ref, gkv_ref, ckvn_ref, tm)
    rc = rc_ref[...]
    rn = rn_ref[...]
    rp = rp_ref[...]

    def rope(x):
        return (x * rc + pltpu.roll(x, LANES - MLA_ROPE // 2, 1) * rn
                + pltpu.roll(x, MLA_ROPE // 2, 1) * rp)

    krp = rope(kr_ref[:, :LANES].astype(F32)).astype(k_ref.dtype)
    cqn = cqn_ref[...]
    ckvn = ckvn_ref[...]
    W = MLA_QK_PAD
    for h in range(MLA_HEADS):
        qh = jnp.dot(cqn, wq_ref[:, h * W:(h + 1) * W], preferred_element_type=F32) * scale
        q_ref[:, h * W:h * W + LANES] = qh[:, :LANES].astype(q_ref.dtype)
        q_ref[:, h * W + LANES:(h + 1) * W] = rope(qh[:, LANES:]).astype(q_ref.dtype)
    for c in range(MLA_HEADS // 2):
        kn = jnp.dot(ckvn, wk_ref[:, c * W:(c + 1) * W], preferred_element_type=F32)
        for s in range(2):
            h = 2 * c + s
            k_ref[:, h * W:h * W + LANES] = kn[:, s * LANES:(s + 1) * LANES].astype(k_ref.dtype)
            k_ref[:, h * W + LANES:(h + 1) * W] = krp
    vt_ref[0] = lax.dot_general(wvt_ref[...], ckvn, (((1,), (1,)), ((), ())),
                                preferred_element_type=F32).astype(vt_ref.dtype)


def mla_projections(proj, g_q, g_kv, wq, wk, wvt, rope_c, rope_n, rope_p, *, seq, tm,
                    kv_tile, cq_blk, ckv_blk, kr_blk):
    T = proj.shape[0]
    q_lora = wq.shape[0]
    kv_lora = wk.shape[0]
    nq = wq.shape[1]
    nv = wvt.shape[0]
    nblk = seq // tm
    per_tile = kv_tile // tm
    scale = (MLA_NOPE + MLA_ROPE) ** -0.5 * math.log2(math.e)
    const = lambda i: (0, 0)
    tab = pl.BlockSpec((tm, LANES), lambda i: (i % nblk, 0))
    est = (2 * (_nbytes(wq.shape, BF16) + _nbytes(wk.shape, BF16) + _nbytes(wvt.shape, BF16))
           + 2 * (2 * _nbytes((tm, nq), BF16) + _nbytes((tm, nv), BF16))
           + 3 * _nbytes((tm, q_lora + kv_lora + 2 * LANES), BF16)
           + 6 * _nbytes((tm, LANES), F32) + _nbytes((tm, nv), F32))
    return pl.pallas_call(
        functools.partial(_mla_proj_kernel, scale=scale),
        grid=(T // tm,),
        in_specs=[pl.BlockSpec((tm, q_lora), lambda i: (i, cq_blk)),
                  pl.BlockSpec((tm, kv_lora), lambda i: (i, ckv_blk)),
                  pl.BlockSpec((tm, 2 * LANES), lambda i: (i, kr_blk)),
                  pl.BlockSpec((1, q_lora), const),
                  pl.BlockSpec((1, kv_lora), const),
                  pl.BlockSpec(wq.shape, const),
                  pl.BlockSpec(wk.shape, const),
                  pl.BlockSpec(wvt.shape, const),
                  tab, tab, tab],
        out_specs=[pl.BlockSpec((tm, nq), lambda i: (i, 0)),
                   pl.BlockSpec((tm, nq), lambda i: (i, 0)),
                   pl.BlockSpec((1, nv, tm), lambda i: (i // per_tile, 0, i % per_tile))],
        out_shape=[jax.ShapeDtypeStruct((T, nq), BF16),
                   jax.ShapeDtypeStruct((T, nq), BF16),
                   jax.ShapeDtypeStruct((T // kv_tile, nv, kv_tile), BF16)],
        scratch_shapes=[pltpu.VMEM((tm, q_lora), BF16),
                        pltpu.VMEM((tm, kv_lora), BF16)],
        compiler_params=pltpu.CompilerParams(
            dimension_semantics=("parallel",),
            vmem_limit_bytes=_vmem_limit(est)),
        name="mla_proj",
    )(proj, proj, proj, g_q, g_kv, wq, wk, wvt, rope_c, rope_n, rope_p)


def _mla_attn_kernel(q_ref, k_ref, vt_ref, o_ref, m_ref, l_ref, acc_ref, *, tile, heads):
    qi = pl.program_id(2)
    W, dv = MLA_QK_PAD, MLA_V
    qs = [q_ref[:, h * W:(h + 1) * W] for h in range(heads)]
    m_ref[...] = jnp.full(m_ref.shape, -jnp.inf, F32)
    l_ref[...] = jnp.zeros(l_ref.shape, F32)
    acc_ref[...] = jnp.zeros(acc_ref.shape, F32)

    def process(tiles):
        scores = []
        for j, _ in tiles:
            start = pl.multiple_of(j * tile, tile)
            scores.append([
                lax.dot_general(k_ref[pl.ds(start, tile), h * W:(h + 1) * W], qs[h],
                                (((1,), (1,)), ((), ())), preferred_element_type=F32)
                for h in range(heads)])
        for t, (j, mask) in enumerate(tiles):
            for h in range(heads):
                s = scores[t][h]
                if mask is not None:
                    s = jnp.where(mask, s, -jnp.inf)
                m = m_ref[h]
                m_new = jnp.maximum(m, jnp.max(s, axis=0, keepdims=True))
                alpha = jnp.exp2(m - m_new)
                p = jnp.exp2(s - m_new)
                vt = vt_ref[j, h * dv:(h + 1) * dv, :]
                l_ref[h] = alpha * l_ref[h] + jnp.sum(p, axis=0, keepdims=True)
                acc_ref[h] = alpha * acc_ref[h] + jnp.dot(vt, p.astype(vt.dtype),
                                                         preferred_element_type=F32)
                m_ref[h] = m_new

    key_chunk = lax.broadcasted_iota(jnp.int32, (tile, tile), 0) >> CHUNK_SHIFT
    qry_chunk = lax.broadcasted_iota(jnp.int32, (tile, tile), 1) >> CHUNK_SHIFT
    diag_mask = key_chunk <= qry_chunk

    @pl.when(qi == 0)
    def _():
        process([(qi, diag_mask)])

    @pl.when(qi > 0)
    def _():
        lead = (qi + 1) % 2

        @pl.when(lead == 1)
        def _():
            process([(0, None)])

        def pair(i, carry):
            j = lead + 2 * i
            process([(j, None), (j + 1, None)])
            return carry

        lax.fori_loop(0, (qi + 1 - lead) // 2 - 1, pair, 0)
        process([(qi - 1, None), (qi, diag_mask)])

    for h in range(heads):
        o_ref[:, h * dv:(h + 1) * dv] = (acc_ref[h] * (1.0 / l_ref[h])).T.astype(o_ref.dtype)


def mla_attention(qc, kc, vt, *, batch, seq, tile, heads):
    T = qc.shape[0]
    H, W, dv = MLA_HEADS, MLA_QK_PAD, MLA_V
    nq = seq // tile
    est = (2 * heads * (_nbytes((seq, W), BF16) + _nbytes((seq, dv), BF16)
                        + _nbytes((tile, W), BF16) + _nbytes((tile, dv), BF16))
           + 4 * heads * _nbytes((tile, tile), F32))
    return pl.pallas_call(
        functools.partial(_mla_attn_kernel, tile=tile, heads=heads),
        grid=(batch, H // heads, nq),
        in_specs=[pl.BlockSpec((tile, heads * W), lambda b, h, i: (b * nq + i, h)),
                  pl.BlockSpec((seq, heads * W), lambda b, h, i: (b, h)),
                  pl.BlockSpec((nq, heads * dv, tile), lambda b, h, i: (b, h, 0))],
        out_specs=pl.BlockSpec((tile, heads * dv), lambda b, h, i: (b * nq + i, h)),
        out_shape=jax.ShapeDtypeStruct((T, H * dv), BF16),
        scratch_shapes=[pltpu.VMEM((heads, 1, tile), F32),
                        pltpu.VMEM((heads, 1, tile), F32),
                        pltpu.VMEM((heads, dv, tile), F32)],
        compiler_params=pltpu.CompilerParams(
            dimension_semantics=("parallel", "parallel", "arbitrary"),
            vmem_limit_bytes=_vmem_limit(est)),
        name="mla_attn",
    )(qc, kc, vt)


def _emit_residual(h, g_ref, o_ref, hg_ref, ssq_ref):
    o_ref[...] = h
    hg_ref[...] = (h * g_ref[...]).astype(hg_ref.dtype)
    part = jnp.sum(h * h, axis=-1, keepdims=True)

    @pl.when(pl.program_id(1) == 0)
    def _():
        ssq_ref[...] = part

    @pl.when(pl.program_id(1) > 0)
    def _():
        ssq_ref[...] += part


def _residual_out(T, N, tm, tn):
    specs = [pl.BlockSpec((tm, tn), lambda i, j: (i, j)),
             pl.BlockSpec((tm, tn), lambda i, j: (i, j)),
             pl.BlockSpec((tm, 1), lambda i, j: (i, 0))]
    shapes = [jax.ShapeDtypeStruct((T, N), F32), jax.ShapeDtypeStruct((T, N), BF16),
              jax.ShapeDtypeStruct((T, 1), F32)]
    return specs, shapes


def _out_proj_kernel(ro_ref, mo_ref, wr_ref, wm_ref, x_ref, g_ref, o_ref, hg_ref, ssq_ref):
    acc = jnp.dot(ro_ref[...], wr_ref[...], preferred_element_type=F32)
    acc = acc + jnp.dot(mo_ref[...], wm_ref[...], preferred_element_type=F32)
    _emit_residual(x_ref[...] + acc, g_ref, o_ref, hg_ref, ssq_ref)


def out_projection(ro, mo, w_o, x, g_next, *, tm, tn):
    T, kr = ro.shape
    km = mo.shape[1]
    assert kr == km
    N = w_o.shape[1]
    est = (4 * _nbytes((tm, kr), BF16) + 4 * _nbytes((kr, tn), BF16)
           + 7 * _nbytes((tm, tn), F32))
    out_specs, out_shape = _residual_out(T, N, tm, tn)
    return pl.pallas_call(
        _out_proj_kernel,
        grid=(T // tm, N // tn),
        in_specs=[pl.BlockSpec((tm, kr), lambda i, j: (i, 0)),
                  pl.BlockSpec((tm, km), lambda i, j: (i, 0)),
                  pl.BlockSpec((kr, tn), lambda i, j: (0, j)),
                  pl.BlockSpec((km, tn), lambda i, j: (1, j)),
                  pl.BlockSpec((tm, tn), lambda i, j: (i, j)),
                  pl.BlockSpec((1, tn), lambda i, j: (0, j))],
        out_specs=out_specs,
        out_shape=out_shape,
        compiler_params=pltpu.CompilerParams(
            dimension_semantics=("parallel", "arbitrary"),
            vmem_limit_bytes=_vmem_limit(est)),
        name="out_proj",
    )(ro, mo, w_o, w_o, x, g_next)


def _ffn_up_kernel(hg_ref, ssq_ref, wg_ref, wu_ref, cw_ref, cb_ref, o_ref, g_ref,
                   *, tiles_per_seq):
    tm = hg_ref.shape[0]
    halo = CONV_HALO

    @pl.when(pl.program_id(1) % tiles_per_seq == 0)
    def _():
        g_ref[0:halo, :] = jnp.zeros((halo, g_ref.shape[1]), F32)

    hg = hg_ref[...]
    r = _row_scale(ssq_ref, hg_ref.shape[1])
    g_ref[halo:halo + tm, :] = jnp.dot(hg, wg_ref[...], preferred_element_type=F32) * r
    up = jnp.dot(hg, wu_ref[...], preferred_element_type=F32) * r
    a = cb_ref[...]
    for j in range(CONV_WIDTH):
        shift = CONV_WIDTH - 1 - j
        a = a + g_ref[halo - shift:halo - shift + tm, :] * cw_ref[j:j + 1, :]
    o_ref[...] = (_silu(a) * up).astype(o_ref.dtype)
    g_ref[0:halo, :] = g_ref[tm:tm + halo, :]


def ffn_up(hg, ssq, w_gate, w_up, conv_w, conv_b, *, seq, tm, tf, col_start, n_cols, name):
    T, K = hg.shape
    assert n_cols % tf == 0 and col_start % tf == 0
    first = col_start // tf
    est = (2 * _nbytes((tm, K), BF16) + 4 * _nbytes((K, tf), BF16)
           + 2 * _nbytes((tm, tf), BF16) + 4 * _nbytes((tm + CONV_HALO, tf), F32)
           + 2 * _nbytes((tm, LANES), F32))
    return pl.pallas_call(
        functools.partial(_ffn_up_kernel, tiles_per_seq=seq // tm),
        grid=(n_cols // tf, T // tm),
        in_specs=[pl.BlockSpec((tm, K), lambda j, i: (i, 0)),
                  pl.BlockSpec((tm, 1), lambda j, i: (i, 0)),
                  pl.BlockSpec((K, tf), lambda j, i: (0, first + j)),
                  pl.BlockSpec((K, tf), lambda j, i: (0, first + j)),
                  pl.BlockSpec((CONV_WIDTH, tf), lambda j, i: (0, first + j)),
                  pl.BlockSpec((1, tf), lambda j, i: (0, first + j))],
        out_specs=pl.BlockSpec((tm, tf), lambda j, i: (i, j)),
        out_shape=jax.ShapeDtypeStruct((T, n_cols), BF16),
        scratch_shapes=[pltpu.VMEM((tm + CONV_HALO, tf), F32)],
        compiler_params=pltpu.CompilerParams(
            dimension_semantics=("arbitrary", "arbitrary"),
            vmem_limit_bytes=_vmem_limit(est)),
        name=name,
    )(hg, ssq, w_gate, w_up, conv_w, conv_b)


def _ffn_down_kernel(a_ref, b_ref, wa_ref, wb_ref, h_ref, g_ref, o_ref, hg_ref, ssq_ref):
    acc = jnp.dot(a_ref[...], wa_ref[...], preferred_element_type=F32)
    acc = acc + jnp.dot(b_ref[...], wb_ref[...], preferred_element_type=F32)
    _emit_residual(h_ref[...] + acc, g_ref, o_ref, hg_ref, ssq_ref)


def ffn_down(hidden_main, hidden_tail, w_down, h, g_next, *, tm, tn):
    T, ka = hidden_main.shape
    kb = hidden_tail.shape[1]
    assert ka % kb == 0 and ka + kb == w_down.shape[0]
    N = w_down.shape[1]
    est = (2 * _nbytes((tm, ka + kb), BF16) + 2 * _nbytes((ka + kb, tn), BF16)
           + 7 * _nbytes((tm, tn), F32))
    out_specs, out_shape = _residual_out(T, N, tm, tn)
    return pl.pallas_call(
        _ffn_down_kernel,
        grid=(T // tm, N // tn),
        in_specs=[pl.BlockSpec((tm, ka), lambda i, j: (i, 0)),
                  pl.BlockSpec((tm, kb), lambda i, j: (i, 0)),
                  pl.BlockSpec((ka, tn), lambda i, j: (0, j)),
                  pl.BlockSpec((kb, tn), lambda i, j: (ka // kb, j)),
                  pl.BlockSpec((tm, tn), lambda i, j: (i, j)),
                  pl.BlockSpec((1, tn), lambda i, j: (0, j))],
        out_specs=out_specs,
        out_shape=out_shape,
        compiler_params=pltpu.CompilerParams(
            dimension_semantics=("parallel", "arbitrary"),
            vmem_limit_bytes=_vmem_limit(est)),
        name="ffn_down",
    )(hidden_main, hidden_tail, w_down, w_down, h, g_next)


def _ple_kernel(hg_ref, ssq_ref, h_ref, wg_ref, p_ref, wp_ref, gf_ref, o_ref, ssq3_ref):
    j = pl.program_id(1)
    tm, D = hg_ref.shape
    tn = wg_ref.shape[1]

    z = jnp.dot(hg_ref[...], wg_ref[...], preferred_element_type=F32) * _row_scale(ssq_ref, D)
    gate = 1.0 / (1.0 + jnp.exp(-z))
    emb = jnp.dot(p_ref[...].astype(BF16), wp_ref[...], preferred_element_type=F32)
    h3 = h_ref[...] + gate * emb
    part = jnp.sum(h3 * h3, axis=-1, keepdims=True)

    @pl.when(j == 0)
    def _():
        ssq3_ref[...] = part

    @pl.when(j > 0)
    def _():
        ssq3_ref[...] += part

    for jj in range(D // tn):
        @pl.when(j == jj)
        def _(jj=jj):
            o_ref[:, jj * tn:(jj + 1) * tn] = h3

    @pl.when(j == pl.num_programs(1) - 1)
    def _():
        o_ref[...] = (o_ref[...] * _row_scale(ssq3_ref, D)) * gf_ref[...]


def ple_final(hg, ssq, h, w_gate, p, w_proj, g_final, *, tm, tn):
    T, D = h.shape
    P = p.shape[1]
    est = (2 * _nbytes((tm, D), F32) + 2 * _nbytes((tm, D), BF16)
           + 2 * _nbytes((D, tn), BF16) + 2 * _nbytes((P, tn), BF16)
           + 2 * _nbytes((tm, P), F32) + 6 * _nbytes((tm, tn), F32))
    return pl.pallas_call(
        _ple_kernel,
        grid=(T // tm, D // tn),
        in_specs=[pl.BlockSpec((tm, D), lambda i, j: (i, 0)),
                  pl.BlockSpec((tm, 1), lambda i, j: (i, 0)),
                  pl.BlockSpec((tm, tn), lambda i, j: (i, j)),
                  pl.BlockSpec((D, tn), lambda i, j: (0, j)),
                  pl.BlockSpec((tm, P), lambda i, j: (i, 0)),
                  pl.BlockSpec((P, tn), lambda i, j: (0, j)),
                  pl.BlockSpec((1, D), lambda i, j: (0, 0))],
        out_specs=pl.BlockSpec((tm, D), lambda i, j: (i, 0)),
        out_shape=jax.ShapeDtypeStruct((T, D), F32),
        scratch_shapes=[pltpu.VMEM((tm, 1), F32)],
        compiler_params=pltpu.CompilerParams(
            dimension_semantics=("parallel", "arbitrary"),
            vmem_limit_bytes=_vmem_limit(est)),
        name="ple_final",
    )(hg, ssq, h, w_gate, p, w_proj, g_final)


def _rope_tables(seq, dim):
    inv = 1.0 / (ROPE_BASE ** (jnp.arange(0, dim, 2, dtype=F32) / dim))
    ang = jnp.arange(seq, dtype=F32)[:, None] * inv[None, :]
    return jnp.cos(ang), jnp.sin(ang)


def _tile_config(seq):
    return dict(
        prenorm_tm=min(1024, seq),
        in_proj=dict(tm=min(1024, seq), tn=768),
        retention_rows=min(4096, seq),
        mla_proj_tm=min(512, seq),
        attn_tile=min(512, seq),
        attn_heads=4,
        out_proj=dict(tm=min(1024, seq), tn=512),
        ffn_up=dict(tm=min(1024, seq), tf=512),
        ffn_down=dict(tm=min(512, seq), tn=512),
        ple=dict(tm=min(512, seq), tn=512),
    )


def _layer(h, p_i, w_in, g_attn, g_q_lora, g_kv_lora, w_uq, w_ukv, w_o, g_ffn,
           w_ffn_gate, w_ffn_up, conv_w, conv_b, w_ffn_down, g_ple, w_ple_gate,
           w_ple_proj, g_out, *, batch, seq):
    cfg = _tile_config(seq)
    D = h.shape[1]
    ret_w = RET_HEADS * RET_HEAD_DIM
    q_lora = w_uq.shape[0]
    kv_lora = w_ukv.shape[0]
    d_ff = w_ffn_gate.shape[1]

    in_w = w_in.shape[1]
    tn_in = cfg["in_proj"]["tn"]
    in_pad = pl.cdiv(in_w, tn_in) * tn_in
    xg, ssq0 = prenorm(h, g_attn.reshape(1, D), tm=cfg["prenorm_tm"])
    proj = in_projection(xg, ssq0, w_in.T, n_out=in_pad, out_dtype=BF16, **cfg["in_proj"])

    log_g = jnp.log1p(-jnp.exp2(-5.0 - jnp.arange(RET_HEADS, dtype=F32)))
    cos_r, sin_r = _rope_tables(seq, RET_HEAD_DIM)
    ro = retention_group(proj, log_g, cos_r, sin_r, batch=batch, seq=seq,
                         rows=cfg["retention_rows"])

    hq = MLA_NOPE + MLA_ROPE
    wq = w_uq.reshape(q_lora, MLA_HEADS, hq)
    wq = jnp.pad(wq, ((0, 0), (0, 0), (0, MLA_QK_PAD - hq)))
    wq = wq.reshape(q_lora, MLA_HEADS * MLA_QK_PAD).astype(BF16)
    wkv = w_ukv.reshape(kv_lora, MLA_HEADS, MLA_NOPE + MLA_V)
    wk = wkv[:, :, :MLA_NOPE].reshape(kv_lora, MLA_HEADS * MLA_NOPE).astype(BF16)
    wvt = wkv[:, :, MLA_NOPE:].reshape(kv_lora, MLA_HEADS * MLA_V).T.astype(BF16)
    cos_m, sin_m = _rope_tables(seq, MLA_ROPE)
    zeros = jnp.zeros_like(cos_m)
    fill = jnp.zeros((seq, LANES - MLA_ROPE), F32)
    rope_c = jnp.concatenate([cos_m, cos_m, fill], axis=1)
    rope_n = jnp.concatenate([-sin_m, zeros, fill], axis=1)
    rope_p = jnp.concatenate([zeros, sin_m, fill], axis=1)
    cq_off = 4 * ret_w
    ckv_off = cq_off + q_lora
    kr_off = ckv_off + kv_lora
    assert cq_off % q_lora == 0 and ckv_off % kv_lora == 0 and kr_off % (2 * LANES) == 0
    assert in_pad - kr_off >= 2 * LANES
    qc, kc, vt = mla_projections(
        proj, g_q_lora.reshape(1, q_lora), g_kv_lora.reshape(1, kv_lora), wq, wk, wvt,
        rope_c, rope_n, rope_p, seq=seq, tm=cfg["mla_proj_tm"], kv_tile=cfg["attn_tile"],
        cq_blk=cq_off // q_lora, ckv_blk=ckv_off // kv_lora, kr_blk=kr_off // (2 * LANES))
    mo = mla_attention(qc, kc, vt, batch=batch, seq=seq, tile=cfg["attn_tile"],
                       heads=cfg["attn_heads"])

    h1, hg1, ssq1 = out_projection(ro, mo, w_o.astype(BF16), h, g_ffn.reshape(1, D),
                                   **cfg["out_proj"])

    up_args = (hg1, ssq1, w_ffn_gate.astype(BF16), w_ffn_up.astype(BF16), conv_w,
               conv_b.reshape(1, d_ff))
    tf = cfg["ffn_up"]["tf"]
    n_main = d_ff // tf * tf
    hidden_main = ffn_up(*up_args, seq=seq, tm=cfg["ffn_up"]["tm"], tf=tf, col_start=0,
                         n_cols=n_main, name="ffn_up")
    hidden_tail = ffn_up(*up_args, seq=seq, tm=cfg["ffn_up"]["tm"], tf=d_ff - n_main,
                         col_start=n_main, n_cols=d_ff - n_main, name="ffn_up_tail")
    h2, hg2, ssq2 = ffn_down(hidden_main, hidden_tail, w_ffn_down.astype(BF16), h1,
                             g_ple.reshape(1, D), **cfg["ffn_down"])

    return ple_final(hg2, ssq2, h2, w_ple_gate.astype(BF16), p_i, w_ple_proj.astype(BF16),
                     g_out.reshape(1, D), **cfg["ple"])


def kernel(x, p, w_in, g_attn, g_q_lora, g_kv_lora, w_uq, w_ukv, w_o, g_ffn, w_ffn_gate,
           w_ffn_up, conv_w, conv_b, w_ffn_down, g_ple, w_ple_gate, w_ple_proj, g_final):
    B, S, D = x.shape
    depth = p.shape[0]
    assert depth == 1, "the final RMSNorm is fused into the layer's last kernel"
    h = x.reshape(B * S, D)
    out = _layer(h, p[0].reshape(B * S, -1), w_in[0], g_attn[0], g_q_lora[0], g_kv_lora[0],
                 w_uq[0], w_ukv[0], w_o[0], g_ffn[0], w_ffn_gate[0], w_ffn_up[0], conv_w[0],
                 conv_b[0], w_ffn_down[0], g_ple[0], w_ple_gate[0], w_ple_proj[0], g_final,
                 batch=B, seq=S)
    return out.reshape(B, S, D)
```

```python
import functools
import math

import jax
import jax.numpy as jnp
from jax import lax
from jax.experimental import pallas as pl
from jax.experimental.pallas import tpu as pltpu

F32 = jnp.float32
BF16 = jnp.bfloat16

EPS = 1e-6
ROPE_BASE = 10000.0
CHUNK = 64
CHUNK_SHIFT = 6
RET_HEADS = 8
RET_HEAD_DIM = 256
MLA_HEADS = 16
MLA_NOPE = 128
MLA_ROPE = 64
MLA_V = 128
MLA_QK_PAD = 256
CONV_WIDTH = 3
RET_BLOCK = 256
LANES = 128
CONV_HALO = 8
MIB = 1024 * 1024
VMEM_LIMIT_CAP = 60 * MIB


def _vmem_limit(nbytes):
    return int(min(VMEM_LIMIT_CAP, nbytes + 16 * MIB))


def _nbytes(shape, dtype):
    return math.prod(shape) * jnp.dtype(dtype).itemsize


def _rms_rows(x_ref, g_ref, o_ref, n_rows, row_chunk=16, unroll=4):
    g = g_ref[...]

    def body(c, carry):
        r = pl.multiple_of(c * row_chunk, row_chunk)
        x = x_ref[pl.ds(r, row_chunk), :].astype(F32)
        ms = jnp.mean(x * x, axis=-1, keepdims=True)
        o_ref[pl.ds(r, row_chunk), :] = ((x * lax.rsqrt(ms + EPS)) * g).astype(o_ref.dtype)
        return carry

    lax.fori_loop(0, n_rows // row_chunk, body, 0, unroll=unroll)


def _row_scale(ssq_ref, width):
    return lax.rsqrt(ssq_ref[...] * (1.0 / width) + EPS)


def _silu(x):
    return x * (1.0 / (1.0 + jnp.exp(-x)))


def _prenorm_kernel(x_ref, g_ref, xg_ref, ssq_ref, *, row_chunk):
    g = g_ref[...]

    def body(c, carry):
        rows = pl.ds(pl.multiple_of(c * row_chunk, row_chunk), row_chunk)
        x = x_ref[rows, :]
        ssq_ref[rows, :] = jnp.sum(x * x, axis=-1, keepdims=True)
        xg_ref[rows, :] = (x * g).astype(xg_ref.dtype)
        return carry

    lax.fori_loop(0, x_ref.shape[0] // row_chunk, body, 0, unroll=4)


def prenorm(x, g, *, tm):
    T, K = x.shape
    est = 2 * _nbytes((tm, K), F32) + 2 * _nbytes((tm, K), BF16)
    return pl.pallas_call(
        functools.partial(_prenorm_kernel, row_chunk=16),
        grid=(T // tm,),
        in_specs=[pl.BlockSpec((tm, K), lambda i: (i, 0)),
                  pl.BlockSpec((1, K), lambda i: (0, 0))],
        out_specs=[pl.BlockSpec((tm, K), lambda i: (i, 0)),
                   pl.BlockSpec((tm, 1), lambda i: (i, 0))],
        out_shape=[jax.ShapeDtypeStruct((T, K), BF16), jax.ShapeDtypeStruct((T, 1), F32)],
        compiler_params=pltpu.CompilerParams(
            dimension_semantics=("parallel",),
            vmem_limit_bytes=_vmem_limit(est)),
        name="prenorm",
    )(x, g)


def _in_proj_kernel(xg_ref, ssq_ref, wt_ref, o_ref, wb_ref, *, edge_rows):
    tn, K = wt_ref.shape

    @pl.when(pl.program_id(1) == 0)
    def _():
        w = wt_ref[...]
        if edge_rows:
            last = pl.program_id(0) == pl.num_programs(0) - 1
            valid = jnp.where(last, edge_rows, tn)
            row = lax.broadcasted_iota(jnp.int32, (tn, 1), 0)
            w = jnp.where(row < valid, w, 0.0)
        wb_ref[...] = w.astype(wb_ref.dtype)

    res = lax.dot_general(xg_ref[...], wb_ref[...], (((1,), (1,)), ((), ())),
                          preferred_element_type=F32)
    o_ref[...] = (res * _row_scale(ssq_ref, K)).astype(o_ref.dtype)


def in_projection(xg, ssq, w_t, *, n_out, tm, tn, out_dtype):
    T, K = xg.shape
    N = n_out
    assert N % tn == 0 and N - tn < w_t.shape[0] <= N
    est = (2 * _nbytes((tm, K), BF16) + 2 * _nbytes((tn, K), w_t.dtype)
           + _nbytes((tn, K), BF16) + 2 * _nbytes((tm, tn), out_dtype)
           + _nbytes((tm, tn), F32))
    return pl.pallas_call(
        functools.partial(_in_proj_kernel, edge_rows=w_t.shape[0] % tn),
        grid=(N // tn, T // tm),
        in_specs=[pl.BlockSpec((tm, K), lambda j, i: (i, 0)),
                  pl.BlockSpec((tm, 1), lambda j, i: (i, 0)),
                  pl.BlockSpec((tn, K), lambda j, i: (j, 0))],
        out_specs=pl.BlockSpec((tm, tn), lambda j, i: (i, j)),
        out_shape=jax.ShapeDtypeStruct((T, N), out_dtype),
        scratch_shapes=[pltpu.VMEM((tn, K), BF16)],
        compiler_params=pltpu.CompilerParams(
            dimension_semantics=("arbitrary", "arbitrary"),
            vmem_limit_bytes=_vmem_limit(est)),
        name="in_proj",
    )(xg, ssq, w_t)


def _retention_kernel(lg_ref, q_ref, k_ref, v_ref, gate_ref, cos_ref, sin_ref, o_ref,
                      state_ref, dmat_ref, qdec_ref, kdec_ref, sdec_ref, *, k_scale):
    L = RET_BLOCK
    dk = q_ref.shape[1]
    half = dk // 2
    lg = lg_ref[pl.program_id(1)]

    @pl.when(pl.program_id(2) == 0)
    def _init():
        state_ref[...] = jnp.zeros_like(state_ref)
        n = lax.broadcasted_iota(jnp.int32, (L, L), 0)
        m = lax.broadcasted_iota(jnp.int32, (L, L), 1)
        cn = n >> CHUNK_SHIFT
        cm = m >> CHUNK_SHIFT
        d = (n - m).astype(F32)
        expo = jnp.where(cn == cm, jnp.abs(d), d)
        visible = cm <= cn
        dmat_ref[...] = jnp.where(visible, jnp.exp(lg * jnp.where(visible, expo, 0.0)), 0.0)
        row = lax.broadcasted_iota(jnp.int32, (L, dk), 0).astype(F32)
        qdec_ref[...] = jnp.exp(lg * (row + 1.0))
        kdec_ref[...] = jnp.exp(lg * (float(L - 1) - row))
        sdec_ref[...] = jnp.exp(jnp.full(sdec_ref.shape, lg * float(L), F32))

    def rope(x, cos, sin):
        x1 = x[:, :half]
        x2 = x[:, half:]
        return jnp.concatenate([x1 * cos - x2 * sin, x2 * cos + x1 * sin], axis=1)

    for sub in range(q_ref.shape[0] // L):
        rows = pl.ds(sub * L, L)
        cos = cos_ref[rows, :]
        sin = sin_ref[rows, :]
        q = rope(q_ref[rows, :].astype(F32), cos, sin)
        k = rope(k_ref[rows, :].astype(F32), cos, sin) * k_scale
        v = v_ref[rows, :]
        qb = q.astype(BF16)
        kb = k.astype(BF16)
        scores = lax.dot_general(qb, kb, (((1,), (1,)), ((), ())),
                                 preferred_element_type=F32)
        state = state_ref[...]
        inter = jnp.dot((q * qdec_ref[...]).astype(BF16), state.astype(BF16),
                        preferred_element_type=F32)
        kd = (k * kdec_ref[...]).astype(BF16)
        state_ref[...] = state * sdec_ref[0:1, :] + lax.dot_general(
            kd, v, (((0,), (0,)), ((), ())), preferred_element_type=F32)
        scores = (scores * dmat_ref[...]).astype(BF16)
        out = jnp.dot(scores, v, preferred_element_type=F32) + inter
        mu = jnp.mean(out, axis=-1, keepdims=True)
        cen = out - mu
        var = jnp.mean(cen * cen, axis=-1, keepdims=True)
        gate = gate_ref[rows, :].astype(F32)
        o_ref[rows, :] = (_silu(gate) * (cen * lax.rsqrt(var + EPS))).astype(o_ref.dtype)


def retention_group(proj, log_g, cos, sin, *, batch, seq, rows):
    T = proj.shape[0]
    H, dk = RET_HEADS, RET_HEAD_DIM
    nblk = seq // rows

    def col(offset):
        return pl.BlockSpec((rows, dk), lambda b, h, r: (b * nblk + r, offset + h))

    tab = pl.BlockSpec((rows, dk // 2), lambda b, h, r: (r, 0))
    est = 10 * _nbytes((rows, dk), BF16) + 4 * _nbytes((rows, dk // 2), F32) \
        + 4 * _nbytes((RET_BLOCK, dk), F32)
    return pl.pallas_call(
        functools.partial(_retention_kernel, k_scale=dk ** -0.5),
        grid=(batch, H, nblk),
        in_specs=[pl.BlockSpec(memory_space=pltpu.SMEM),
                  col(0), col(H), col(2 * H), col(3 * H), tab, tab],
        out_specs=pl.BlockSpec((rows, dk), lambda b, h, r: (b * nblk + r, h)),
        out_shape=jax.ShapeDtypeStruct((T, H * dk), BF16),
        scratch_shapes=[pltpu.VMEM((dk, dk), F32),
                        pltpu.VMEM((RET_BLOCK, RET_BLOCK), F32),
                        pltpu.VMEM((RET_BLOCK, dk), F32),
                        pltpu.VMEM((RET_BLOCK, dk), F32),
                        pltpu.VMEM((8, dk), F32)],
        compiler_params=pltpu.CompilerParams(
            dimension_semantics=("parallel", "parallel", "arbitrary"),
            vmem_limit_bytes=_vmem_limit(est)),
        name="retention",
    )(log_g, proj, proj, proj, proj, cos, sin)


def _mla_proj_kernel(cq_ref, ckv_ref, kr_ref, gq_ref, gkv_ref, wq_ref, wk_ref, wvt_ref,
                     rc_ref, rn_ref, rp_ref, q_ref, k_ref, vt_ref, cqn_ref, ckvn_ref,
                     *, scale):
    tm = cq_ref.shape[0]
    _rms_rows(cq_ref, gq_ref, cqn_ref, tm)
    _rms_rows(ckv_ref, gkv_ref, ckvn_ref, tm)
    rc = rc_ref[...]
    rn = rn_ref[...]
    rp = rp_ref[...]

    def rope(x):
        return (x * rc + pltpu.roll(x, LANES - MLA_ROPE // 2, 1) * rn
                + pltpu.roll(x, MLA_ROPE // 2, 1) * rp)

    krp = rope(kr_ref[:, :LANES].astype(F32)).astype(k_ref.dtype)
    cqn = cqn_ref[...]
    ckvn = ckvn_ref[...]
    W = MLA_QK_PAD
    for h in range(MLA_HEADS):
        qh = jnp.dot(cqn, wq_ref[:, h * W:(h + 1) * W], preferred_element_type=F32) * scale
        q_ref[:, h * W:h * W + LANES] = qh[:, :LANES].astype(q_ref.dtype)
        q_ref[:, h * W + LANES:(h + 1) * W] = rope(qh[:, LANES:]).astype(q_ref.dtype)
    for c in range(MLA_HEADS // 2):
        kn = jnp.dot(ckvn, wk_ref[:, c * W:(c + 1) * W], preferred_element_type=F32)
        for s in range(2):
            h = 2 * c + s
            k_ref[:, h * W:h * W + LANES] = kn[:, s * LANES:(s + 1) * LANES].astype(k_ref.dtype)
            k_ref[:, h * W + LANES:(h + 1) * W] = krp
    vt_ref[0] = lax.dot_general(wvt_ref[...], ckvn, (((1,), (1,)), ((), ())),
                                preferred_element_type=F32).astype(vt_ref.dtype)


def mla_projections(proj, g_q, g_kv, wq, wk, wvt, rope_c, rope_n, rope_p, *, seq, tm,
                    kv_tile, cq_blk, ckv_blk, kr_blk):
    T = proj.shape[0]
    q_lora = wq.shape[0]
    kv_lora = wk.shape[0]
    nq = wq.shape[1]
    nv = wvt.shape[0]
    nblk = seq // tm
    per_tile = kv_tile // tm
    scale = (MLA_NOPE + MLA_ROPE) ** -0.5 * math.log2(math.e)
    const = lambda i: (0, 0)
    tab = pl.BlockSpec((tm, LANES), lambda i: (i % nblk, 0))
    est = (2 * (_nbytes(wq.shape, BF16) + _nbytes(wk.shape, BF16) + _nbytes(wvt.shape, BF16))
           + 2 * (2 * _nbytes((tm, nq), BF16) + _nbytes((tm, nv), BF16))
           + 3 * _nbytes((tm, q_lora + kv_lora + 2 * LANES), BF16)
           + 6 * _nbytes((tm, LANES), F32) + _nbytes((tm, nv), F32))
    return pl.pallas_call(
        functools.partial(_mla_proj_kernel, scale=scale),
        grid=(T // tm,),
        in_specs=[pl.BlockSpec((tm, q_lora), lambda i: (i, cq_blk)),
                  pl.BlockSpec((tm, kv_lora), lambda i: (i, ckv_blk)),
                  pl.BlockSpec((tm, 2 * LANES), lambda i: (i, kr_blk)),
                  pl.BlockSpec((1, q_lora), const),
                  pl.BlockSpec((1, kv_lora), const),
                  pl.BlockSpec(wq.shape, const),
                  pl.BlockSpec(wk.shape, const),
                  pl.BlockSpec(wvt.shape, const),
                  tab, tab, tab],
        out_specs=[pl.BlockSpec((tm, nq), lambda i: (i, 0)),
                   pl.BlockSpec((tm, nq), lambda i: (i, 0)),
                   pl.BlockSpec((1, nv, tm), lambda i: (i // per_tile, 0, i % per_tile))],
        out_shape=[jax.ShapeDtypeStruct((T, nq), BF16),
                   jax.ShapeDtypeStruct((T, nq), BF16),
                   jax.ShapeDtypeStruct((T // kv_tile, nv, kv_tile), BF16)],
        scratch_shapes=[pltpu.VMEM((tm, q_lora), BF16),
                        pltpu.VMEM((tm, kv_lora), BF16)],
        compiler_params=pltpu.CompilerParams(
            dimension_semantics=("parallel",),
            vmem_limit_bytes=_vmem_limit(est)),
        name="mla_proj",
    )(proj, proj, proj, g_q, g_kv, wq, wk, wvt, rope_c, rope_n, rope_p)


def _mla_attn_kernel(q_ref, k_ref, vt_ref, o_ref, m_ref, l_ref, acc_ref, *, tile, heads):
    qi = pl.program_id(2)
    W, dv = MLA_QK_PAD, MLA_V
    qs = [q_ref[:, h * W:(h + 1) * W] for h in range(heads)]
    m_ref[...] = jnp.full(m_ref.shape, -jnp.inf, F32)
    l_ref[...] = jnp.zeros(l_ref.shape, F32)
    acc_ref[...] = jnp.zeros(acc_ref.shape, F32)

    def process(tiles):
        scores = []
        for j, _ in tiles:
            start = pl.multiple_of(j * tile, tile)
            scores.append([
                lax.dot_general(k_ref[pl.ds(start, tile), h * W:(h + 1) * W], qs[h],
                                (((1,), (1,)), ((), ())), preferred_element_type=F32)
                for h in range(heads)])
        for t, (j, mask) in enumerate(tiles):
            for h in range(heads):
                s = scores[t][h]
                if mask is not None:
                    s = jnp.where(mask, s, -jnp.inf)
                m = m_ref[h]
                m_new = jnp.maximum(m, jnp.max(s, axis=0, keepdims=True))
                alpha = jnp.exp2(m - m_new)
                p = jnp.exp2(s - m_new)
                vt = vt_ref[j, h * dv:(h + 1) * dv, :]
                l_ref[h] = alpha * l_ref[h] + jnp.sum(p, axis=0, keepdims=True)
                acc_ref[h] = alpha * acc_ref[h] + jnp.dot(vt, p.astype(vt.dtype),
                                                         preferred_element_type=F32)
                m_ref[h] = m_new

    key_chunk = lax.broadcasted_iota(jnp.int32, (tile, tile), 0) >> CHUNK_SHIFT
    qry_chunk = lax.broadcasted_iota(jnp.int32, (tile, tile), 1) >> CHUNK_SHIFT
    diag_mask = key_chunk <= qry_chunk

    @pl.when(qi == 0)
    def _():
        process([(qi, diag_mask)])

    @pl.when(qi > 0)
    def _():
        lead = (qi + 1) % 2

        @pl.when(lead == 1)
        def _():
            process([(0, None)])

        def pair(i, carry):
            j = lead + 2 * i
            process([(j, None), (j + 1, None)])
            return carry

        lax.fori_loop(0, (qi + 1 - lead) // 2 - 1, pair, 0)
        process([(qi - 1, None), (qi, diag_mask)])

    for h in range(heads):
        o_ref[:, h * dv:(h + 1) * dv] = (acc_ref[h] * (1.0 / l_ref[h])).T.astype(o_ref.dtype)


def mla_attention(qc, kc, vt, *, batch, seq, tile, heads):
    T = qc.shape[0]
    H, W, dv = MLA_HEADS, MLA_QK_PAD, MLA_V
    nq = seq // tile
    est = (2 * heads * (_nbytes((seq, W), BF16) + _nbytes((seq, dv), BF16)
                        + _nbytes((tile, W), BF16) + _nbytes((tile, dv), BF16))
           + 4 * heads * _nbytes((tile, tile), F32))
    return pl.pallas_call(
        functools.partial(_mla_attn_kernel, tile=tile, heads=heads),
        grid=(batch, H // heads, nq),
        in_specs=[pl.BlockSpec((tile, heads * W), lambda b, h, i: (b * nq + i, h)),
                  pl.BlockSpec((seq, heads * W), lambda b, h, i: (b, h)),
                  pl.BlockSpec((nq, heads * dv, tile), lambda b, h, i: (b, h, 0))],
        out_specs=pl.BlockSpec((tile, heads * dv), lambda b, h, i: (b * nq + i, h)),
        out_shape=jax.ShapeDtypeStruct((T, H * dv), BF16),
        scratch_shapes=[pltpu.VMEM((heads, 1, tile), F32),
                        pltpu.VMEM((heads, 1, tile), F32),
                        pltpu.VMEM((heads, dv, tile), F32)],
        compiler_params=pltpu.CompilerParams(
            dimension_semantics=("parallel", "parallel", "arbitrary"),
            vmem_limit_bytes=_vmem_limit(est)),
        name="mla_attn",
    )(qc, kc, vt)


def _emit_residual(h, g_ref, o_ref, hg_ref, ssq_ref):
    o_ref[...] = h
    hg_ref[...] = (h * g_ref[...]).astype(hg_ref.dtype)
    part = jnp.sum(h * h, axis=-1, keepdims=True)

    @pl.when(pl.program_id(1) == 0)
    def _():
        ssq_ref[...] = part

    @pl.when(pl.program_id(1) > 0)
    def _():
        ssq_ref[...] += part


def _residual_out(T, N, tm, tn):
    specs = [pl.BlockSpec((tm, tn), lambda i, j: (i, j)),
             pl.BlockSpec((tm, tn), lambda i, j: (i, j)),
             pl.BlockSpec((tm, 1), lambda i, j: (i, 0))]
    shapes = [jax.ShapeDtypeStruct((T, N), F32), jax.ShapeDtypeStruct((T, N), BF16),
              jax.ShapeDtypeStruct((T, 1), F32)]
    return specs, shapes


def _out_proj_kernel(ro_ref, mo_ref, wr_ref, wm_ref, x_ref, g_ref, o_ref, hg_ref, ssq_ref):
    acc = jnp.dot(ro_ref[...], wr_ref[...], preferred_element_type=F32)
    acc = acc + jnp.dot(mo_ref[...], wm_ref[...], preferred_element_type=F32)
    _emit_residual(x_ref[...] + acc, g_ref, o_ref, hg_ref, ssq_ref)


def out_projection(ro, mo, w_o, x, g_next, *, tm, tn):
    T, kr = ro.shape
    km = mo.shape[1]
    assert kr == km
    N = w_o.shape[1]
    est = (4 * _nbytes((tm, kr), BF16) + 4 * _nbytes((kr, tn), BF16)
           + 7 * _nbytes((tm, tn), F32))
    out_specs, out_shape = _residual_out(T, N, tm, tn)
    return pl.pallas_call(
        _out_proj_kernel,
        grid=(T // tm, N // tn),
        in_specs=[pl.BlockSpec((tm, kr), lambda i, j: (i, 0)),
                  pl.BlockSpec((tm, km), lambda i, j: (i, 0)),
                  pl.BlockSpec((kr, tn), lambda i, j: (0, j)),
                  pl.BlockSpec((km, tn), lambda i, j: (1, j)),
                  pl.BlockSpec((tm, tn), lambda i, j: (i, j)),
                  pl.BlockSpec((1, tn), lambda i, j: (0, j))],
        out_specs=out_specs,
        out_shape=out_shape,
        compiler_params=pltpu.CompilerParams(
            dimension_semantics=("parallel", "arbitrary"),
            vmem_limit_bytes=_vmem_limit(est)),
        name="out_proj",
    )(ro, mo, w_o, w_o, x, g_next)


def _ffn_up_kernel(hg_ref, ssq_ref, wg_ref, wu_ref, cw_ref, cb_ref, o_ref, g_ref,
                   *, tiles_per_seq):
    tm = hg_ref.shape[0]
    halo = CONV_HALO

    @pl.when(pl.program_id(1) % tiles_per_seq == 0)
    def _():
        g_ref[0:halo, :] = jnp.zeros((halo, g_ref.shape[1]), F32)

    hg = hg_ref[...]
    r = _row_scale(ssq_ref, hg_ref.shape[1])
    g_ref[halo:halo + tm, :] = jnp.dot(hg, wg_ref[...], preferred_element_type=F32) * r
    up = jnp.dot(hg, wu_ref[...], preferred_element_type=F32) * r
    a = cb_ref[...]
    for j in range(CONV_WIDTH):
        shift = CONV_WIDTH - 1 - j
        a = a + g_ref[halo - shift:halo - shift + tm, :] * cw_ref[j:j + 1, :]
    o_ref[...] = (_silu(a) * up).astype(o_ref.dtype)
    g_ref[0:halo, :] = g_ref[tm:tm + halo, :]


def ffn_up(hg, ssq, w_gate, w_up, conv_w, conv_b, *, seq, tm, tf, col_start, n_cols, name):
    T, K = hg.shape
    assert n_cols % tf == 0 and col_start % tf == 0
    first = col_start // tf
    est = (2 * _nbytes((tm, K), BF16) + 4 * _nbytes((K, tf), BF16)
           + 2 * _nbytes((tm, tf), BF16) + 4 * _nbytes((tm + CONV_HALO, tf), F32)
           + 2 * _nbytes((tm, LANES), F32))
    return pl.pallas_call(
        functools.partial(_ffn_up_kernel, tiles_per_seq=seq // tm),
        grid=(n_cols // tf, T // tm),
        in_specs=[pl.BlockSpec((tm, K), lambda j, i: (i, 0)),
                  pl.BlockSpec((tm, 1), lambda j, i: (i, 0)),
                  pl.BlockSpec((K, tf), lambda j, i: (0, first + j)),
                  pl.BlockSpec((K, tf), lambda j, i: (0, first + j)),
                  pl.BlockSpec((CONV_WIDTH, tf), lambda j, i: (0, first + j)),
                  pl.BlockSpec((1, tf), lambda j, i: (0, first + j))],
        out_specs=pl.BlockSpec((tm, tf), lambda j, i: (i, j)),
        out_shape=jax.ShapeDtypeStruct((T, n_cols), BF16),
        scratch_shapes=[pltpu.VMEM((tm + CONV_HALO, tf), F32)],
        compiler_params=pltpu.CompilerParams(
            dimension_semantics=("arbitrary", "arbitrary"),
            vmem_limit_bytes=_vmem_limit(est)),
        name=name,
    )(hg, ssq, w_gate, w_up, conv_w, conv_b)


def _ffn_down_kernel(a_ref, b_ref, wa_ref, wb_ref, h_ref, g_ref, o_ref, hg_ref, ssq_ref):
    acc = jnp.dot(a_ref[...], wa_ref[...], preferred_element_type=F32)
    acc = acc + jnp.dot(b_ref[...], wb_ref[...], preferred_element_type=F32)
    _emit_residual(h_ref[...] + acc, g_ref, o_ref, hg_ref, ssq_ref)


def ffn_down(hidden_main, hidden_tail, w_down, h, g_next, *, tm, tn):
    T, ka = hidden_main.shape
    kb = hidden_tail.shape[1]
    assert ka % kb == 0 and ka + kb == w_down.shape[0]
    N = w_down.shape[1]
    est = (2 * _nbytes((tm, ka + kb), BF16) + 2 * _nbytes((ka + kb, tn), BF16)
           + 7 * _nbytes((tm, tn), F32))
    out_specs, out_shape = _residual_out(T, N, tm, tn)
    return pl.pallas_call(
        _ffn_down_kernel,
        grid=(T // tm, N // tn),
        in_specs=[pl.BlockSpec((tm, ka), lambda i, j: (i, 0)),
                  pl.BlockSpec((tm, kb), lambda i, j: (i, 0)),
                  pl.BlockSpec((ka, tn), lambda i, j: (0, j)),
                  pl.BlockSpec((kb, tn), lambda i, j: (ka // kb, j)),
                  pl.BlockSpec((tm, tn), lambda i, j: (i, j)),
                  pl.BlockSpec((1, tn), lambda i, j: (0, j))],
        out_specs=out_specs,
        out_shape=out_shape,
        compiler_params=pltpu.CompilerParams(
            dimension_semantics=("parallel", "arbitrary"),
            vmem_limit_bytes=_vmem_limit(est)),
        name="ffn_down",
    )(hidden_main, hidden_tail, w_down, w_down, h, g_next)


def _ple_kernel(hg_ref, ssq_ref, h_ref, wg_ref, p_ref, wp_ref, gf_ref, o_ref, ssq3_ref):
    j = pl.program_id(1)
    tm, D = hg_ref.shape
    tn = wg_ref.shape[1]

    z = jnp.dot(hg_ref[...], wg_ref[...], preferred_element_type=F32) * _row_scale(ssq_ref, D)
    gate = 1.0 / (1.0 + jnp.exp(-z))
    emb = jnp.dot(p_ref[...].astype(BF16), wp_ref[...], preferred_element_type=F32)
    h3 = h_ref[...] + gate * emb
    part = jnp.sum(h3 * h3, axis=-1, keepdims=True)

    @pl.when(j == 0)
    def _():
        ssq3_ref[...] = part

    @pl.when(j > 0)
    def _():
        ssq3_ref[...] += part

    for jj in range(D // tn):
        @pl.when(j == jj)
        def _(jj=jj):
            o_ref[:, jj * tn:(jj + 1) * tn] = h3

    @pl.when(j == pl.num_programs(1) - 1)
    def _():
        o_ref[...] = (o_ref[...] * _row_scale(ssq3_ref, D)) * gf_ref[...]


def ple_final(hg, ssq, h, w_gate, p, w_proj, g_final, *, tm, tn):
    T, D = h.shape
    P = p.shape[1]
    est = (2 * _nbytes((tm, D), F32) + 2 * _nbytes((tm, D), BF16)
           + 2 * _nbytes((D, tn), BF16) + 2 * _nbytes((P, tn), BF16)
           + 2 * _nbytes((tm, P), F32) + 6 * _nbytes((tm, tn), F32))
    return pl.pallas_call(
        _ple_kernel,
        grid=(T // tm, D // tn),
        in_specs=[pl.BlockSpec((tm, D), lambda i, j: (i, 0), pipeline_mode=pl.Buffered(1)),
                  pl.BlockSpec((tm, 1), lambda i, j: (i, 0)),
                  pl.BlockSpec((tm, tn), lambda i, j: (i, j)),
                  pl.BlockSpec((D, tn), lambda i, j: (0, j)),
                  pl.BlockSpec((tm, P), lambda i, j: (i, 0)),
                  pl.BlockSpec((P, tn), lambda i, j: (0, j)),
                  pl.BlockSpec((1, D), lambda i, j: (0, 0))],
        out_specs=pl.BlockSpec((tm, D), lambda i, j: (i, 0)),
        out_shape=jax.ShapeDtypeStruct((T, D), F32),
        scratch_shapes=[pltpu.VMEM((tm, 1), F32)],
        compiler_params=pltpu.CompilerParams(
            dimension_semantics=("parallel", "arbitrary"),
            vmem_limit_bytes=_vmem_limit(est)),
        name="ple_final",
    )(hg, ssq, h, w_gate, p, w_proj, g_final)


def _rope_tables(seq, dim):
    inv = 1.0 / (ROPE_BASE ** (jnp.arange(0, dim, 2, dtype=F32) / dim))
    ang = jnp.arange(seq, dtype=F32)[:, None] * inv[None, :]
    return jnp.cos(ang), jnp.sin(ang)


def _tile_config(seq):
    return dict(
        prenorm_tm=min(512, seq),
        in_proj=dict(tm=min(1024, seq), tn=768),
        retention_rows=min(2048, seq),
        mla_proj_tm=min(512, seq),
        attn_tile=min(512, seq),
        attn_heads=4,
        out_proj=dict(tm=min(1024, seq), tn=512),
        ffn_up=dict(tm=min(1024, seq), tf=512),
        ffn_down=dict(tm=min(512, seq), tn=512),
        ple=dict(tm=min(512, seq), tn=1024),
    )


def _layer(h, p_i, w_in, g_attn, g_q_lora, g_kv_lora, w_uq, w_ukv, w_o, g_ffn,
           w_ffn_gate, w_ffn_up, conv_w, conv_b, w_ffn_down, g_ple, w_ple_gate,
           w_ple_proj, g_out, *, batch, seq):
    cfg = _tile_config(seq)
    D = h.shape[1]
    ret_w = RET_HEADS * RET_HEAD_DIM
    q_lora = w_uq.shape[0]
    kv_lora = w_ukv.shape[0]
    d_ff = w_ffn_gate.shape[1]

    in_w = w_in.shape[1]
    tn_in = cfg["in_proj"]["tn"]
    in_pad = pl.cdiv(in_w, tn_in) * tn_in
    xg, ssq0 = prenorm(h, g_attn.reshape(1, D), tm=cfg["prenorm_tm"])
    proj = in_projection(xg, ssq0, w_in.T, n_out=in_pad, out_dtype=BF16, **cfg["in_proj"])

    log_g = jnp.log1p(-jnp.exp2(-5.0 - jnp.arange(RET_HEADS, dtype=F32)))
    cos_r, sin_r = _rope_tables(seq, RET_HEAD_DIM)
    ro = retention_group(proj, log_g, cos_r, sin_r, batch=batch, seq=seq,
                         rows=cfg["retention_rows"])

    hq = MLA_NOPE + MLA_ROPE
    wq = w_uq.reshape(q_lora, MLA_HEADS, hq)
    wq = jnp.pad(wq, ((0, 0), (0, 0), (0, MLA_QK_PAD - hq)))
    wq = wq.reshape(q_lora, MLA_HEADS * MLA_QK_PAD).astype(BF16)
    wkv = w_ukv.reshape(kv_lora, MLA_HEADS, MLA_NOPE + MLA_V)
    wk = wkv[:, :, :MLA_NOPE].reshape(kv_lora, MLA_HEADS * MLA_NOPE).astype(BF16)
    wvt = wkv[:, :, MLA_NOPE:].reshape(kv_lora, MLA_HEADS * MLA_V).T.astype(BF16)
    cos_m, sin_m = _rope_tables(seq, MLA_ROPE)
    zeros = jnp.zeros_like(cos_m)
    fill = jnp.zeros((seq, LANES - MLA_ROPE), F32)
    rope_c = jnp.concatenate([cos_m, cos_m, fill], axis=1)
    rope_n = jnp.concatenate([-sin_m, zeros, fill], axis=1)
    rope_p = jnp.concatenate([zeros, sin_m, fill], axis=1)
    cq_off = 4 * ret_w
    ckv_off = cq_off + q_lora
    kr_off = ckv_off + kv_lora
    assert cq_off % q_lora == 0 and ckv_off % kv_lora == 0 and kr_off % (2 * LANES) == 0
    assert in_pad - kr_off >= 2 * LANES
    qc, kc, vt = mla_projections(
        proj, g_q_lora.reshape(1, q_lora), g_kv_lora.reshape(1, kv_lora), wq, wk, wvt,
        rope_c, rope_n, rope_p, seq=seq, tm=cfg["mla_proj_tm"], kv_tile=cfg["attn_tile"],
        cq_blk=cq_off // q_lora, ckv_blk=ckv_off // kv_lora, kr_blk=kr_off // (2 * LANES))
    mo = mla_attention(qc, kc, vt, batch=batch, seq=seq, tile=cfg["attn_tile"],
                       heads=cfg["attn_heads"])

    h1, hg1, ssq1 = out_projection(ro, mo, w_o.astype(BF16), h, g_ffn.reshape(1, D),
                                   **cfg["out_proj"])

    up_args = (hg1, ssq1, w_ffn_gate.astype(BF16), w_ffn_up.astype(BF16), conv_w,
               conv_b.reshape(1, d_ff))
    tf = cfg["ffn_up"]["tf"]
    n_main = d_ff // tf * tf
    hidden_main = ffn_up(*up_args, seq=seq, tm=cfg["ffn_up"]["tm"], tf=tf, col_start=0,
                         n_cols=n_main, name="ffn_up")
    hidden_tail = ffn_up(*up_args, seq=seq, tm=cfg["ffn_up"]["tm"], tf=d_ff - n_main,
                         col_start=n_main, n_cols=d_ff - n_main, name="ffn_up_tail")
    h2, hg2, ssq2 = ffn_down(hidden_main, hidden_tail, w_ffn_down.astype(BF16), h1,
                             g_ple.reshape(1, D), **cfg["ffn_down"])

    return ple_final(hg2, ssq2, h2, w_ple_gate.astype(BF16), p_i, w_ple_proj.astype(BF16),
                     g_out.reshape(1, D), **cfg["ple"])


def kernel(x, p, w_in, g_attn, g_q_lora, g_kv_lora, w_uq, w_ukv, w_o, g_ffn, w_ffn_gate,
           w_ffn_up, conv_w, conv_b, w_ffn_down, g_ple, w_ple_gate, w_ple_proj, g_final):
    B, S, D = x.shape
    depth = p.shape[0]
    assert depth == 1, "the final RMSNorm is fused into the layer's last kernel"
    h = x.reshape(B * S, D)
    out = _layer(h, p[0].reshape(B * S, -1), w_in[0], g_attn[0], g_q_lora[0], g_kv_lora[0],
                 w_uq[0], w_ukv[0], w_o[0], g_ffn[0], w_ffn_gate[0], w_ffn_up[0], conv_w[0],
                 conv_b[0], w_ffn_down[0], g_ple[0], w_ple_gate[0], w_ple_proj[0], g_final,
                 batch=B, seq=S)
    return out.reshape(B, S, D)
```

```python
import functools
import math

import jax
import jax.numpy as jnp
from jax import lax
from jax.experimental import pallas as pl
from jax.experimental.pallas import tpu as pltpu

F32 = jnp.float32
BF16 = jnp.bfloat16

EPS = 1e-6
ROPE_BASE = 10000.0
CHUNK = 64
CHUNK_SHIFT = 6
RET_HEADS = 8
RET_HEAD_DIM = 256
MLA_HEADS = 16
MLA_NOPE = 128
MLA_ROPE = 64
MLA_V = 128
MLA_QK_PAD = 256
CONV_WIDTH = 3
RET_BLOCK = 256
LANES = 128
CONV_HALO = 8
MIB = 1024 * 1024
VMEM_LIMIT_CAP = 60 * MIB


def _vmem_limit(nbytes):
    return int(min(VMEM_LIMIT_CAP, nbytes + 16 * MIB))


def _nbytes(shape, dtype):
    return math.prod(shape) * jnp.dtype(dtype).itemsize


def _rms_rows(x_ref, g_ref, o_ref, n_rows, row_chunk=16, unroll=4):
    g = g_ref[...]

    def body(c, carry):
        r = pl.multiple_of(c * row_chunk, row_chunk)
        x = x_ref[pl.ds(r, row_chunk), :].astype(F32)
        ms = jnp.mean(x * x, axis=-1, keepdims=True)
        o_ref[pl.ds(r, row_chunk), :] = ((x * lax.rsqrt(ms + EPS)) * g).astype(o_ref.dtype)
        return carry

    lax.fori_loop(0, n_rows // row_chunk, body, 0, unroll=unroll)


def _row_scale(ssq_ref, width):
    return lax.rsqrt(ssq_ref[...] * (1.0 / width) + EPS)


def _silu(x):
    return x * (1.0 / (1.0 + jnp.exp(-x)))


def _prenorm_kernel(x_ref, g_ref, xg_ref, ssq_ref, *, row_chunk):
    g = g_ref[...]

    def body(c, carry):
        rows = pl.ds(pl.multiple_of(c * row_chunk, row_chunk), row_chunk)
        x = x_ref[rows, :]
        ssq_ref[rows, :] = jnp.sum(x * x, axis=-1, keepdims=True)
        xg_ref[rows, :] = (x * g).astype(xg_ref.dtype)
        return carry

    lax.fori_loop(0, x_ref.shape[0] // row_chunk, body, 0, unroll=4)


def prenorm(x, g, *, tm):
    T, K = x.shape
    est = 2 * _nbytes((tm, K), F32) + 2 * _nbytes((tm, K), BF16)
    return pl.pallas_call(
        functools.partial(_prenorm_kernel, row_chunk=16),
        grid=(T // tm,),
        in_specs=[pl.BlockSpec((tm, K), lambda i: (i, 0)),
                  pl.BlockSpec((1, K), lambda i: (0, 0))],
        out_specs=[pl.BlockSpec((tm, K), lambda i: (i, 0)),
                   pl.BlockSpec((tm, 1), lambda i: (i, 0))],
        out_shape=[jax.ShapeDtypeStruct((T, K), BF16), jax.ShapeDtypeStruct((T, 1), F32)],
        compiler_params=pltpu.CompilerParams(
            dimension_semantics=("parallel",),
            vmem_limit_bytes=_vmem_limit(est)),
        name="prenorm",
    )(x, g)


def _in_proj_kernel(xg_ref, ssq_ref, wt_ref, o_ref, wb_ref, *, edge_rows):
    tn, K = wt_ref.shape

    @pl.when(pl.program_id(1) == 0)
    def _():
        w = wt_ref[...]
        if edge_rows:
            last = pl.program_id(0) == pl.num_programs(0) - 1
            valid = jnp.where(last, edge_rows, tn)
            row = lax.broadcasted_iota(jnp.int32, (tn, 1), 0)
            w = jnp.where(row < valid, w, 0.0)
        wb_ref[...] = w.astype(wb_ref.dtype)

    res = lax.dot_general(xg_ref[...], wb_ref[...], (((1,), (1,)), ((), ())),
                          preferred_element_type=F32)
    o_ref[...] = (res * _row_scale(ssq_ref, K)).astype(o_ref.dtype)


def in_projection(xg, ssq, w_t, *, n_out, tm, tn, out_dtype):
    T, K = xg.shape
    N = n_out
    assert N % tn == 0 and N - tn < w_t.shape[0] <= N
    est = (2 * _nbytes((tm, K), BF16) + 2 * _nbytes((tn, K), w_t.dtype)
           + _nbytes((tn, K), BF16) + 2 * _nbytes((tm, tn), out_dtype)
           + _nbytes((tm, tn), F32))
    return pl.pallas_call(
        functools.partial(_in_proj_kernel, edge_rows=w_t.shape[0] % tn),
        grid=(N // tn, T // tm),
        in_specs=[pl.BlockSpec((tm, K), lambda j, i: (i, 0)),
                  pl.BlockSpec((tm, 1), lambda j, i: (i, 0)),
                  pl.BlockSpec((tn, K), lambda j, i: (j, 0))],
        out_specs=pl.BlockSpec((tm, tn), lambda j, i: (i, j)),
        out_shape=jax.ShapeDtypeStruct((T, N), out_dtype),
        scratch_shapes=[pltpu.VMEM((tn, K), BF16)],
        compiler_params=pltpu.CompilerParams(
            dimension_semantics=("arbitrary", "arbitrary"),
            vmem_limit_bytes=_vmem_limit(est)),
        name="in_proj",
    )(xg, ssq, w_t)


def _retention_kernel(lg_ref, q_ref, k_ref, v_ref, gate_ref, cos_ref, sin_ref, o_ref,
                      state_ref, dmat_ref, qdec_ref, kdec_ref, sdec_ref, *, k_scale):
    L = RET_BLOCK
    dk = q_ref.shape[1]
    half = dk // 2
    lg = lg_ref[pl.program_id(1)]

    @pl.when(pl.program_id(2) == 0)
    def _init():
        state_ref[...] = jnp.zeros_like(state_ref)
        n = lax.broadcasted_iota(jnp.int32, (L, L), 0)
        m = lax.broadcasted_iota(jnp.int32, (L, L), 1)
        cn = n >> CHUNK_SHIFT
        cm = m >> CHUNK_SHIFT
        d = (n - m).astype(F32)
        expo = jnp.where(cn == cm, jnp.abs(d), d)
        visible = cm <= cn
        dmat_ref[...] = jnp.where(visible, jnp.exp(lg * jnp.where(visible, expo, 0.0)), 0.0)
        row = lax.broadcasted_iota(jnp.int32, (L, dk), 0).astype(F32)
        qdec_ref[...] = jnp.exp(lg * (row + 1.0))
        kdec_ref[...] = jnp.exp(lg * (float(L - 1) - row))
        sdec_ref[...] = jnp.exp(jnp.full(sdec_ref.shape, lg * float(L), F32))

    def rope(x, cos, sin):
        x1 = x[:, :half]
        x2 = x[:, half:]
        return jnp.concatenate([x1 * cos - x2 * sin, x2 * cos + x1 * sin], axis=1)

    for sub in range(q_ref.shape[0] // L):
        rows = pl.ds(sub * L, L)
        cos = cos_ref[rows, :]
        sin = sin_ref[rows, :]
        q = rope(q_ref[rows, :].astype(F32), cos, sin)
        k = rope(k_ref[rows, :].astype(F32), cos, sin) * k_scale
        v = v_ref[rows, :]
        qb = q.astype(BF16)
        kb = k.astype(BF16)
        scores = lax.dot_general(qb, kb, (((1,), (1,)), ((), ())),
                                 preferred_element_type=F32)
        state = state_ref[...]
        inter = jnp.dot((q * qdec_ref[...]).astype(BF16), state.astype(BF16),
                        preferred_element_type=F32)
        kd = (k * kdec_ref[...]).astype(BF16)
        state_ref[...] = state * sdec_ref[0:1, :] + lax.dot_general(
            kd, v, (((0,), (0,)), ((), ())), preferred_element_type=F32)
        scores = (scores * dmat_ref[...]).astype(BF16)
        out = jnp.dot(scores, v, preferred_element_type=F32) + inter
        mu = jnp.mean(out, axis=-1, keepdims=True)
        cen = out - mu
        var = jnp.mean(cen * cen, axis=-1, keepdims=True)
        gate = gate_ref[rows, :].astype(F32)
        o_ref[rows, :] = (_silu(gate) * (cen * lax.rsqrt(var + EPS))).astype(o_ref.dtype)


def retention_group(proj, log_g, cos, sin, *, batch, seq, rows):
    T = proj.shape[0]
    H, dk = RET_HEADS, RET_HEAD_DIM
    nblk = seq // rows

    def col(offset):
        return pl.BlockSpec((rows, dk), lambda b, h, r: (b * nblk + r, offset + h))

    tab = pl.BlockSpec((rows, dk // 2), lambda b, h, r: (r, 0))
    est = 10 * _nbytes((rows, dk), BF16) + 4 * _nbytes((rows, dk // 2), F32) \
        + 4 * _nbytes((RET_BLOCK, dk), F32)
    return pl.pallas_call(
        functools.partial(_retention_kernel, k_scale=dk ** -0.5),
        grid=(batch, H, nblk),
        in_specs=[pl.BlockSpec(memory_space=pltpu.SMEM),
                  col(0), col(H), col(2 * H), col(3 * H), tab, tab],
        out_specs=pl.BlockSpec((rows, dk), lambda b, h, r: (b * nblk + r, h)),
        out_shape=jax.ShapeDtypeStruct((T, H * dk), BF16),
        scratch_shapes=[pltpu.VMEM((dk, dk), F32),
                        pltpu.VMEM((RET_BLOCK, RET_BLOCK), F32),
                        pltpu.VMEM((RET_BLOCK, dk), F32),
                        pltpu.VMEM((RET_BLOCK, dk), F32),
                        pltpu.VMEM((8, dk), F32)],
        compiler_params=pltpu.CompilerParams(
            dimension_semantics=("parallel", "parallel", "arbitrary"),
            vmem_limit_bytes=_vmem_limit(est)),
        name="retention",
    )(log_g, proj, proj, proj, proj, cos, sin)


def _mla_proj_kernel(cq_ref, ckv_ref, kr_ref, gq_ref, gkv_ref, wq_ref, wk_ref, wvt_ref,
                     rc_ref, rn_ref, rp_ref, q_ref, k_ref, vt_ref, cqn_ref, ckvn_ref,
                     *, scale):
    tm = cq_ref.shape[0]
    _rms_rows(cq_ref, gq_ref, cqn_ref, tm)
    _rms_rows(ckv_ref, gkv_ref, ckvn_ref, tm)
    rc = rc_ref[...]
    rn = rn_ref[...]
    rp = rp_ref[...]

    def rope(x):
        return (x * rc + pltpu.roll(x, LANES - MLA_ROPE // 2, 1) * rn
                + pltpu.roll(x, MLA_ROPE // 2, 1) * rp)

    krp = rope(kr_ref[:, :LANES].astype(F32)).astype(k_ref.dtype)
    cqn = cqn_ref[...]
    ckvn = ckvn_ref[...]
    W = MLA_QK_PAD
    for h in range(MLA_HEADS):
        qh = jnp.dot(cqn, wq_ref[:, h * W:(h + 1) * W], preferred_element_type=F32) * scale
        q_ref[:, h * W:h * W + LANES] = qh[:, :LANES].astype(q_ref.dtype)
        q_ref[:, h * W + LANES:(h + 1) * W] = rope(qh[:, LANES:]).astype(q_ref.dtype)
    for c in range(MLA_HEADS // 2):
        kn = jnp.dot(ckvn, wk_ref[:, c * W:(c + 1) * W], preferred_element_type=F32)
        for s in range(2):
            h = 2 * c + s
            k_ref[:, h * W:h * W + LANES] = kn[:, s * LANES:(s + 1) * LANES].astype(k_ref.dtype)
            k_ref[:, h * W + LANES:(h + 1) * W] = krp
    vt_ref[0] = lax.dot_general(wvt_ref[...], ckvn, (((1,), (1,)), ((), ())),
                                preferred_element_type=F32).astype(vt_ref.dtype)


def mla_projections(proj, g_q, g_kv, wq, wk, wvt, rope_c, rope_n, rope_p, *, seq, tm,
                    kv_tile, cq_blk, ckv_blk, kr_blk):
    T = proj.shape[0]
    q_lora = wq.shape[0]
    kv_lora = wk.shape[0]
    nq = wq.shape[1]
    nv = wvt.shape[0]
    nblk = seq // tm
    per_tile = kv_tile // tm
    scale = (MLA_NOPE + MLA_ROPE) ** -0.5 * math.log2(math.e)
    const = lambda i: (0, 0)
    tab = pl.BlockSpec((tm, LANES), lambda i: (i % nblk, 0))
    est = (2 * (_nbytes(wq.shape, BF16) + _nbytes(wk.shape, BF16) + _nbytes(wvt.shape, BF16))
           + 2 * (2 * _nbytes((tm, nq), BF16) + _nbytes((tm, nv), BF16))
           + 3 * _nbytes((tm, q_lora + kv_lora + 2 * LANES), BF16)
           + 6 * _nbytes((tm, LANES), F32) + _nbytes((tm, nv), F32))
    return pl.pallas_call(
        functools.partial(_mla_proj_kernel, scale=scale),
        grid=(T // tm,),
        in_specs=[pl.BlockSpec((tm, q_lora), lambda i: (i, cq_blk)),
                  pl.BlockSpec((tm, kv_lora), lambda i: (i, ckv_blk)),
                  pl.BlockSpec((tm, 2 * LANES), lambda i: (i, kr_blk)),
                  pl.BlockSpec((1, q_lora), const),
                  pl.BlockSpec((1, kv_lora), const),
                  pl.BlockSpec(wq.shape, const),
                  pl.BlockSpec(wk.shape, const),
                  pl.BlockSpec(wvt.shape, const),
                  tab, tab, tab],
        out_specs=[pl.BlockSpec((tm, nq), lambda i: (i, 0)),
                   pl.BlockSpec((tm, nq), lambda i: (i, 0)),
                   pl.BlockSpec((1, nv, tm), lambda i: (i // per_tile, 0, i % per_tile))],
        out_shape=[jax.ShapeDtypeStruct((T, nq), BF16),
                   jax.ShapeDtypeStruct((T, nq), BF16),
                   jax.ShapeDtypeStruct((T // kv_tile, nv, kv_tile), BF16)],
        scratch_shapes=[pltpu.VMEM((tm, q_lora), BF16),
                        pltpu.VMEM((tm, kv_lora), BF16)],
        compiler_params=pltpu.CompilerParams(
            dimension_semantics=("parallel",),
            vmem_limit_bytes=_vmem_limit(est)),
        name="mla_proj",
    )(proj, proj, proj, g_q, g_kv, wq, wk, wvt, rope_c, rope_n, rope_p)


def _mla_attn_kernel(q_ref, k_ref, vt_ref, o_ref, m_ref, l_ref, acc_ref,
                     *, tile, heads, q_tiles):
    def one(sub, carry):
        rows = pl.ds(pl.multiple_of(sub * tile, tile), tile)
        _attn_query_tile(pl.program_id(2) * q_tiles + sub, q_ref.at[rows, :], k_ref, vt_ref,
                         o_ref.at[rows, :], m_ref, l_ref, acc_ref, tile=tile, heads=heads)
        return carry

    lax.fori_loop(0, q_tiles, one, 0)


def _attn_query_tile(qi, q_ref, k_ref, vt_ref, o_ref, m_ref, l_ref, acc_ref, *, tile, heads):
    W, dv = MLA_QK_PAD, MLA_V
    qs = [q_ref[:, h * W:(h + 1) * W] for h in range(heads)]
    m_ref[...] = jnp.full(m_ref.shape, -jnp.inf, F32)
    l_ref[...] = jnp.zeros(l_ref.shape, F32)
    acc_ref[...] = jnp.zeros(acc_ref.shape, F32)

    def process(tiles):
        scores = []
        for j, _ in tiles:
            start = pl.multiple_of(j * tile, tile)
            scores.append([
                lax.dot_general(k_ref[pl.ds(start, tile), h * W:(h + 1) * W], qs[h],
                                (((1,), (1,)), ((), ())), preferred_element_type=F32)
                for h in range(heads)])
        for t, (j, mask) in enumerate(tiles):
            for h in range(heads):
                s = scores[t][h]
                if mask is not None:
                    s = jnp.where(mask, s, -jnp.inf)
                m = m_ref[h]
                m_new = jnp.maximum(m, jnp.max(s, axis=0, keepdims=True))
                alpha = jnp.exp2(m - m_new)
                p = jnp.exp2(s - m_new)
                vt = vt_ref[j, h * dv:(h + 1) * dv, :]
                l_ref[h] = alpha * l_ref[h] + jnp.sum(p, axis=0, keepdims=True)
                acc_ref[h] = alpha * acc_ref[h] + jnp.dot(vt, p.astype(vt.dtype),
                                                         preferred_element_type=F32)
                m_ref[h] = m_new

    key_chunk = lax.broadcasted_iota(jnp.int32, (tile, tile), 0) >> CHUNK_SHIFT
    qry_chunk = lax.broadcasted_iota(jnp.int32, (tile, tile), 1) >> CHUNK_SHIFT
    diag_mask = key_chunk <= qry_chunk

    @pl.when(qi == 0)
    def _():
        process([(qi, diag_mask)])

    @pl.when(qi > 0)
    def _():
        lead = (qi + 1) % 2

        @pl.when(lead == 1)
        def _():
            process([(0, None)])

        def pair(i, carry):
            j = lead + 2 * i
            process([(j, None), (j + 1, None)])
            return carry

        lax.fori_loop(0, (qi + 1 - lead) // 2 - 1, pair, 0)
        process([(qi - 1, None), (qi, diag_mask)])

    for h in range(heads):
        o_ref[:, h * dv:(h + 1) * dv] = (acc_ref[h] * (1.0 / l_ref[h])).T.astype(o_ref.dtype)


def mla_attention(qc, kc, vt, *, batch, seq, tile, heads, q_tiles):
    T = qc.shape[0]
    H, W, dv = MLA_HEADS, MLA_QK_PAD, MLA_V
    nq = seq // tile
    steps = nq // q_tiles
    rows = q_tiles * tile
    est = (2 * heads * (_nbytes((seq, W), BF16) + _nbytes((seq, dv), BF16)
                        + _nbytes((rows, W), BF16) + _nbytes((rows, dv), BF16))
           + 4 * heads * _nbytes((tile, tile), F32))
    return pl.pallas_call(
        functools.partial(_mla_attn_kernel, tile=tile, heads=heads, q_tiles=q_tiles),
        grid=(batch, H // heads, steps),
        in_specs=[pl.BlockSpec((rows, heads * W), lambda b, h, i: (b * steps + i, h)),
                  pl.BlockSpec((seq, heads * W), lambda b, h, i: (b, h)),
                  pl.BlockSpec((nq, heads * dv, tile), lambda b, h, i: (b, h, 0))],
        out_specs=pl.BlockSpec((rows, heads * dv), lambda b, h, i: (b * steps + i, h)),
        out_shape=jax.ShapeDtypeStruct((T, H * dv), BF16),
        scratch_shapes=[pltpu.VMEM((heads, 1, tile), F32),
                        pltpu.VMEM((heads, 1, tile), F32),
                        pltpu.VMEM((heads, dv, tile), F32)],
        compiler_params=pltpu.CompilerParams(
            dimension_semantics=("parallel", "parallel", "arbitrary"),
            vmem_limit_bytes=_vmem_limit(est)),
        name="mla_attn",
    )(qc, kc, vt)


def _emit_residual(h, g_ref, o_ref, hg_ref, ssq_ref):
    o_ref[...] = h
    hg_ref[...] = (h * g_ref[...]).astype(hg_ref.dtype)
    part = jnp.sum(h * h, axis=-1, keepdims=True)

    @pl.when(pl.program_id(1) == 0)
    def _():
        ssq_ref[...] = part

    @pl.when(pl.program_id(1) > 0)
    def _():
        ssq_ref[...] += part


def _residual_out(T, N, tm, tn):
    specs = [pl.BlockSpec((tm, tn), lambda i, j: (i, j)),
             pl.BlockSpec((tm, tn), lambda i, j: (i, j)),
             pl.BlockSpec((tm, 1), lambda i, j: (i, 0))]
    shapes = [jax.ShapeDtypeStruct((T, N), F32), jax.ShapeDtypeStruct((T, N), BF16),
              jax.ShapeDtypeStruct((T, 1), F32)]
    return specs, shapes


def _out_proj_kernel(ro_ref, mo_ref, wr_ref, wm_ref, x_ref, g_ref, o_ref, hg_ref, ssq_ref):
    acc = jnp.dot(ro_ref[...], wr_ref[...], preferred_element_type=F32)
    acc = acc + jnp.dot(mo_ref[...], wm_ref[...], preferred_element_type=F32)
    _emit_residual(x_ref[...] + acc, g_ref, o_ref, hg_ref, ssq_ref)


def out_projection(ro, mo, w_o, x, g_next, *, tm, tn):
    T, kr = ro.shape
    km = mo.shape[1]
    assert kr == km
    N = w_o.shape[1]
    est = (4 * _nbytes((tm, kr), BF16) + 4 * _nbytes((kr, tn), BF16)
           + 7 * _nbytes((tm, tn), F32))
    out_specs, out_shape = _residual_out(T, N, tm, tn)
    return pl.pallas_call(
        _out_proj_kernel,
        grid=(T // tm, N // tn),
        in_specs=[pl.BlockSpec((tm, kr), lambda i, j: (i, 0)),
                  pl.BlockSpec((tm, km), lambda i, j: (i, 0)),
                  pl.BlockSpec((kr, tn), lambda i, j: (0, j)),
                  pl.BlockSpec((km, tn), lambda i, j: (1, j)),
                  pl.BlockSpec((tm, tn), lambda i, j: (i, j)),
                  pl.BlockSpec((1, tn), lambda i, j: (0, j))],
        out_specs=out_specs,
        out_shape=out_shape,
        compiler_params=pltpu.CompilerParams(
            dimension_semantics=("parallel", "arbitrary"),
            vmem_limit_bytes=_vmem_limit(est)),
        name="out_proj",
    )(ro, mo, w_o, w_o, x, g_next)


def _ffn_up_kernel(hg_ref, ssq_ref, wg_ref, wu_ref, cw_ref, cb_ref, o_ref, g_ref,
                   *, tiles_per_seq):
    tm = hg_ref.shape[0]
    halo = CONV_HALO

    @pl.when(pl.program_id(1) % tiles_per_seq == 0)
    def _():
        g_ref[0:halo, :] = jnp.zeros((halo, g_ref.shape[1]), F32)

    hg = hg_ref[...]
    r = _row_scale(ssq_ref, hg_ref.shape[1])
    g_ref[halo:halo + tm, :] = jnp.dot(hg, wg_ref[...], preferred_element_type=F32) * r
    up = jnp.dot(hg, wu_ref[...], preferred_element_type=F32) * r
    a = cb_ref[...]
    for j in range(CONV_WIDTH):
        shift = CONV_WIDTH - 1 - j
        a = a + g_ref[halo - shift:halo - shift + tm, :] * cw_ref[j:j + 1, :]
    o_ref[...] = (_silu(a) * up).astype(o_ref.dtype)
    g_ref[0:halo, :] = g_ref[tm:tm + halo, :]


def ffn_up(hg, ssq, w_gate, w_up, conv_w, conv_b, *, seq, tm, tf, col_start, n_cols, name):
    T, K = hg.shape
    assert n_cols % tf == 0 and col_start % tf == 0
    first = col_start // tf
    est = (2 * _nbytes((tm, K), BF16) + 4 * _nbytes((K, tf), BF16)
           + 2 * _nbytes((tm, tf), BF16) + 4 * _nbytes((tm + CONV_HALO, tf), F32)
           + 2 * _nbytes((tm, LANES), F32))
    return pl.pallas_call(
        functools.partial(_ffn_up_kernel, tiles_per_seq=seq // tm),
        grid=(n_cols // tf, T // tm),
        in_specs=[pl.BlockSpec((tm, K), lambda j, i: (i, 0)),
                  pl.BlockSpec((tm, 1), lambda j, i: (i, 0)),
                  pl.BlockSpec((K, tf), lambda j, i: (0, first + j)),
                  pl.BlockSpec((K, tf), lambda j, i: (0, first + j)),
                  pl.BlockSpec((CONV_WIDTH, tf), lambda j, i: (0, first + j)),
                  pl.BlockSpec((1, tf), lambda j, i: (0, first + j))],
        out_specs=pl.BlockSpec((tm, tf), lambda j, i: (i, j)),
        out_shape=jax.ShapeDtypeStruct((T, n_cols), BF16),
        scratch_shapes=[pltpu.VMEM((tm + CONV_HALO, tf), F32)],
        compiler_params=pltpu.CompilerParams(
            dimension_semantics=("arbitrary", "arbitrary"),
            vmem_limit_bytes=_vmem_limit(est)),
        name=name,
    )(hg, ssq, w_gate, w_up, conv_w, conv_b)


def _ffn_down_kernel(a_ref, b_ref, wa_ref, wb_ref, h_ref, g_ref, o_ref, hg_ref, ssq_ref):
    acc = jnp.dot(a_ref[...], wa_ref[...], preferred_element_type=F32)
    acc = acc + jnp.dot(b_ref[...], wb_ref[...], preferred_element_type=F32)
    _emit_residual(h_ref[...] + acc, g_ref, o_ref, hg_ref, ssq_ref)


def ffn_down(hidden_main, hidden_tail, w_down, h, g_next, *, tm, tn):
    T, ka = hidden_main.shape
    kb = hidden_tail.shape[1]
    assert ka % kb == 0 and ka + kb == w_down.shape[0]
    N = w_down.shape[1]
    est = (2 * _nbytes((tm, ka + kb), BF16) + 2 * _nbytes((ka + kb, tn), BF16)
           + 7 * _nbytes((tm, tn), F32))
    out_specs, out_shape = _residual_out(T, N, tm, tn)
    return pl.pallas_call(
        _ffn_down_kernel,
        grid=(T // tm, N // tn),
        in_specs=[pl.BlockSpec((tm, ka), lambda i, j: (i, 0)),
                  pl.BlockSpec((tm, kb), lambda i, j: (i, 0)),
                  pl.BlockSpec((ka, tn), lambda i, j: (0, j)),
                  pl.BlockSpec((kb, tn), lambda i, j: (ka // kb, j)),
                  pl.BlockSpec((tm, tn), lambda i, j: (i, j)),
                  pl.BlockSpec((1, tn), lambda i, j: (0, j))],
        out_specs=out_specs,
        out_shape=out_shape,
        compiler_params=pltpu.CompilerParams(
            dimension_semantics=("parallel", "arbitrary"),
            vmem_limit_bytes=_vmem_limit(est)),
        name="ffn_down",
    )(hidden_main, hidden_tail, w_down, w_down, h, g_next)


def _ple_kernel(hg_ref, ssq_ref, h_ref, wg_ref, p_ref, wp_ref, gf_ref, o_ref, ssq3_ref):
    j = pl.program_id(1)
    tm, D = hg_ref.shape
    tn = wg_ref.shape[1]

    z = jnp.dot(hg_ref[...], wg_ref[...], preferred_element_type=F32) * _row_scale(ssq_ref, D)
    gate = 1.0 / (1.0 + jnp.exp(-z))
    emb = jnp.dot(p_ref[...].astype(BF16), wp_ref[...], preferred_element_type=F32)
    h3 = h_ref[...] + gate * emb
    part = jnp.sum(h3 * h3, axis=-1, keepdims=True)

    @pl.when(j == 0)
    def _():
        ssq3_ref[...] = part

    @pl.when(j > 0)
    def _():
        ssq3_ref[...] += part

    for jj in range(D // tn):
        @pl.when(j == jj)
        def _(jj=jj):
            o_ref[:, jj * tn:(jj + 1) * tn] = h3

    @pl.when(j == pl.num_programs(1) - 1)
    def _():
        o_ref[...] = (o_ref[...] * _row_scale(ssq3_ref, D)) * gf_ref[...]


def ple_final(hg, ssq, h, w_gate, p, w_proj, g_final, *, tm, tn):
    T, D = h.shape
    P = p.shape[1]
    est = (2 * _nbytes((tm, D), F32) + 2 * _nbytes((tm, D), BF16)
           + 2 * _nbytes((D, tn), BF16) + 2 * _nbytes((P, tn), BF16)
           + 2 * _nbytes((tm, P), F32) + 6 * _nbytes((tm, tn), F32))
    return pl.pallas_call(
        _ple_kernel,
        grid=(T // tm, D // tn),
        in_specs=[pl.BlockSpec((tm, D), lambda i, j: (i, 0)),
                  pl.BlockSpec((tm, 1), lambda i, j: (i, 0)),
                  pl.BlockSpec((tm, tn), lambda i, j: (i, j)),
                  pl.BlockSpec((D, tn), lambda i, j: (0, j)),
                  pl.BlockSpec((tm, P), lambda i, j: (i, 0)),
                  pl.BlockSpec((P, tn), lambda i, j: (0, j)),
                  pl.BlockSpec((1, D), lambda i, j: (0, 0))],
        out_specs=pl.BlockSpec((tm, D), lambda i, j: (i, 0)),
        out_shape=jax.ShapeDtypeStruct((T, D), F32),
        scratch_shapes=[pltpu.VMEM((tm, 1), F32)],
        compiler_params=pltpu.CompilerParams(
            dimension_semantics=("parallel", "arbitrary"),
            vmem_limit_bytes=_vmem_limit(est)),
        name="ple_final",
    )(hg, ssq, h, w_gate, p, w_proj, g_final)


def _rope_tables(seq, dim):
    inv = 1.0 / (ROPE_BASE ** (jnp.arange(0, dim, 2, dtype=F32) / dim))
    ang = jnp.arange(seq, dtype=F32)[:, None] * inv[None, :]
    return jnp.cos(ang), jnp.sin(ang)


def _tile_config(seq):
    return dict(
        prenorm_tm=min(512, seq),
        in_proj=dict(tm=min(1024, seq), tn=768),
        retention_rows=min(2048, seq),
        mla_proj_tm=min(512, seq),
        attn_tile=min(512, seq),
        attn_heads=4,
        attn_q_tiles=min(2, seq // min(512, seq)),
        out_proj=dict(tm=min(1024, seq), tn=512),
        ffn_up=dict(tm=min(1024, seq), tf=512),
        ffn_down=dict(tm=min(512, seq), tn=512),
        ple=dict(tm=min(512, seq), tn=512),
    )


def _layer(h, p_i, w_in, g_attn, g_q_lora, g_kv_lora, w_uq, w_ukv, w_o, g_ffn,
           w_ffn_gate, w_ffn_up, conv_w, conv_b, w_ffn_down, g_ple, w_ple_gate,
           w_ple_proj, g_out, *, batch, seq):
    cfg = _tile_config(seq)
    D = h.shape[1]
    ret_w = RET_HEADS * RET_HEAD_DIM
    q_lora = w_uq.shape[0]
    kv_lora = w_ukv.shape[0]
    d_ff = w_ffn_gate.shape[1]

    in_w = w_in.shape[1]
    tn_in = cfg["in_proj"]["tn"]
    in_pad = pl.cdiv(in_w, tn_in) * tn_in
    xg, ssq0 = prenorm(h, g_attn.reshape(1, D), tm=cfg["prenorm_tm"])
    proj = in_projection(xg, ssq0, w_in.T, n_out=in_pad, out_dtype=BF16, **cfg["in_proj"])

    log_g = jnp.log1p(-jnp.exp2(-5.0 - jnp.arange(RET_HEADS, dtype=F32)))
    cos_r, sin_r = _rope_tables(seq, RET_HEAD_DIM)
    ro = retention_group(proj, log_g, cos_r, sin_r, batch=batch, seq=seq,
                         rows=cfg["retention_rows"])

    hq = MLA_NOPE + MLA_ROPE
    wq = w_uq.reshape(q_lora, MLA_HEADS, hq)
    wq = jnp.pad(wq, ((0, 0), (0, 0), (0, MLA_QK_PAD - hq)))
    wq = wq.reshape(q_lora, MLA_HEADS * MLA_QK_PAD).astype(BF16)
    wkv = w_ukv.reshape(kv_lora, MLA_HEADS, MLA_NOPE + MLA_V)
    wk = wkv[:, :, :MLA_NOPE].reshape(kv_lora, MLA_HEADS * MLA_NOPE).astype(BF16)
    wvt = wkv[:, :, MLA_NOPE:].reshape(kv_lora, MLA_HEADS * MLA_V).T.astype(BF16)
    cos_m, sin_m = _rope_tables(seq, MLA_ROPE)
    zeros = jnp.zeros_like(cos_m)
    fill = jnp.zeros((seq, LANES - MLA_ROPE), F32)
    rope_c = jnp.concatenate([cos_m, cos_m, fill], axis=1)
    rope_n = jnp.concatenate([-sin_m, zeros, fill], axis=1)
    rope_p = jnp.concatenate([zeros, sin_m, fill], axis=1)
    cq_off = 4 * ret_w
    ckv_off = cq_off + q_lora
    kr_off = ckv_off + kv_lora
    assert cq_off % q_lora == 0 and ckv_off % kv_lora == 0 and kr_off % (2 * LANES) == 0
    assert in_pad - kr_off >= 2 * LANES
    qc, kc, vt = mla_projections(
        proj, g_q_lora.reshape(1, q_lora), g_kv_lora.reshape(1, kv_lora), wq, wk, wvt,
        rope_c, rope_n, rope_p, seq=seq, tm=cfg["mla_proj_tm"], kv_tile=cfg["attn_tile"],
        cq_blk=cq_off // q_lora, ckv_blk=ckv_off // kv_lora, kr_blk=kr_off // (2 * LANES))
    mo = mla_attention(qc, kc, vt, batch=batch, seq=seq, tile=cfg["attn_tile"],
                       heads=cfg["attn_heads"], q_tiles=cfg["attn_q_tiles"])

    h1, hg1, ssq1 = out_projection(ro, mo, w_o.astype(BF16), h, g_ffn.reshape(1, D),
                                   **cfg["out_proj"])

    up_args = (hg1, ssq1, w_ffn_gate.astype(BF16), w_ffn_up.astype(BF16), conv_w,
               conv_b.reshape(1, d_ff))
    tf = cfg["ffn_up"]["tf"]
    n_main = d_ff // tf * tf
    hidden_main = ffn_up(*up_args, seq=seq, tm=cfg["ffn_up"]["tm"], tf=tf, col_start=0,
                         n_cols=n_main, name="ffn_up")
    hidden_tail = ffn_up(*up_args, seq=seq, tm=cfg["ffn_up"]["tm"], tf=d_ff - n_main,
                         col_start=n_main, n_cols=d_ff - n_main, name="ffn_up_tail")
    h2, hg2, ssq2 = ffn_down(hidden_main, hidden_tail, w_ffn_down.astype(BF16), h1,
                             g_ple.reshape(1, D), **cfg["ffn_down"])

    return ple_final(hg2, ssq2, h2, w_ple_gate.astype(BF16), p_i, w_ple_proj.astype(BF16),
                     g_out.reshape(1, D), **cfg["ple"])


def kernel(x, p, w_in, g_attn, g_q_lora, g_kv_lora, w_uq, w_ukv, w_o, g_ffn, w_ffn_gate,
           w_ffn_up, conv_w, conv_b, w_ffn_down, g_ple, w_ple_gate, w_ple_proj, g_final):
    B, S, D = x.shape
    depth = p.shape[0]
    assert depth == 1, "the final RMSNorm is fused into the layer's last kernel"
    h = x.reshape(B * S, D)
    out = _layer(h, p[0].reshape(B * S, -1), w_in[0], g_attn[0], g_q_lora[0], g_kv_lora[0],
                 w_uq[0], w_ukv[0], w_o[0], g_ffn[0], w_ffn_gate[0], w_ffn_up[0], conv_w[0],
                 conv_b[0], w_ffn_down[0], g_ple[0], w_ple_gate[0], w_ple_proj[0], g_final,
                 batch=B, seq=S)
    return out.reshape(B, S, D)
```

```python
import functools
import math

import jax
import jax.numpy as jnp
from jax import lax
from jax.experimental import pallas as pl
from jax.experimental.pallas import tpu as pltpu

F32 = jnp.float32
BF16 = jnp.bfloat16

EPS = 1e-6
ROPE_BASE = 10000.0
CHUNK = 64
CHUNK_SHIFT = 6
RET_HEADS = 8
RET_HEAD_DIM = 256
MLA_HEADS = 16
MLA_NOPE = 128
MLA_ROPE = 64
MLA_V = 128
MLA_QK_PAD = 256
CONV_WIDTH = 3
RET_BLOCK = 256
LANES = 128
CONV_HALO = 8
MIB = 1024 * 1024
VMEM_LIMIT_CAP = 60 * MIB


def _vmem_limit(nbytes):
    return int(min(VMEM_LIMIT_CAP, nbytes + 16 * MIB))


def _nbytes(shape, dtype):
    return math.prod(shape) * jnp.dtype(dtype).itemsize


def _rms_rows(x_ref, g_ref, o_ref, n_rows, row_chunk=16, unroll=4):
    g = g_ref[...]

    def body(c, carry):
        r = pl.multiple_of(c * row_chunk, row_chunk)
        x = x_ref[pl.ds(r, row_chunk), :].astype(F32)
        ms = jnp.mean(x * x, axis=-1, keepdims=True)
        o_ref[pl.ds(r, row_chunk), :] = ((x * lax.rsqrt(ms + EPS)) * g).astype(o_ref.dtype)
        return carry

    lax.fori_loop(0, n_rows // row_chunk, body, 0, unroll=unroll)


def _row_scale(ssq_ref, width):
    return lax.rsqrt(ssq_ref[...] * (1.0 / width) + EPS)


def _silu(x):
    return x * (1.0 / (1.0 + jnp.exp(-x)))


def _prenorm_kernel(x_ref, g_ref, xg_ref, ssq_ref, *, row_chunk):
    g = g_ref[...]

    def body(c, carry):
        rows = pl.ds(pl.multiple_of(c * row_chunk, row_chunk), row_chunk)
        x = x_ref[rows, :]
        ssq_ref[rows, :] = jnp.sum(x * x, axis=-1, keepdims=True)
        xg_ref[rows, :] = (x * g).astype(xg_ref.dtype)
        return carry

    lax.fori_loop(0, x_ref.shape[0] // row_chunk, body, 0, unroll=4)


def prenorm(x, g, *, tm):
    T, K = x.shape
    est = 2 * _nbytes((tm, K), F32) + 2 * _nbytes((tm, K), BF16)
    return pl.pallas_call(
        functools.partial(_prenorm_kernel, row_chunk=16),
        grid=(T // tm,),
        in_specs=[pl.BlockSpec((tm, K), lambda i: (i, 0)),
                  pl.BlockSpec((1, K), lambda i: (0, 0))],
        out_specs=[pl.BlockSpec((tm, K), lambda i: (i, 0)),
                   pl.BlockSpec((tm, 1), lambda i: (i, 0))],
        out_shape=[jax.ShapeDtypeStruct((T, K), BF16), jax.ShapeDtypeStruct((T, 1), F32)],
        compiler_params=pltpu.CompilerParams(
            dimension_semantics=("parallel",),
            vmem_limit_bytes=_vmem_limit(est)),
        name="prenorm",
    )(x, g)


def _in_proj_kernel(xg_ref, ssq_ref, wt_ref, o_ref, wb_ref, *, edge_rows):
    tn, K = wt_ref.shape

    @pl.when(pl.program_id(1) == 0)
    def _():
        w = wt_ref[...]
        if edge_rows:
            last = pl.program_id(0) == pl.num_programs(0) - 1
            valid = jnp.where(last, edge_rows, tn)
            row = lax.broadcasted_iota(jnp.int32, (tn, 1), 0)
            w = jnp.where(row < valid, w, 0.0)
        wb_ref[...] = w.astype(wb_ref.dtype)

    res = lax.dot_general(xg_ref[...], wb_ref[...], (((1,), (1,)), ((), ())),
                          preferred_element_type=F32)
    o_ref[...] = (res * _row_scale(ssq_ref, K)).astype(o_ref.dtype)


def in_projection(xg, ssq, w_t, *, n_out, tm, tn, out_dtype):
    T, K = xg.shape
    N = n_out
    assert N % tn == 0 and N - tn < w_t.shape[0] <= N
    est = (2 * _nbytes((tm, K), BF16) + 2 * _nbytes((tn, K), w_t.dtype)
           + _nbytes((tn, K), BF16) + 2 * _nbytes((tm, tn), out_dtype)
           + _nbytes((tm, tn), F32))
    return pl.pallas_call(
        functools.partial(_in_proj_kernel, edge_rows=w_t.shape[0] % tn),
        grid=(N // tn, T // tm),
        in_specs=[pl.BlockSpec((tm, K), lambda j, i: (i, 0)),
                  pl.BlockSpec((tm, 1), lambda j, i: (i, 0)),
                  pl.BlockSpec((tn, K), lambda j, i: (j, 0))],
        out_specs=pl.BlockSpec((tm, tn), lambda j, i: (i, j)),
        out_shape=jax.ShapeDtypeStruct((T, N), out_dtype),
        scratch_shapes=[pltpu.VMEM((tn, K), BF16)],
        compiler_params=pltpu.CompilerParams(
            dimension_semantics=("arbitrary", "arbitrary"),
            vmem_limit_bytes=_vmem_limit(est)),
        name="in_proj",
    )(xg, ssq, w_t)


def _retention_kernel(lg_ref, q_ref, k_ref, v_ref, gate_ref, cos_ref, sin_ref, o_ref,
                      state_ref, dmat_ref, qdec_ref, kdec_ref, sdec_ref, *, k_scale):
    L = RET_BLOCK
    dk = q_ref.shape[1]
    half = dk // 2
    lg = lg_ref[pl.program_id(1)]

    @pl.when(pl.program_id(2) == 0)
    def _init():
        state_ref[...] = jnp.zeros_like(state_ref)
        n = lax.broadcasted_iota(jnp.int32, (L, L), 0)
        m = lax.broadcasted_iota(jnp.int32, (L, L), 1)
        cn = n >> CHUNK_SHIFT
        cm = m >> CHUNK_SHIFT
        d = (n - m).astype(F32)
        expo = jnp.where(cn == cm, jnp.abs(d), d)
        visible = cm <= cn
        dmat_ref[...] = jnp.where(visible, jnp.exp(lg * jnp.where(visible, expo, 0.0)), 0.0)
        row = lax.broadcasted_iota(jnp.int32, (L, dk), 0).astype(F32)
        qdec_ref[...] = jnp.exp(lg * (row + 1.0))
        kdec_ref[...] = jnp.exp(lg * (float(L - 1) - row))
        sdec_ref[...] = jnp.exp(jnp.full(sdec_ref.shape, lg * float(L), F32))

    def rope(x, cos, sin):
        x1 = x[:, :half]
        x2 = x[:, half:]
        return jnp.concatenate([x1 * cos - x2 * sin, x2 * cos + x1 * sin], axis=1)

    for sub in range(q_ref.shape[0] // L):
        rows = pl.ds(sub * L, L)
        cos = cos_ref[rows, :]
        sin = sin_ref[rows, :]
        q = rope(q_ref[rows, :].astype(F32), cos, sin)
        k = rope(k_ref[rows, :].astype(F32), cos, sin) * k_scale
        v = v_ref[rows, :]
        qb = q.astype(BF16)
        kb = k.astype(BF16)
        scores = lax.dot_general(qb, kb, (((1,), (1,)), ((), ())),
                                 preferred_element_type=F32)
        state = state_ref[...]
        inter = jnp.dot((q * qdec_ref[...]).astype(BF16), state.astype(BF16),
                        preferred_element_type=F32)
        kd = (k * kdec_ref[...]).astype(BF16)
        state_ref[...] = state * sdec_ref[0:1, :] + lax.dot_general(
            kd, v, (((0,), (0,)), ((), ())), preferred_element_type=F32)
        scores = (scores * dmat_ref[...]).astype(BF16)
        out = jnp.dot(scores, v, preferred_element_type=F32) + inter
        mu = jnp.mean(out, axis=-1, keepdims=True)
        cen = out - mu
        var = jnp.mean(cen * cen, axis=-1, keepdims=True)
        gate = gate_ref[rows, :].astype(F32)
        o_ref[rows, :] = (_silu(gate) * (cen * lax.rsqrt(var + EPS))).astype(o_ref.dtype)


def retention_group(proj, log_g, cos, sin, *, batch, seq, rows):
    T = proj.shape[0]
    H, dk = RET_HEADS, RET_HEAD_DIM
    nblk = seq // rows

    def col(offset):
        return pl.BlockSpec((rows, dk), lambda b, h, r: (b * nblk + r, offset + h))

    tab = pl.BlockSpec((rows, dk // 2), lambda b, h, r: (r, 0))
    est = 10 * _nbytes((rows, dk), BF16) + 4 * _nbytes((rows, dk // 2), F32) \
        + 4 * _nbytes((RET_BLOCK, dk), F32)
    return pl.pallas_call(
        functools.partial(_retention_kernel, k_scale=dk ** -0.5),
        grid=(batch, H, nblk),
        in_specs=[pl.BlockSpec(memory_space=pltpu.SMEM),
                  col(0), col(H), col(2 * H), col(3 * H), tab, tab],
        out_specs=pl.BlockSpec((rows, dk), lambda b, h, r: (b * nblk + r, h)),
        out_shape=jax.ShapeDtypeStruct((T, H * dk), BF16),
        scratch_shapes=[pltpu.VMEM((dk, dk), F32),
                        pltpu.VMEM((RET_BLOCK, RET_BLOCK), F32),
                        pltpu.VMEM((RET_BLOCK, dk), F32),
                        pltpu.VMEM((RET_BLOCK, dk), F32),
                        pltpu.VMEM((8, dk), F32)],
        compiler_params=pltpu.CompilerParams(
            dimension_semantics=("parallel", "parallel", "arbitrary"),
            vmem_limit_bytes=_vmem_limit(est)),
        name="retention",
    )(log_g, proj, proj, proj, proj, cos, sin)


def _mla_proj_kernel(cq_ref, ckv_ref, kr_ref, gq_ref, gkv_ref, wq_ref, wk_ref, wvt_ref,
                     rc_ref, rn_ref, rp_ref, q_ref, k_ref, vt_ref, cqn_ref, ckvn_ref,
                     *, scale):
    tm = cq_ref.shape[0]
    _rms_rows(cq_ref, gq_ref, cqn_ref, tm)
    _rms_rows(ckv_ref, gkv_ref, ckvn_ref, tm)
    rc = rc_ref[...]
    rn = rn_ref[...]
    rp = rp_ref[...]

    def rope(x):
        return (x * rc + pltpu.roll(x, LANES - MLA_ROPE // 2, 1) * rn
                + pltpu.roll(x, MLA_ROPE // 2, 1) * rp)

    krp = rope(kr_ref[:, :LANES].astype(F32)).astype(k_ref.dtype)
    cqn = cqn_ref[...]
    ckvn = ckvn_ref[...]
    W = MLA_QK_PAD
    for h in range(MLA_HEADS):
        qh = jnp.dot(cqn, wq_ref[:, h * W:(h + 1) * W], preferred_element_type=F32) * scale
        q_ref[:, h * W:h * W + LANES] = qh[:, :LANES].astype(q_ref.dtype)
        q_ref[:, h * W + LANES:(h + 1) * W] = rope(qh[:, LANES:]).astype(q_ref.dtype)
    for c in range(MLA_HEADS // 2):
        kn = jnp.dot(ckvn, wk_ref[:, c * W:(c + 1) * W], preferred_element_type=F32)
        for s in range(2):
            h = 2 * c + s
            k_ref[:, h * W:h * W + LANES] = kn[:, s * LANES:(s + 1) * LANES].astype(k_ref.dtype)
            k_ref[:, h * W + LANES:(h + 1) * W] = krp
    vt_ref[0] = lax.dot_general(wvt_ref[...], ckvn, (((1,), (1,)), ((), ())),
                                preferred_element_type=F32).astype(vt_ref.dtype)


def mla_projections(proj, g_q, g_kv, wq, wk, wvt, rope_c, rope_n, rope_p, *, seq, tm,
                    kv_tile, cq_blk, ckv_blk, kr_blk):
    T = proj.shape[0]
    q_lora = wq.shape[0]
    kv_lora = wk.shape[0]
    nq = wq.shape[1]
    nv = wvt.shape[0]
    nblk = seq // tm
    per_tile = kv_tile // tm
    scale = (MLA_NOPE + MLA_ROPE) ** -0.5 * math.log2(math.e)
    const = lambda i: (0, 0)
    tab = pl.BlockSpec((tm, LANES), lambda i: (i % nblk, 0))
    est = (2 * (_nbytes(wq.shape, BF16) + _nbytes(wk.shape, BF16) + _nbytes(wvt.shape, BF16))
           + 2 * (2 * _nbytes((tm, nq), BF16) + _nbytes((tm, nv), BF16))
           + 3 * _nbytes((tm, q_lora + kv_lora + 2 * LANES), BF16)
           + 6 * _nbytes((tm, LANES), F32) + _nbytes((tm, nv), F32))
    return pl.pallas_call(
        functools.partial(_mla_proj_kernel, scale=scale),
        grid=(T // tm,),
        in_specs=[pl.BlockSpec((tm, q_lora), lambda i: (i, cq_blk)),
                  pl.BlockSpec((tm, kv_lora), lambda i: (i, ckv_blk)),
                  pl.BlockSpec((tm, 2 * LANES), lambda i: (i, kr_blk)),
                  pl.BlockSpec((1, q_lora), const),
                  pl.BlockSpec((1, kv_lora), const),
                  pl.BlockSpec(wq.shape, const),
                  pl.BlockSpec(wk.shape, const),
                  pl.BlockSpec(wvt.shape, const),
                  tab, tab, tab],
        out_specs=[pl.BlockSpec((tm, nq), lambda i: (i, 0)),
                   pl.BlockSpec((tm, nq), lambda i: (i, 0)),
                   pl.BlockSpec((1, nv, tm), lambda i: (i // per_tile, 0, i % per_tile))],
        out_shape=[jax.ShapeDtypeStruct((T, nq), BF16),
                   jax.ShapeDtypeStruct((T, nq), BF16),
                   jax.ShapeDtypeStruct((T // kv_tile, nv, kv_tile), BF16)],
        scratch_shapes=[pltpu.VMEM((tm, q_lora), BF16),
                        pltpu.VMEM((tm, kv_lora), BF16)],
        compiler_params=pltpu.CompilerParams(
            dimension_semantics=("parallel",),
            vmem_limit_bytes=_vmem_limit(est)),
        name="mla_proj",
    )(proj, proj, proj, g_q, g_kv, wq, wk, wvt, rope_c, rope_n, rope_p)


def _mla_attn_kernel(q_ref, k_ref, vt_ref, o_ref, m_ref, l_ref, acc_ref, *, tile, heads):
    qi = pl.program_id(2)
    W, dv = MLA_QK_PAD, MLA_V
    qs = [q_ref[:, h * W:(h + 1) * W] for h in range(heads)]
    m_ref[...] = jnp.full(m_ref.shape, -jnp.inf, F32)
    l_ref[...] = jnp.zeros(l_ref.shape, F32)
    acc_ref[...] = jnp.zeros(acc_ref.shape, F32)

    def process(tiles):
        scores = []
        for j, _ in tiles:
            start = pl.multiple_of(j * tile, tile)
            scores.append([
                lax.dot_general(k_ref[pl.ds(start, tile), h * W:(h + 1) * W], qs[h],
                                (((1,), (1,)), ((), ())), preferred_element_type=F32)
                for h in range(heads)])
        for t, (j, mask) in enumerate(tiles):
            for h in range(heads):
                s = scores[t][h]
                if mask is not None:
                    s = jnp.where(mask, s, -jnp.inf)
                m = m_ref[h]
                m_new = jnp.maximum(m, jnp.max(s, axis=0, keepdims=True))
                alpha = jnp.exp2(m - m_new)
                p = jnp.exp2(s - m_new)
                vt = vt_ref[j, h * dv:(h + 1) * dv, :]
                l_ref[h] = alpha * l_ref[h] + jnp.sum(p, axis=0, keepdims=True)
                acc_ref[h] = alpha * acc_ref[h] + jnp.dot(vt, p.astype(vt.dtype),
                                                         preferred_element_type=F32)
                m_ref[h] = m_new

    key_chunk = lax.broadcasted_iota(jnp.int32, (tile, tile), 0) >> CHUNK_SHIFT
    qry_chunk = lax.broadcasted_iota(jnp.int32, (tile, tile), 1) >> CHUNK_SHIFT
    diag_mask = key_chunk <= qry_chunk

    @pl.when(qi == 0)
    def _():
        process([(qi, diag_mask)])

    @pl.when(qi > 0)
    def _():
        lead = (qi + 1) % 2

        @pl.when(lead == 1)
        def _():
            process([(0, None)])

        def pair(i, carry):
            j = lead + 2 * i
            process([(j, None), (j + 1, None)])
            return carry

        lax.fori_loop(0, (qi + 1 - lead) // 2 - 1, pair, 0)
        process([(qi - 1, None), (qi, diag_mask)])

    for h in range(heads):
        o_ref[:, h * dv:(h + 1) * dv] = (acc_ref[h] * (1.0 / l_ref[h])).T.astype(o_ref.dtype)


def mla_attention(qc, kc, vt, *, batch, seq, tile, heads):
    T = qc.shape[0]
    H, W, dv = MLA_HEADS, MLA_QK_PAD, MLA_V
    nq = seq // tile
    est = (2 * heads * (_nbytes((seq, W), BF16) + _nbytes((seq, dv), BF16)
                        + _nbytes((tile, W), BF16) + _nbytes((tile, dv), BF16))
           + 4 * heads * _nbytes((tile, tile), F32))
    return pl.pallas_call(
        functools.partial(_mla_attn_kernel, tile=tile, heads=heads),
        grid=(batch, H // heads, nq),
        in_specs=[pl.BlockSpec((tile, heads * W), lambda b, h, i: (b * nq + i, h)),
                  pl.BlockSpec((seq, heads * W), lambda b, h, i: (b, h)),
                  pl.BlockSpec((nq, heads * dv, tile), lambda b, h, i: (b, h, 0))],
        out_specs=pl.BlockSpec((tile, heads * dv), lambda b, h, i: (b * nq + i, h)),
        out_shape=jax.ShapeDtypeStruct((T, H * dv), BF16),
        scratch_shapes=[pltpu.VMEM((heads, 1, tile), F32),
                        pltpu.VMEM((heads, 1, tile), F32),
                        pltpu.VMEM((heads, dv, tile), F32)],
        compiler_params=pltpu.CompilerParams(
            dimension_semantics=("parallel", "parallel", "arbitrary"),
            vmem_limit_bytes=_vmem_limit(est)),
        name="mla_attn",
    )(qc, kc, vt)


def _emit_residual(h, g_ref, o_ref, hg_ref, ssq_ref):
    o_ref[...] = h
    hg_ref[...] = (h * g_ref[...]).astype(hg_ref.dtype)
    part = jnp.sum(h * h, axis=-1, keepdims=True)

    @pl.when(pl.program_id(1) == 0)
    def _():
        ssq_ref[...] = part

    @pl.when(pl.program_id(1) > 0)
    def _():
        ssq_ref[...] += part


def _residual_out(T, N, tm, tn):
    specs = [pl.BlockSpec((tm, tn), lambda i, j: (i, j)),
             pl.BlockSpec((tm, tn), lambda i, j: (i, j)),
             pl.BlockSpec((tm, 1), lambda i, j: (i, 0))]
    shapes = [jax.ShapeDtypeStruct((T, N), F32), jax.ShapeDtypeStruct((T, N), BF16),
              jax.ShapeDtypeStruct((T, 1), F32)]
    return specs, shapes


def _out_proj_kernel(ro_ref, mo_ref, wr_ref, wm_ref, x_ref, g_ref, o_ref, hg_ref, ssq_ref):
    acc = jnp.dot(ro_ref[...], wr_ref[...], preferred_element_type=F32)
    acc = acc + jnp.dot(mo_ref[...], wm_ref[...], preferred_element_type=F32)
    _emit_residual(x_ref[...] + acc, g_ref, o_ref, hg_ref, ssq_ref)


def out_projection(ro, mo, w_o, x, g_next, *, tm, tn):
    T, kr = ro.shape
    km = mo.shape[1]
    assert kr == km
    N = w_o.shape[1]
    est = (4 * _nbytes((tm, kr), BF16) + 4 * _nbytes((kr, tn), BF16)
           + 7 * _nbytes((tm, tn), F32))
    out_specs, out_shape = _residual_out(T, N, tm, tn)
    return pl.pallas_call(
        _out_proj_kernel,
        grid=(T // tm, N // tn),
        in_specs=[pl.BlockSpec((tm, kr), lambda i, j: (i, 0)),
                  pl.BlockSpec((tm, km), lambda i, j: (i, 0)),
                  pl.BlockSpec((kr, tn), lambda i, j: (0, j)),
                  pl.BlockSpec((km, tn), lambda i, j: (1, j)),
                  pl.BlockSpec((tm, tn), lambda i, j: (i, j)),
                  pl.BlockSpec((1, tn), lambda i, j: (0, j))],
        out_specs=out_specs,
        out_shape=out_shape,
        compiler_params=pltpu.CompilerParams(
            dimension_semantics=("parallel", "arbitrary"),
            vmem_limit_bytes=_vmem_limit(est)),
        name="out_proj",
    )(ro, mo, w_o, w_o, x, g_next)


def _ffn_up_kernel(hg_ref, ssq_ref, wg_ref, wu_ref, cw_ref, cb_ref, o_ref, g_ref, wub_ref,
                   *, tiles_per_seq):
    tm = hg_ref.shape[0]
    halo = CONV_HALO

    @pl.when(pl.program_id(1) % tiles_per_seq == 0)
    def _():
        g_ref[0:halo, :] = jnp.zeros((halo, g_ref.shape[1]), F32)

    @pl.when(pl.program_id(1) == 0)
    def _():
        wub_ref[...] = wu_ref[...].astype(wub_ref.dtype)

    hg = hg_ref[...]
    r = _row_scale(ssq_ref, hg_ref.shape[1])
    g_ref[halo:halo + tm, :] = jnp.dot(hg, wg_ref[...], preferred_element_type=F32) * r
    up = jnp.dot(hg, wub_ref[...], preferred_element_type=F32) * r
    a = cb_ref[...]
    for j in range(CONV_WIDTH):
        shift = CONV_WIDTH - 1 - j
        a = a + g_ref[halo - shift:halo - shift + tm, :] * cw_ref[j:j + 1, :]
    o_ref[...] = (_silu(a) * up).astype(o_ref.dtype)
    g_ref[0:halo, :] = g_ref[tm:tm + halo, :]


def ffn_up(hg, ssq, w_gate, w_up, conv_w, conv_b, *, seq, tm, tf, col_start, n_cols, name):
    T, K = hg.shape
    assert n_cols % tf == 0 and col_start % tf == 0
    first = col_start // tf
    est = (2 * _nbytes((tm, K), BF16) + 2 * _nbytes((K, tf), w_gate.dtype)
           + 2 * _nbytes((K, tf), w_up.dtype) + _nbytes((K, tf), BF16)
           + 2 * _nbytes((tm, tf), BF16) + 4 * _nbytes((tm + CONV_HALO, tf), F32)
           + 2 * _nbytes((tm, LANES), F32))
    return pl.pallas_call(
        functools.partial(_ffn_up_kernel, tiles_per_seq=seq // tm),
        grid=(n_cols // tf, T // tm),
        in_specs=[pl.BlockSpec((tm, K), lambda j, i: (i, 0)),
                  pl.BlockSpec((tm, 1), lambda j, i: (i, 0)),
                  pl.BlockSpec((K, tf), lambda j, i: (0, first + j)),
                  pl.BlockSpec((K, tf), lambda j, i: (0, first + j)),
                  pl.BlockSpec((CONV_WIDTH, tf), lambda j, i: (0, first + j)),
                  pl.BlockSpec((1, tf), lambda j, i: (0, first + j))],
        out_specs=pl.BlockSpec((tm, tf), lambda j, i: (i, j)),
        out_shape=jax.ShapeDtypeStruct((T, n_cols), BF16),
        scratch_shapes=[pltpu.VMEM((tm + CONV_HALO, tf), F32),
                        pltpu.VMEM((K, tf), BF16)],
        compiler_params=pltpu.CompilerParams(
            dimension_semantics=("arbitrary", "arbitrary"),
            vmem_limit_bytes=_vmem_limit(est)),
        name=name,
    )(hg, ssq, w_gate, w_up, conv_w, conv_b)


def _ffn_down_kernel(a_ref, b_ref, wa_ref, wb_ref, h_ref, g_ref, o_ref, hg_ref, ssq_ref):
    acc = jnp.dot(a_ref[...], wa_ref[...], preferred_element_type=F32)
    acc = acc + jnp.dot(b_ref[...], wb_ref[...], preferred_element_type=F32)
    _emit_residual(h_ref[...] + acc, g_ref, o_ref, hg_ref, ssq_ref)


def ffn_down(hidden_main, hidden_tail, w_down, h, g_next, *, tm, tn):
    T, ka = hidden_main.shape
    kb = hidden_tail.shape[1]
    assert ka % kb == 0 and ka + kb == w_down.shape[0]
    N = w_down.shape[1]
    est = (2 * _nbytes((tm, ka + kb), BF16) + 2 * _nbytes((ka + kb, tn), BF16)
           + 7 * _nbytes((tm, tn), F32))
    out_specs, out_shape = _residual_out(T, N, tm, tn)
    return pl.pallas_call(
        _ffn_down_kernel,
        grid=(T // tm, N // tn),
        in_specs=[pl.BlockSpec((tm, ka), lambda i, j: (i, 0)),
                  pl.BlockSpec((tm, kb), lambda i, j: (i, 0)),
                  pl.BlockSpec((ka, tn), lambda i, j: (0, j)),
                  pl.BlockSpec((kb, tn), lambda i, j: (ka // kb, j)),
                  pl.BlockSpec((tm, tn), lambda i, j: (i, j)),
                  pl.BlockSpec((1, tn), lambda i, j: (0, j))],
        out_specs=out_specs,
        out_shape=out_shape,
        compiler_params=pltpu.CompilerParams(
            dimension_semantics=("parallel", "arbitrary"),
            vmem_limit_bytes=_vmem_limit(est)),
        name="ffn_down",
    )(hidden_main, hidden_tail, w_down, w_down, h, g_next)


def _ple_kernel(hg_ref, ssq_ref, h_ref, wg_ref, p_ref, wp_ref, gf_ref, o_ref, ssq3_ref):
    j = pl.program_id(1)
    tm, D = hg_ref.shape
    tn = wg_ref.shape[1]

    z = jnp.dot(hg_ref[...], wg_ref[...], preferred_element_type=F32) * _row_scale(ssq_ref, D)
    gate = 1.0 / (1.0 + jnp.exp(-z))
    emb = jnp.dot(p_ref[...].astype(BF16), wp_ref[...], preferred_element_type=F32)
    h3 = h_ref[...] + gate * emb
    part = jnp.sum(h3 * h3, axis=-1, keepdims=True)

    @pl.when(j == 0)
    def _():
        ssq3_ref[...] = part

    @pl.when(j > 0)
    def _():
        ssq3_ref[...] += part

    for jj in range(D // tn):
        @pl.when(j == jj)
        def _(jj=jj):
            o_ref[:, jj * tn:(jj + 1) * tn] = h3

    @pl.when(j == pl.num_programs(1) - 1)
    def _():
        o_ref[...] = (o_ref[...] * _row_scale(ssq3_ref, D)) * gf_ref[...]


def ple_final(hg, ssq, h, w_gate, p, w_proj, g_final, *, tm, tn):
    T, D = h.shape
    P = p.shape[1]
    est = (2 * _nbytes((tm, D), F32) + 2 * _nbytes((tm, D), BF16)
           + 2 * _nbytes((D, tn), BF16) + 2 * _nbytes((P, tn), BF16)
           + 2 * _nbytes((tm, P), F32) + 6 * _nbytes((tm, tn), F32))
    return pl.pallas_call(
        _ple_kernel,
        grid=(T // tm, D // tn),
        in_specs=[pl.BlockSpec((tm, D), lambda i, j: (i, 0)),
                  pl.BlockSpec((tm, 1), lambda i, j: (i, 0)),
                  pl.BlockSpec((tm, tn), lambda i, j: (i, j)),
                  pl.BlockSpec((D, tn), lambda i, j: (0, j)),
                  pl.BlockSpec((tm, P), lambda i, j: (i, 0)),
                  pl.BlockSpec((P, tn), lambda i, j: (0, j)),
                  pl.BlockSpec((1, D), lambda i, j: (0, 0))],
        out_specs=pl.BlockSpec((tm, D), lambda i, j: (i, 0)),
        out_shape=jax.ShapeDtypeStruct((T, D), F32),
        scratch_shapes=[pltpu.VMEM((tm, 1), F32)],
        compiler_params=pltpu.CompilerParams(
            dimension_semantics=("parallel", "arbitrary"),
            vmem_limit_bytes=_vmem_limit(est)),
        name="ple_final",
    )(hg, ssq, h, w_gate, p, w_proj, g_final)


def _rope_tables(seq, dim):
    inv = 1.0 / (ROPE_BASE ** (jnp.arange(0, dim, 2, dtype=F32) / dim))
    ang = jnp.arange(seq, dtype=F32)[:, None] * inv[None, :]
    return jnp.cos(ang), jnp.sin(ang)


def _tile_config(seq):
    return dict(
        prenorm_tm=min(512, seq),
        in_proj=dict(tm=min(1024, seq), tn=768),
        retention_rows=min(2048, seq),
        mla_proj_tm=min(512, seq),
        attn_tile=min(512, seq),
        attn_heads=4,
        out_proj=dict(tm=min(1024, seq), tn=512),
        ffn_up=dict(tm=min(1024, seq), tf=512),
        ffn_down=dict(tm=min(512, seq), tn=512),
        ple=dict(tm=min(512, seq), tn=512),
    )


def _layer(h, p_i, w_in, g_attn, g_q_lora, g_kv_lora, w_uq, w_ukv, w_o, g_ffn,
           w_ffn_gate, w_ffn_up, conv_w, conv_b, w_ffn_down, g_ple, w_ple_gate,
           w_ple_proj, g_out, *, batch, seq):
    cfg = _tile_config(seq)
    D = h.shape[1]
    ret_w = RET_HEADS * RET_HEAD_DIM
    q_lora = w_uq.shape[0]
    kv_lora = w_ukv.shape[0]
    d_ff = w_ffn_gate.shape[1]

    in_w = w_in.shape[1]
    tn_in = cfg["in_proj"]["tn"]
    in_pad = pl.cdiv(in_w, tn_in) * tn_in
    xg, ssq0 = prenorm(h, g_attn.reshape(1, D), tm=cfg["prenorm_tm"])
    proj = in_projection(xg, ssq0, w_in.T, n_out=in_pad, out_dtype=BF16, **cfg["in_proj"])

    log_g = jnp.log1p(-jnp.exp2(-5.0 - jnp.arange(RET_HEADS, dtype=F32)))
    cos_r, sin_r = _rope_tables(seq, RET_HEAD_DIM)
    ro = retention_group(proj, log_g, cos_r, sin_r, batch=batch, seq=seq,
                         rows=cfg["retention_rows"])

    hq = MLA_NOPE + MLA_ROPE
    wq = w_uq.reshape(q_lora, MLA_HEADS, hq)
    wq = jnp.pad(wq, ((0, 0), (0, 0), (0, MLA_QK_PAD - hq)))
    wq = wq.reshape(q_lora, MLA_HEADS * MLA_QK_PAD).astype(BF16)
    wkv = w_ukv.reshape(kv_lora, MLA_HEADS, MLA_NOPE + MLA_V)
    wk = wkv[:, :, :MLA_NOPE].reshape(kv_lora, MLA_HEADS * MLA_NOPE).astype(BF16)
    wvt = wkv[:, :, MLA_NOPE:].reshape(kv_lora, MLA_HEADS * MLA_V).T.astype(BF16)
    cos_m, sin_m = _rope_tables(seq, MLA_ROPE)
    zeros = jnp.zeros_like(cos_m)
    fill = jnp.zeros((seq, LANES - MLA_ROPE), F32)
    rope_c = jnp.concatenate([cos_m, cos_m, fill], axis=1)
    rope_n = jnp.concatenate([-sin_m, zeros, fill], axis=1)
    rope_p = jnp.concatenate([zeros, sin_m, fill], axis=1)
    cq_off = 4 * ret_w
    ckv_off = cq_off + q_lora
    kr_off = ckv_off + kv_lora
    assert cq_off % q_lora == 0 and ckv_off % kv_lora == 0 and kr_off % (2 * LANES) == 0
    assert in_pad - kr_off >= 2 * LANES
    qc, kc, vt = mla_projections(
        proj, g_q_lora.reshape(1, q_lora), g_kv_lora.reshape(1, kv_lora), wq, wk, wvt,
        rope_c, rope_n, rope_p, seq=seq, tm=cfg["mla_proj_tm"], kv_tile=cfg["attn_tile"],
        cq_blk=cq_off // q_lora, ckv_blk=ckv_off // kv_lora, kr_blk=kr_off // (2 * LANES))
    mo = mla_attention(qc, kc, vt, batch=batch, seq=seq, tile=cfg["attn_tile"],
                       heads=cfg["attn_heads"])

    h1, hg1, ssq1 = out_projection(ro, mo, w_o.astype(BF16), h, g_ffn.reshape(1, D),
                                   **cfg["out_proj"])

    up_args = (hg1, ssq1, w_ffn_gate.astype(BF16), w_ffn_up, conv_w,
               conv_b.reshape(1, d_ff))
    tf = cfg["ffn_up"]["tf"]
    n_main = d_ff // tf * tf
    hidden_main = ffn_up(*up_args, seq=seq, tm=cfg["ffn_up"]["tm"], tf=tf, col_start=0,
                         n_cols=n_main, name="ffn_up")
    hidden_tail = ffn_up(*up_args, seq=seq, tm=cfg["ffn_up"]["tm"], tf=d_ff - n_main,
                         col_start=n_main, n_cols=d_ff - n_main, name="ffn_up_tail")
    h2, hg2, ssq2 = ffn_down(hidden_main, hidden_tail, w_ffn_down.astype(BF16), h1,
                             g_ple.reshape(1, D), **cfg["ffn_down"])

    return ple_final(hg2, ssq2, h2, w_ple_gate.astype(BF16), p_i, w_ple_proj.astype(BF16),
                     g_out.reshape(1, D), **cfg["ple"])


def kernel(x, p, w_in, g_attn, g_q_lora, g_kv_lora, w_uq, w_ukv, w_o, g_ffn, w_ffn_gate,
           w_ffn_up, conv_w, conv_b, w_ffn_down, g_ple, w_ple_gate, w_ple_proj, g_final):
    B, S, D = x.shape
    depth = p.shape[0]
    assert depth == 1, "the final RMSNorm is fused into the layer's last kernel"
    h = x.reshape(B * S, D)
    out = _layer(h, p[0].reshape(B * S, -1), w_in[0], g_attn[0], g_q_lora[0], g_kv_lora[0],
                 w_uq[0], w_ukv[0], w_o[0], g_ffn[0], w_ffn_gate[0], w_ffn_up[0], conv_w[0],
                 conv_b[0], w_ffn_down[0], g_ple[0], w_ple_gate[0], w_ple_proj[0], g_final,
                 batch=B, seq=S)
    return out.reshape(B, S, D)
```

```python
import functools
import math

import jax
import jax.numpy as jnp
from jax import lax
from jax.experimental import pallas as pl
from jax.experimental.pallas import tpu as pltpu

F32 = jnp.float32
BF16 = jnp.bfloat16

EPS = 1e-6
ROPE_BASE = 10000.0
CHUNK = 64
CHUNK_SHIFT = 6
RET_HEADS = 8
RET_HEAD_DIM = 256
MLA_HEADS = 16
MLA_NOPE = 128
MLA_ROPE = 64
MLA_V = 128
MLA_V_ONES = 16
MLA_QK_PAD = 256
CONV_WIDTH = 3
RET_BLOCK = 256
LANES = 128
CONV_HALO = 8
MIB = 1024 * 1024
VMEM_LIMIT_CAP = 60 * MIB


def _vmem_limit(nbytes):
    return int(min(VMEM_LIMIT_CAP, nbytes + 16 * MIB))


def _nbytes(shape, dtype):
    return math.prod(shape) * jnp.dtype(dtype).itemsize


def _rms_rows(x_ref, g_ref, o_ref, n_rows, row_chunk=16, unroll=4):
    g = g_ref[...]

    def body(c, carry):
        r = pl.multiple_of(c * row_chunk, row_chunk)
        x = x_ref[pl.ds(r, row_chunk), :].astype(F32)
        ms = jnp.mean(x * x, axis=-1, keepdims=True)
        o_ref[pl.ds(r, row_chunk), :] = ((x * lax.rsqrt(ms + EPS)) * g).astype(o_ref.dtype)
        return carry

    lax.fori_loop(0, n_rows // row_chunk, body, 0, unroll=unroll)


def _row_scale(ssq_ref, width):
    return lax.rsqrt(ssq_ref[...] * (1.0 / width) + EPS)


def _silu(x):
    return x * (1.0 / (1.0 + jnp.exp(-x)))


def _prenorm_kernel(x_ref, g_ref, xg_ref, ssq_ref, *, row_chunk):
    g = g_ref[...]

    def body(c, carry):
        rows = pl.ds(pl.multiple_of(c * row_chunk, row_chunk), row_chunk)
        x = x_ref[rows, :]
        ssq_ref[rows, :] = jnp.sum(x * x, axis=-1, keepdims=True)
        xg_ref[rows, :] = (x * g).astype(xg_ref.dtype)
        return carry

    lax.fori_loop(0, x_ref.shape[0] // row_chunk, body, 0, unroll=4)


def prenorm(x, g, *, tm):
    T, K = x.shape
    est = 2 * _nbytes((tm, K), F32) + 2 * _nbytes((tm, K), BF16)
    return pl.pallas_call(
        functools.partial(_prenorm_kernel, row_chunk=16),
        grid=(T // tm,),
        in_specs=[pl.BlockSpec((tm, K), lambda i: (i, 0)),
                  pl.BlockSpec((1, K), lambda i: (0, 0))],
        out_specs=[pl.BlockSpec((tm, K), lambda i: (i, 0)),
                   pl.BlockSpec((tm, 1), lambda i: (i, 0))],
        out_shape=[jax.ShapeDtypeStruct((T, K), BF16), jax.ShapeDtypeStruct((T, 1), F32)],
        compiler_params=pltpu.CompilerParams(
            dimension_semantics=("parallel",),
            vmem_limit_bytes=_vmem_limit(est)),
        name="prenorm",
    )(x, g)


def _in_proj_kernel(xg_ref, ssq_ref, wt_ref, o_ref, wb_ref, *, edge_rows):
    tn, K = wt_ref.shape

    @pl.when(pl.program_id(1) == 0)
    def _():
        w = wt_ref[...]
        if edge_rows:
            last = pl.program_id(0) == pl.num_programs(0) - 1
            valid = jnp.where(last, edge_rows, tn)
            row = lax.broadcasted_iota(jnp.int32, (tn, 1), 0)
            w = jnp.where(row < valid, w, 0.0)
        wb_ref[...] = w.astype(wb_ref.dtype)

    res = lax.dot_general(xg_ref[...], wb_ref[...], (((1,), (1,)), ((), ())),
                          preferred_element_type=F32)
    o_ref[...] = (res * _row_scale(ssq_ref, K)).astype(o_ref.dtype)


def in_projection(xg, ssq, w_t, *, n_out, tm, tn, out_dtype):
    T, K = xg.shape
    N = n_out
    assert N % tn == 0 and N - tn < w_t.shape[0] <= N
    est = (2 * _nbytes((tm, K), BF16) + 2 * _nbytes((tn, K), w_t.dtype)
           + _nbytes((tn, K), BF16) + 2 * _nbytes((tm, tn), out_dtype)
           + _nbytes((tm, tn), F32))
    return pl.pallas_call(
        functools.partial(_in_proj_kernel, edge_rows=w_t.shape[0] % tn),
        grid=(N // tn, T // tm),
        in_specs=[pl.BlockSpec((tm, K), lambda j, i: (i, 0)),
                  pl.BlockSpec((tm, 1), lambda j, i: (i, 0)),
                  pl.BlockSpec((tn, K), lambda j, i: (j, 0))],
        out_specs=pl.BlockSpec((tm, tn), lambda j, i: (i, j)),
        out_shape=jax.ShapeDtypeStruct((T, N), out_dtype),
        scratch_shapes=[pltpu.VMEM((tn, K), BF16)],
        compiler_params=pltpu.CompilerParams(
            dimension_semantics=("arbitrary", "arbitrary"),
            vmem_limit_bytes=_vmem_limit(est)),
        name="in_proj",
    )(xg, ssq, w_t)


def _retention_kernel(lg_ref, q_ref, k_ref, v_ref, gate_ref, cos_ref, sin_ref, o_ref,
                      state_ref, dmat_ref, qdec_ref, kdec_ref, sdec_ref, *, k_scale):
    L = RET_BLOCK
    dk = q_ref.shape[1]
    half = dk // 2
    lg = lg_ref[pl.program_id(1)]

    @pl.when(pl.program_id(2) == 0)
    def _init():
        state_ref[...] = jnp.zeros_like(state_ref)
        n = lax.broadcasted_iota(jnp.int32, (L, L), 0)
        m = lax.broadcasted_iota(jnp.int32, (L, L), 1)
        cn = n >> CHUNK_SHIFT
        cm = m >> CHUNK_SHIFT
        d = (n - m).astype(F32)
        expo = jnp.where(cn == cm, jnp.abs(d), d)
        visible = cm <= cn
        dmat_ref[...] = jnp.where(visible, jnp.exp(lg * jnp.where(visible, expo, 0.0)), 0.0)
        row = lax.broadcasted_iota(jnp.int32, (L, dk), 0).astype(F32)
        qdec_ref[...] = jnp.exp(lg * (row + 1.0))
        kdec_ref[...] = jnp.exp(lg * (float(L - 1) - row))
        sdec_ref[...] = jnp.exp(jnp.full(sdec_ref.shape, lg * float(L), F32))

    def rope(x, cos, sin):
        x1 = x[:, :half]
        x2 = x[:, half:]
        return jnp.concatenate([x1 * cos - x2 * sin, x2 * cos + x1 * sin], axis=1)

    for sub in range(q_ref.shape[0] // L):
        rows = pl.ds(sub * L, L)
        cos = cos_ref[rows, :]
        sin = sin_ref[rows, :]
        q = rope(q_ref[rows, :].astype(F32), cos, sin)
        k = rope(k_ref[rows, :].astype(F32), cos, sin) * k_scale
        v = v_ref[rows, :]
        qb = q.astype(BF16)
        kb = k.astype(BF16)
        scores = lax.dot_general(qb, kb, (((1,), (1,)), ((), ())),
                                 preferred_element_type=F32)
        state = state_ref[...]
        inter = jnp.dot((q * qdec_ref[...]).astype(BF16), state.astype(BF16),
                        preferred_element_type=F32)
        kd = (k * kdec_ref[...]).astype(BF16)
        state_ref[...] = state * sdec_ref[0:1, :] + lax.dot_general(
            kd, v, (((0,), (0,)), ((), ())), preferred_element_type=F32)
        scores = (scores * dmat_ref[...]).astype(BF16)
        out = jnp.dot(scores, v, preferred_element_type=F32) + inter
        mu = jnp.mean(out, axis=-1, keepdims=True)
        cen = out - mu
        var = jnp.mean(cen * cen, axis=-1, keepdims=True)
        gate = gate_ref[rows, :].astype(F32)
        o_ref[rows, :] = (_silu(gate) * (cen * lax.rsqrt(var + EPS))).astype(o_ref.dtype)


def retention_group(proj, log_g, cos, sin, *, batch, seq, rows):
    T = proj.shape[0]
    H, dk = RET_HEADS, RET_HEAD_DIM
    nblk = seq // rows

    def col(offset):
        return pl.BlockSpec((rows, dk), lambda b, h, r: (b * nblk + r, offset + h))

    tab = pl.BlockSpec((rows, dk // 2), lambda b, h, r: (r, 0))
    est = 10 * _nbytes((rows, dk), BF16) + 4 * _nbytes((rows, dk // 2), F32) \
        + 4 * _nbytes((RET_BLOCK, dk), F32)
    return pl.pallas_call(
        functools.partial(_retention_kernel, k_scale=dk ** -0.5),
        grid=(batch, H, nblk),
        in_specs=[pl.BlockSpec(memory_space=pltpu.SMEM),
                  col(0), col(H), col(2 * H), col(3 * H), tab, tab],
        out_specs=pl.BlockSpec((rows, dk), lambda b, h, r: (b * nblk + r, h)),
        out_shape=jax.ShapeDtypeStruct((T, H * dk), BF16),
        scratch_shapes=[pltpu.VMEM((dk, dk), F32),
                        pltpu.VMEM((RET_BLOCK, RET_BLOCK), F32),
                        pltpu.VMEM((RET_BLOCK, dk), F32),
                        pltpu.VMEM((RET_BLOCK, dk), F32),
                        pltpu.VMEM((8, dk), F32)],
        compiler_params=pltpu.CompilerParams(
            dimension_semantics=("parallel", "parallel", "arbitrary"),
            vmem_limit_bytes=_vmem_limit(est)),
        name="retention",
    )(log_g, proj, proj, proj, proj, cos, sin)


def _mla_proj_kernel(cq_ref, ckv_ref, kr_ref, gq_ref, gkv_ref, wq_ref, wk_ref, wvt_ref,
                     rc_ref, rn_ref, rp_ref, q_ref, k_ref, vt_ref, cqn_ref, ckvn_ref,
                     *, scale):
    tm = cq_ref.shape[0]
    _rms_rows(cq_ref, gq_ref, cqn_ref, tm)
    _rms_rows(ckv_ref, gkv_ref, ckvn_ref, tm)
    rc = rc_ref[...]
    rn = rn_ref[...]
    rp = rp_ref[...]

    def rope(x):
        return (x * rc + pltpu.roll(x, LANES - MLA_ROPE // 2, 1) * rn
                + pltpu.roll(x, MLA_ROPE // 2, 1) * rp)

    krp = rope(kr_ref[:, :LANES].astype(F32)).astype(k_ref.dtype)
    cqn = cqn_ref[...]
    ckvn = ckvn_ref[...]
    W = MLA_QK_PAD
    for h in range(MLA_HEADS):
        qh = jnp.dot(cqn, wq_ref[:, h * W:(h + 1) * W], preferred_element_type=F32) * scale
        q_ref[:, h * W:h * W + LANES] = qh[:, :LANES].astype(q_ref.dtype)
        q_ref[:, h * W + LANES:(h + 1) * W] = rope(qh[:, LANES:]).astype(q_ref.dtype)
    for c in range(MLA_HEADS // 2):
        kn = jnp.dot(ckvn, wk_ref[:, c * W:(c + 1) * W], preferred_element_type=F32)
        for s in range(2):
            h = 2 * c + s
            k_ref[:, h * W:h * W + LANES] = kn[:, s * LANES:(s + 1) * LANES].astype(k_ref.dtype)
            k_ref[:, h * W + LANES:(h + 1) * W] = krp
    vt = lax.dot_general(wvt_ref[...], ckvn, (((1,), (1,)), ((), ())),
                         preferred_element_type=F32).astype(vt_ref.dtype)
    ones = jnp.ones((MLA_V_ONES, tm), vt_ref.dtype)
    for h in range(MLA_HEADS):
        base = h * (MLA_V + MLA_V_ONES)
        vt_ref[0, base:base + MLA_V, :] = vt[h * MLA_V:(h + 1) * MLA_V, :]
        vt_ref[0, base + MLA_V:base + MLA_V + MLA_V_ONES, :] = ones


def mla_projections(proj, g_q, g_kv, wq, wk, wvt, rope_c, rope_n, rope_p, *, seq, tm,
                    kv_tile, cq_blk, ckv_blk, kr_blk):
    T = proj.shape[0]
    q_lora = wq.shape[0]
    kv_lora = wk.shape[0]
    nq = wq.shape[1]
    nv = MLA_HEADS * (MLA_V + MLA_V_ONES)
    nblk = seq // tm
    per_tile = kv_tile // tm
    scale = (MLA_NOPE + MLA_ROPE) ** -0.5 * math.log2(math.e)
    const = lambda i: (0, 0)
    tab = pl.BlockSpec((tm, LANES), lambda i: (i % nblk, 0))
    est = (2 * (_nbytes(wq.shape, BF16) + _nbytes(wk.shape, BF16) + _nbytes(wvt.shape, BF16))
           + 2 * (2 * _nbytes((tm, nq), BF16) + _nbytes((tm, nv), BF16))
           + 3 * _nbytes((tm, q_lora + kv_lora + 2 * LANES), BF16)
           + 6 * _nbytes((tm, LANES), F32) + _nbytes((tm, nv), F32))
    return pl.pallas_call(
        functools.partial(_mla_proj_kernel, scale=scale),
        grid=(T // tm,),
        in_specs=[pl.BlockSpec((tm, q_lora), lambda i: (i, cq_blk)),
                  pl.BlockSpec((tm, kv_lora), lambda i: (i, ckv_blk)),
                  pl.BlockSpec((tm, 2 * LANES), lambda i: (i, kr_blk)),
                  pl.BlockSpec((1, q_lora), const),
                  pl.BlockSpec((1, kv_lora), const),
                  pl.BlockSpec(wq.shape, const),
                  pl.BlockSpec(wk.shape, const),
                  pl.BlockSpec(wvt.shape, const),
                  tab, tab, tab],
        out_specs=[pl.BlockSpec((tm, nq), lambda i: (i, 0)),
                   pl.BlockSpec((tm, nq), lambda i: (i, 0)),
                   pl.BlockSpec((1, nv, tm), lambda i: (i // per_tile, 0, i % per_tile))],
        out_shape=[jax.ShapeDtypeStruct((T, nq), BF16),
                   jax.ShapeDtypeStruct((T, nq), BF16),
                   jax.ShapeDtypeStruct((T // kv_tile, nv, kv_tile), BF16)],
        scratch_shapes=[pltpu.VMEM((tm, q_lora), BF16),
                        pltpu.VMEM((tm, kv_lora), BF16)],
        compiler_params=pltpu.CompilerParams(
            dimension_semantics=("parallel",),
            vmem_limit_bytes=_vmem_limit(est)),
        name="mla_proj",
    )(proj, proj, proj, g_q, g_kv, wq, wk, wvt, rope_c, rope_n, rope_p)


def _mla_attn_kernel(q_ref, k_ref, vt_ref, o_ref, m_ref, acc_ref, *, tile, heads):
    qi = pl.program_id(2)
    W, dv, dvx = MLA_QK_PAD, MLA_V, MLA_V + MLA_V_ONES
    qs = [q_ref[:, h * W:(h + 1) * W] for h in range(heads)]
    m_ref[...] = jnp.full(m_ref.shape, -jnp.inf, F32)
    acc_ref[...] = jnp.zeros(acc_ref.shape, F32)

    def process(tiles):
        scores = []
        for j, _ in tiles:
            start = pl.multiple_of(j * tile, tile)
            scores.append([
                lax.dot_general(k_ref[pl.ds(start, tile), h * W:(h + 1) * W], qs[h],
                                (((1,), (1,)), ((), ())), preferred_element_type=F32)
                for h in range(heads)])
        for t, (j, mask) in enumerate(tiles):
            for h in range(heads):
                s = scores[t][h]
                if mask is not None:
                    s = jnp.where(mask, s, -jnp.inf)
                m = m_ref[h]
                m_new = jnp.maximum(m, jnp.max(s, axis=0, keepdims=True))
                alpha = jnp.exp2(m - m_new)
                p = jnp.exp2(s - m_new)
                vt = vt_ref[j, h * dvx:(h + 1) * dvx, :]
                acc_ref[h] = alpha * acc_ref[h] + jnp.dot(vt, p.astype(vt.dtype),
                                                         preferred_element_type=F32)
                m_ref[h] = m_new

    key_chunk = lax.broadcasted_iota(jnp.int32, (tile, tile), 0) >> CHUNK_SHIFT
    qry_chunk = lax.broadcasted_iota(jnp.int32, (tile, tile), 1) >> CHUNK_SHIFT
    diag_mask = key_chunk <= qry_chunk

    @pl.when(qi == 0)
    def _():
        process([(qi, diag_mask)])

    @pl.when(qi > 0)
    def _():
        lead = (qi + 1) % 2

        @pl.when(lead == 1)
        def _():
            process([(0, None)])

        def pair(i, carry):
            j = lead + 2 * i
            process([(j, None), (j + 1, None)])
            return carry

        lax.fori_loop(0, (qi + 1 - lead) // 2 - 1, pair, 0)
        process([(qi - 1, None), (qi, diag_mask)])

    for h in range(heads):
        den = acc_ref[h, dv:dv + 1, :]
        o_ref[:, h * dv:(h + 1) * dv] = (acc_ref[h, 0:dv, :] * (1.0 / den)).T.astype(o_ref.dtype)


def mla_attention(qc, kc, vt, *, batch, seq, tile, heads):
    T = qc.shape[0]
    H, W, dv, dvx = MLA_HEADS, MLA_QK_PAD, MLA_V, MLA_V + MLA_V_ONES
    nq = seq // tile
    est = (2 * heads * (_nbytes((seq, W), BF16) + _nbytes((seq, dv), BF16)
                        + _nbytes((tile, W), BF16) + _nbytes((tile, dv), BF16))
           + 4 * heads * _nbytes((tile, tile), F32))
    return pl.pallas_call(
        functools.partial(_mla_attn_kernel, tile=tile, heads=heads),
        grid=(batch, H // heads, nq),
        in_specs=[pl.BlockSpec((tile, heads * W), lambda b, h, i: (b * nq + i, h)),
                  pl.BlockSpec((seq, heads * W), lambda b, h, i: (b, h)),
                  pl.BlockSpec((nq, heads * dvx, tile), lambda b, h, i: (b, h, 0))],
        out_specs=pl.BlockSpec((tile, heads * dv), lambda b, h, i: (b * nq + i, h)),
        out_shape=jax.ShapeDtypeStruct((T, H * dv), BF16),
        scratch_shapes=[pltpu.VMEM((heads, 1, tile), F32),
                        pltpu.VMEM((heads, dvx, tile), F32)],
        compiler_params=pltpu.CompilerParams(
            dimension_semantics=("parallel", "parallel", "arbitrary"),
            vmem_limit_bytes=_vmem_limit(est)),
        name="mla_attn",
    )(qc, kc, vt)


def _emit_residual(h, g_ref, o_ref, hg_ref, ssq_ref):
    o_ref[...] = h
    hg_ref[...] = (h * g_ref[...]).astype(hg_ref.dtype)
    part = jnp.sum(h * h, axis=-1, keepdims=True)

    @pl.when(pl.program_id(1) == 0)
    def _():
        ssq_ref[...] = part

    @pl.when(pl.program_id(1) > 0)
    def _():
        ssq_ref[...] += part


def _residual_out(T, N, tm, tn):
    specs = [pl.BlockSpec((tm, tn), lambda i, j: (i, j)),
             pl.BlockSpec((tm, tn), lambda i, j: (i, j)),
             pl.BlockSpec((tm, 1), lambda i, j: (i, 0))]
    shapes = [jax.ShapeDtypeStruct((T, N), F32), jax.ShapeDtypeStruct((T, N), BF16),
              jax.ShapeDtypeStruct((T, 1), F32)]
    return specs, shapes


def _out_proj_kernel(ro_ref, mo_ref, wr_ref, wm_ref, x_ref, g_ref, o_ref, hg_ref, ssq_ref):
    acc = jnp.dot(ro_ref[...], wr_ref[...], preferred_element_type=F32)
    acc = acc + jnp.dot(mo_ref[...], wm_ref[...], preferred_element_type=F32)
    _emit_residual(x_ref[...] + acc, g_ref, o_ref, hg_ref, ssq_ref)


def out_projection(ro, mo, w_o, x, g_next, *, tm, tn):
    T, kr = ro.shape
    km = mo.shape[1]
    assert kr == km
    N = w_o.shape[1]
    est = (4 * _nbytes((tm, kr), BF16) + 4 * _nbytes((kr, tn), BF16)
           + 7 * _nbytes((tm, tn), F32))
    out_specs, out_shape = _residual_out(T, N, tm, tn)
    return pl.pallas_call(
        _out_proj_kernel,
        grid=(T // tm, N // tn),
        in_specs=[pl.BlockSpec((tm, kr), lambda i, j: (i, 0)),
                  pl.BlockSpec((tm, km), lambda i, j: (i, 0)),
                  pl.BlockSpec((kr, tn), lambda i, j: (0, j)),
                  pl.BlockSpec((km, tn), lambda i, j: (1, j)),
                  pl.BlockSpec((tm, tn), lambda i, j: (i, j)),
                  pl.BlockSpec((1, tn), lambda i, j: (0, j))],
        out_specs=out_specs,
        out_shape=out_shape,
        compiler_params=pltpu.CompilerParams(
            dimension_semantics=("parallel", "arbitrary"),
            vmem_limit_bytes=_vmem_limit(est)),
        name="out_proj",
    )(ro, mo, w_o, w_o, x, g_next)


def _ffn_up_kernel(hg_ref, ssq_ref, wg_ref, wu_ref, cw_ref, cb_ref, o_ref, g_ref, wub_ref,
                   *, tiles_per_seq):
    tm = hg_ref.shape[0]
    halo = CONV_HALO

    @pl.when(pl.program_id(1) % tiles_per_seq == 0)
    def _():
        g_ref[0:halo, :] = jnp.zeros((halo, g_ref.shape[1]), F32)

    @pl.when(pl.program_id(1) == 0)
    def _():
        wub_ref[...] = wu_ref[...].astype(wub_ref.dtype)

    hg = hg_ref[...]
    r = _row_scale(ssq_ref, hg_ref.shape[1])
    g_ref[halo:halo + tm, :] = jnp.dot(hg, wg_ref[...], preferred_element_type=F32) * r
    up = jnp.dot(hg, wub_ref[...], preferred_element_type=F32) * r
    a = cb_ref[...]
    for j in range(CONV_WIDTH):
        shift = CONV_WIDTH - 1 - j
        a = a + g_ref[halo - shift:halo - shift + tm, :] * cw_ref[j:j + 1, :]
    o_ref[...] = (_silu(a) * up).astype(o_ref.dtype)
    g_ref[0:halo, :] = g_ref[tm:tm + halo, :]


def ffn_up(hg, ssq, w_gate, w_up, conv_w, conv_b, *, seq, tm, tf, col_start, n_cols, name):
    T, K = hg.shape
    assert n_cols % tf == 0 and col_start % tf == 0
    first = col_start // tf
    est = (2 * _nbytes((tm, K), BF16) + 2 * _nbytes((K, tf), w_gate.dtype)
           + 2 * _nbytes((K, tf), w_up.dtype) + _nbytes((K, tf), BF16)
           + 2 * _nbytes((tm, tf), BF16) + 4 * _nbytes((tm + CONV_HALO, tf), F32)
           + 2 * _nbytes((tm, LANES), F32))
    return pl.pallas_call(
        functools.partial(_ffn_up_kernel, tiles_per_seq=seq // tm),
        grid=(n_cols // tf, T // tm),
        in_specs=[pl.BlockSpec((tm, K), lambda j, i: (i, 0)),
                  pl.BlockSpec((tm, 1), lambda j, i: (i, 0)),
                  pl.BlockSpec((K, tf), lambda j, i: (0, first + j)),
                  pl.BlockSpec((K, tf), lambda j, i: (0, first + j)),
                  pl.BlockSpec((CONV_WIDTH, tf), lambda j, i: (0, first + j)),
                  pl.BlockSpec((1, tf), lambda j, i: (0, first + j))],
        out_specs=pl.BlockSpec((tm, tf), lambda j, i: (i, j)),
        out_shape=jax.ShapeDtypeStruct((T, n_cols), BF16),
        scratch_shapes=[pltpu.VMEM((tm + CONV_HALO, tf), F32),
                        pltpu.VMEM((K, tf), BF16)],
        compiler_params=pltpu.CompilerParams(
            dimension_semantics=("arbitrary", "arbitrary"),
            vmem_limit_bytes=_vmem_limit(est)),
        name=name,
    )(hg, ssq, w_gate, w_up, conv_w, conv_b)


def _ffn_down_kernel(a_ref, b_ref, wa_ref, wb_ref, h_ref, g_ref, o_ref, hg_ref, ssq_ref):
    acc = jnp.dot(a_ref[...], wa_ref[...], preferred_element_type=F32)
    acc = acc + jnp.dot(b_ref[...], wb_ref[...], preferred_element_type=F32)
    _emit_residual(h_ref[...] + acc, g_ref, o_ref, hg_ref, ssq_ref)


def ffn_down(hidden_main, hidden_tail, w_down, h, g_next, *, tm, tn):
    T, ka = hidden_main.shape
    kb = hidden_tail.shape[1]
    assert ka % kb == 0 and ka + kb == w_down.shape[0]
    N = w_down.shape[1]
    est = (2 * _nbytes((tm, ka + kb), BF16) + 2 * _nbytes((ka + kb, tn), BF16)
           + 7 * _nbytes((tm, tn), F32))
    out_specs, out_shape = _residual_out(T, N, tm, tn)
    return pl.pallas_call(
        _ffn_down_kernel,
        grid=(T // tm, N // tn),
        in_specs=[pl.BlockSpec((tm, ka), lambda i, j: (i, 0)),
                  pl.BlockSpec((tm, kb), lambda i, j: (i, 0)),
                  pl.BlockSpec((ka, tn), lambda i, j: (0, j)),
                  pl.BlockSpec((kb, tn), lambda i, j: (ka // kb, j)),
                  pl.BlockSpec((tm, tn), lambda i, j: (i, j)),
                  pl.BlockSpec((1, tn), lambda i, j: (0, j))],
        out_specs=out_specs,
        out_shape=out_shape,
        compiler_params=pltpu.CompilerParams(
            dimension_semantics=("parallel", "arbitrary"),
            vmem_limit_bytes=_vmem_limit(est)),
        name="ffn_down",
    )(hidden_main, hidden_tail, w_down, w_down, h, g_next)


def _ple_kernel(hg_ref, ssq_ref, h_ref, wg_ref, p_ref, wp_ref, gf_ref, o_ref, ssq3_ref):
    j = pl.program_id(1)
    tm, D = hg_ref.shape
    tn = wg_ref.shape[1]

    z = jnp.dot(hg_ref[...], wg_ref[...], preferred_element_type=F32) * _row_scale(ssq_ref, D)
    gate = 1.0 / (1.0 + jnp.exp(-z))
    emb = jnp.dot(p_ref[...].astype(BF16), wp_ref[...], preferred_element_type=F32)
    h3 = h_ref[...] + gate * emb
    part = jnp.sum(h3 * h3, axis=-1, keepdims=True)

    @pl.when(j == 0)
    def _():
        ssq3_ref[...] = part

    @pl.when(j > 0)
    def _():
        ssq3_ref[...] += part

    for jj in range(D // tn):
        @pl.when(j == jj)
        def _(jj=jj):
            o_ref[:, jj * tn:(jj + 1) * tn] = h3

    @pl.when(j == pl.num_programs(1) - 1)
    def _():
        o_ref[...] = (o_ref[...] * _row_scale(ssq3_ref, D)) * gf_ref[...]


def ple_final(hg, ssq, h, w_gate, p, w_proj, g_final, *, tm, tn):
    T, D = h.shape
    P = p.shape[1]
    est = (2 * _nbytes((tm, D), F32) + 2 * _nbytes((tm, D), BF16)
           + 2 * _nbytes((D, tn), BF16) + 2 * _nbytes((P, tn), BF16)
           + 2 * _nbytes((tm, P), F32) + 6 * _nbytes((tm, tn), F32))
    return pl.pallas_call(
        _ple_kernel,
        grid=(T // tm, D // tn),
        in_specs=[pl.BlockSpec((tm, D), lambda i, j: (i, 0)),
                  pl.BlockSpec((tm, 1), lambda i, j: (i, 0)),
                  pl.BlockSpec((tm, tn), lambda i, j: (i, j)),
                  pl.BlockSpec((D, tn), lambda i, j: (0, j)),
                  pl.BlockSpec((tm, P), lambda i, j: (i, 0)),
                  pl.BlockSpec((P, tn), lambda i, j: (0, j)),
                  pl.BlockSpec((1, D), lambda i, j: (0, 0))],
        out_specs=pl.BlockSpec((tm, D), lambda i, j: (i, 0)),
        out_shape=jax.ShapeDtypeStruct((T, D), F32),
        scratch_shapes=[pltpu.VMEM((tm, 1), F32)],
        compiler_params=pltpu.CompilerParams(
            dimension_semantics=("parallel", "arbitrary"),
            vmem_limit_bytes=_vmem_limit(est)),
        name="ple_final",
    )(hg, ssq, h, w_gate, p, w_proj, g_final)


def _rope_tables(seq, dim):
    inv = 1.0 / (ROPE_BASE ** (jnp.arange(0, dim, 2, dtype=F32) / dim))
    ang = jnp.arange(seq, dtype=F32)[:, None] * inv[None, :]
    return jnp.cos(ang), jnp.sin(ang)


def _tile_config(seq):
    return dict(
        prenorm_tm=min(512, seq),
        in_proj=dict(tm=min(1024, seq), tn=768),
        retention_rows=min(2048, seq),
        mla_proj_tm=min(512, seq),
        attn_tile=min(512, seq),
        attn_heads=4,
        out_proj=dict(tm=min(1024, seq), tn=512),
        ffn_up=dict(tm=min(1024, seq), tf=512),
        ffn_down=dict(tm=min(512, seq), tn=512),
        ple=dict(tm=min(512, seq), tn=512),
    )


def _layer(h, p_i, w_in, g_attn, g_q_lora, g_kv_lora, w_uq, w_ukv, w_o, g_ffn,
           w_ffn_gate, w_ffn_up, conv_w, conv_b, w_ffn_down, g_ple, w_ple_gate,
           w_ple_proj, g_out, *, batch, seq):
    cfg = _tile_config(seq)
    D = h.shape[1]
    ret_w = RET_HEADS * RET_HEAD_DIM
    q_lora = w_uq.shape[0]
    kv_lora = w_ukv.shape[0]
    d_ff = w_ffn_gate.shape[1]

    in_w = w_in.shape[1]
    tn_in = cfg["in_proj"]["tn"]
    in_pad = pl.cdiv(in_w, tn_in) * tn_in
    xg, ssq0 = prenorm(h, g_attn.reshape(1, D), tm=cfg["prenorm_tm"])
    proj = in_projection(xg, ssq0, w_in.T, n_out=in_pad, out_dtype=BF16, **cfg["in_proj"])

    log_g = jnp.log1p(-jnp.exp2(-5.0 - jnp.arange(RET_HEADS, dtype=F32)))
    cos_r, sin_r = _rope_tables(seq, RET_HEAD_DIM)
    ro = retention_group(proj, log_g, cos_r, sin_r, batch=batch, seq=seq,
                         rows=cfg["retention_rows"])

    hq = MLA_NOPE + MLA_ROPE
    wq = w_uq.reshape(q_lora, MLA_HEADS, hq)
    wq = jnp.pad(wq, ((0, 0), (0, 0), (0, MLA_QK_PAD - hq)))
    wq = wq.reshape(q_lora, MLA_HEADS * MLA_QK_PAD).astype(BF16)
    wkv = w_ukv.reshape(kv_lora, MLA_HEADS, MLA_NOPE + MLA_V)
    wk = wkv[:, :, :MLA_NOPE].reshape(kv_lora, MLA_HEADS * MLA_NOPE).astype(BF16)
    wvt = wkv[:, :, MLA_NOPE:].reshape(kv_lora, MLA_HEADS * MLA_V).T.astype(BF16)
    cos_m, sin_m = _rope_tables(seq, MLA_ROPE)
    zeros = jnp.zeros_like(cos_m)
    fill = jnp.zeros((seq, LANES - MLA_ROPE), F32)
    rope_c = jnp.concatenate([cos_m, cos_m, fill], axis=1)
    rope_n = jnp.concatenate([-sin_m, zeros, fill], axis=1)
    rope_p = jnp.concatenate([zeros, sin_m, fill], axis=1)
    cq_off = 4 * ret_w
    ckv_off = cq_off + q_lora
    kr_off = ckv_off + kv_lora
    assert cq_off % q_lora == 0 and ckv_off % kv_lora == 0 and kr_off % (2 * LANES) == 0
    assert in_pad - kr_off >= 2 * LANES
    qc, kc, vt = mla_projections(
        proj, g_q_lora.reshape(1, q_lora), g_kv_lora.reshape(1, kv_lora), wq, wk, wvt,
        rope_c, rope_n, rope_p, seq=seq, tm=cfg["mla_proj_tm"], kv_tile=cfg["attn_tile"],
        cq_blk=cq_off // q_lora, ckv_blk=ckv_off // kv_lora, kr_blk=kr_off // (2 * LANES))
    mo = mla_attention(qc, kc, vt, batch=batch, seq=seq, tile=cfg["attn_tile"],
                       heads=cfg["attn_heads"])

    h1, hg1, ssq1 = out_projection(ro, mo, w_o.astype(BF16), h, g_ffn.reshape(1, D),
                                   **cfg["out_proj"])

    up_args = (hg1, ssq1, w_ffn_gate.astype(BF16), w_ffn_up, conv_w,
               conv_b.reshape(1, d_ff))
    tf = cfg["ffn_up"]["tf"]
    n_main = d_ff // tf * tf
    hidden_main = ffn_up(*up_args, seq=seq, tm=cfg["ffn_up"]["tm"], tf=tf, col_start=0,
                         n_cols=n_main, name="ffn_up")
    hidden_tail = ffn_up(*up_args, seq=seq, tm=cfg["ffn_up"]["tm"], tf=d_ff - n_main,
                         col_start=n_main, n_cols=d_ff - n_main, name="ffn_up_tail")
    h2, hg2, ssq2 = ffn_down(hidden_main, hidden_tail, w_ffn_down.astype(BF16), h1,
                             g_ple.reshape(1, D), **cfg["ffn_down"])

    return ple_final(hg2, ssq2, h2, w_ple_gate.astype(BF16), p_i, w_ple_proj.astype(BF16),
                     g_out.reshape(1, D), **cfg["ple"])


def kernel(x, p, w_in, g_attn, g_q_lora, g_kv_lora, w_uq, w_ukv, w_o, g_ffn, w_ffn_gate,
           w_ffn_up, conv_w, conv_b, w_ffn_down, g_ple, w_ple_gate, w_ple_proj, g_final):
    B, S, D = x.shape
    depth = p.shape[0]
    assert depth == 1, "the final RMSNorm is fused into the layer's last kernel"
    h = x.reshape(B * S, D)
    out = _layer(h, p[0].reshape(B * S, -1), w_in[0], g_attn[0], g_q_lora[0], g_kv_lora[0],
                 w_uq[0], w_ukv[0], w_o[0], g_ffn[0], w_ffn_gate[0], w_ffn_up[0], conv_w[0],
                 conv_b[0], w_ffn_down[0], g_ple[0], w_ple_gate[0], w_ple_proj[0], g_final,
                 batch=B, seq=S)
    return out.reshape(B, S, D)
```
